```python
import math
import jax, jax.numpy as jnp
from jax import lax
import numpy as np

D_MODEL = 1024
BATCH = 16
SEQ = 2048
DEPTH = 4

N_MIXERS = 2
N_HEADS = 16
HEAD_DIM = 64
KV_RANK = 256
IDX_HEADS = 8
IDX_DIM = 64
INDEX_TOPK = 256
Q_BLOCK = 128
KV_HEADS = 2
WINDOW = 128
N_BUCKETS = 32
MAX_DISTANCE = 128
D_FF = -(-8 * D_MODEL // (3 * 256)) * 256
N_A = (DEPTH + 1) // 2
N_B = DEPTH // 2
RMS_EPS = 1e-6
NEG = -1e30

A_SPLITS = (N_HEADS * HEAD_DIM,
            N_HEADS * HEAD_DIM + KV_RANK,
            N_HEADS * HEAD_DIM + KV_RANK + IDX_HEADS * IDX_DIM,
            N_HEADS * HEAD_DIM + KV_RANK + IDX_HEADS * IDX_DIM + IDX_DIM)
A_IN = A_SPLITS[-1] + IDX_HEADS
B_IN = (N_HEADS + 2 * KV_HEADS) * HEAD_DIM

kernel_name = 'hybrid_dsa_swa_sink_adaln_trunk'


def rms_norm(x, g):
    xf = x.astype(jnp.float32)
    y = xf * lax.rsqrt(jnp.mean(xf * xf, axis=-1, keepdims=True) + RMS_EPS)
    return (y * g.astype(jnp.float32)).astype(x.dtype)


def layer_norm(x, g, b):
    xf = x.astype(jnp.float32)
    mu = jnp.mean(xf, axis=-1, keepdims=True)
    var = jnp.mean(jnp.square(xf - mu), axis=-1, keepdims=True)
    y = (xf - mu) * lax.rsqrt(var + RMS_EPS)
    return (y * g.astype(jnp.float32) + b.astype(jnp.float32)).astype(x.dtype)


def t5_bucket(dist):
    max_exact = N_BUCKETS // 2
    d = jnp.maximum(dist, 0)
    large = max_exact + (jnp.log(jnp.maximum(d, 1).astype(jnp.float32) / max_exact)
                         / math.log(MAX_DISTANCE / max_exact)
                         * (N_BUCKETS - max_exact)).astype(jnp.int32)
    large = jnp.minimum(large, N_BUCKETS - 1)
    return jnp.where(d < max_exact, d, large)


def modulate(h, shift, scale):
    return h * (1 + scale[:, None, :]) + shift[:, None, :]


def dsa_mla_mixer(h, w_in, kv_norm_g, w_uk, w_uv, idx_k_g, idx_k_b, w_out, rel_bias):
    B, L, _ = h.shape
    proj = h @ w_in
    q, ckv, qi, ki, wi = jnp.split(proj, A_SPLITS, axis=-1)
    q = q.reshape(B, L, N_HEADS, HEAD_DIM)
    ckv = rms_norm(ckv, kv_norm_g)
    qi = qi.reshape(B, L, IDX_HEADS, IDX_DIM)
    ki = layer_norm(ki, idx_k_g, idx_k_b)
    wi = wi * (IDX_HEADS ** -0.5 * IDX_DIM ** -0.5)
    topk = min(INDEX_TOPK, L // 4)
    nblk = L // Q_BLOCK
    key_pos = jnp.arange(L)

    def blocks(a):
        return a.reshape(B, nblk, Q_BLOCK, *a.shape[2:]).swapaxes(0, 1)

    def one_block(args):
        q_b, qi_b, wi_b, t_b = args
        rel = jax.nn.relu(jnp.einsum('bqhd,bsd->bqhs', qi_b, ki).astype(jnp.float32))
        score = jnp.einsum('bqhs,bqh->bqs', rel, wi_b.astype(jnp.float32))
        causal = key_pos[None, :] <= t_b[:, None]
        score = jnp.where(causal[None], score, NEG)
        _, sel = lax.top_k(score, topk)
        valid = sel <= t_b[None, :, None]
        kv_sel = jax.vmap(lambda cb, ib: cb[ib])(ckv, sel)
        q_abs = jnp.einsum('bqhd,hrd->bqhr', q_b, w_uk)
        logits = jnp.einsum('bqhr,bqkr->bqhk', q_abs, kv_sel).astype(jnp.float32) * HEAD_DIM ** -0.5
        bias = rel_bias[t5_bucket(t_b[None, :, None] - sel)]
        logits = logits + bias.astype(jnp.float32).transpose(0, 1, 3, 2)
        logits = jnp.where(valid[:, :, None, :], logits, NEG)
        p = jax.nn.softmax(logits, axis=-1).astype(kv_sel.dtype)
        return jnp.einsum('bqhk,bqkr->bqhr', p, kv_sel)

    t_blocks = jnp.arange(L).reshape(nblk, Q_BLOCK)
    o_lat = lax.map(one_block, (blocks(q), blocks(qi), blocks(wi), t_blocks))
    o_lat = o_lat.swapaxes(0, 1).reshape(B, L, N_HEADS, KV_RANK)
    o = jnp.einsum('blhr,hrd->blhd', o_lat, w_uv)
    return o.reshape(B, L, N_HEADS * HEAD_DIM) @ w_out


def swa_sink_mixer(h, w_in, b_in, sinks, w_out, b_out, rel_bias):
    B, L, _ = h.shape
    W = WINDOW
    nb = L // W
    G = N_HEADS // KV_HEADS
    proj = h @ w_in + b_in
    q, k, v = jnp.split(proj, (N_HEADS * HEAD_DIM, (N_HEADS + KV_HEADS) * HEAD_DIM), axis=-1)
    q = q.reshape(B, nb, W, KV_HEADS, G, HEAD_DIM)
    k = k.reshape(B, nb, W, KV_HEADS, HEAD_DIM)
    v = v.reshape(B, nb, W, KV_HEADS, HEAD_DIM)

    def with_prev(a):
        prev = jnp.pad(a[:, :-1], ((0, 0), (1, 0), (0, 0), (0, 0), (0, 0)))
        return jnp.concatenate([prev, a], axis=2)

    kb, vb = with_prev(k), with_prev(v)
    logits = jnp.einsum('bnqkgd,bnskd->bnkgqs', q, kb).astype(jnp.float32) * HEAD_DIM ** -0.5
    qpos = jnp.arange(W)
    kpos = jnp.arange(2 * W)
    dist = qpos[:, None] + W - kpos[None, :]
    allowed = (dist >= 0) & (dist < WINDOW)
    blk_valid = (jnp.arange(nb)[:, None] > 0) | (kpos[None, :] >= W)
    mask = allowed[None] & blk_valid[:, None, :]
    bias = rel_bias[t5_bucket(dist)].transpose(2, 0, 1).reshape(KV_HEADS, G, W, 2 * W)
    logits = logits + bias.astype(jnp.float32)[None, None]
    logits = jnp.where(mask[None, :, None, None], logits, NEG)
    sink = jnp.broadcast_to(sinks.astype(jnp.float32).reshape(KV_HEADS, G, 1, 1),
                            logits.shape[:-1] + (1,))
    p = jax.nn.softmax(jnp.concatenate([logits, sink], axis=-1), axis=-1)[..., :-1]
    o = jnp.einsum('bnkgqs,bnskd->bnqkgd', p.astype(vb.dtype), vb)
    return o.reshape(B, L, N_HEADS * HEAD_DIM) @ w_out + b_out


def swiglu(h, w1, w3, w2):
    return (jax.nn.silu(h @ w1) * (h @ w3)) @ w2


def setup_inputs(seed: int = 0) -> dict:
    key = jax.random.key(seed)
    ks = jax.random.split(key, 24)
    f32 = jnp.float32
    D = D_MODEL
    HD = N_HEADS * HEAD_DIM

    def nrm(k, shape, scale):
        return jax.random.normal(k, shape, f32) * scale

    return {
        'x': nrm(ks[0], (BATCH, SEQ, D), 1.0),
        'c': nrm(ks[1], (BATCH, D), 1.0),
        'rel_bias': nrm(ks[2], (N_BUCKETS, N_HEADS), 0.5),
        'w_ada': nrm(ks[3], (DEPTH, D, 6 * D), 0.5 * D ** -0.5),
        'b_ada': nrm(ks[4], (DEPTH, 6 * D), 0.02),
        'norm_mix_g': 1.0 + nrm(ks[5], (DEPTH, D), 0.02),
        'norm_ffn_g': 1.0 + nrm(ks[6], (DEPTH, D), 0.02),
        'a_w_in': nrm(ks[7], (N_A, D, A_IN), D ** -0.5),
        'a_kv_norm_g': 1.0 + nrm(ks[8], (N_A, KV_RANK), 0.02),
        'a_w_uk': nrm(ks[9], (N_A, N_HEADS, KV_RANK, HEAD_DIM), KV_RANK ** -0.5),
        'a_w_uv': nrm(ks[10], (N_A, N_HEADS, KV_RANK, HEAD_DIM), KV_RANK ** -0.5),
        'a_idx_k_g': 1.0 + nrm(ks[11], (N_A, IDX_DIM), 0.02),
        'a_idx_k_b': nrm(ks[12], (N_A, IDX_DIM), 0.02),
        'a_w_out': nrm(ks[13], (N_A, HD, D), HD ** -0.5),
        'b_w_in': nrm(ks[14], (N_B, D, B_IN), D ** -0.5),
        'b_b_in': nrm(ks[15], (N_B, B_IN), 0.02),
        'b_sinks': nrm(ks[16], (N_B, N_HEADS), 0.5),
        'b_w_out': nrm(ks[17], (N_B, HD, D), HD ** -0.5),
        'b_b_out': nrm(ks[18], (N_B, D), 0.02),
        'ffn_w1': nrm(ks[19], (DEPTH, D, D_FF), D ** -0.5),
        'ffn_w3': nrm(ks[20], (DEPTH, D, D_FF), D ** -0.5),
        'ffn_w2': nrm(ks[21], (DEPTH, D_FF, D), D_FF ** -0.5),
        'norm_final_g': 1.0 + nrm(ks[22], (D,), 0.02),
    }


def reference(x, c, rel_bias, w_ada, b_ada, norm_mix_g, norm_ffn_g,
              a_w_in, a_kv_norm_g, a_w_uk, a_w_uv, a_idx_k_g, a_idx_k_b, a_w_out,
              b_w_in, b_b_in, b_sinks, b_w_out, b_b_out,
              ffn_w1, ffn_w3, ffn_w2, norm_final_g):
    cs = jax.nn.silu(c)
    for i in range(DEPTH):
        mod = cs @ w_ada[i] + b_ada[i]
        sh1, sc1, g1, sh2, sc2, g2 = jnp.split(mod, 6, axis=-1)
        h = modulate(rms_norm(x, norm_mix_g[i]), sh1, sc1)
        j = i // N_MIXERS
        if i % N_MIXERS == 0:
            y = dsa_mla_mixer(h, a_w_in[j], a_kv_norm_g[j], a_w_uk[j], a_w_uv[j],
                              a_idx_k_g[j], a_idx_k_b[j], a_w_out[j], rel_bias)
        else:
            y = swa_sink_mixer(h, b_w_in[j], b_b_in[j], b_sinks[j], b_w_out[j],
                               b_b_out[j], rel_bias)
        x = x + g1[:, None, :] * y
        h = modulate(rms_norm(x, norm_ffn_g[i]), sh2, sc2)
        x = x + g2[:, None, :] * swiglu(h, ffn_w1[i], ffn_w3[i], ffn_w2[i])
    return rms_norm(x, norm_final_g)
```

```python
import functools
import math

import numpy as np
import jax
import jax.numpy as jnp
from jax import lax
from jax.experimental import pallas as pl
from jax.experimental.pallas import tpu as pltpu

D_MODEL = 1024
N_HEADS = 16
HEAD_DIM = 64
KV_RANK = 256
IDX_HEADS = 8
IDX_DIM = 64
INDEX_TOPK = 256
KV_HEADS = 2
WINDOW = 128
N_BUCKETS = 32
MAX_DISTANCE = 128
D_FF = 2816
RMS_EPS = 1e-6
NEG = -1e30

HD = N_HEADS * HEAD_DIM
A_Q0, A_KV0, A_QI0, A_KI0, A_WI0 = 0, HD, HD + KV_RANK, HD + KV_RANK + IDX_HEADS * IDX_DIM, \
    HD + KV_RANK + IDX_HEADS * IDX_DIM + IDX_DIM
A_MAIN = A_WI0
A_MAIN_PAD = 1920
B_IN = (N_HEADS + 2 * KV_HEADS) * HEAD_DIM

CHUNK = 256
BIAS_BLK = 128
VMEM_LIMIT = 56 * 1024 * 1024

F32 = jnp.float32
BF16 = jnp.bfloat16
I32 = jnp.int32

_NT = (((1,), (1,)), ((), ()))


def _cparams(*sem):
    return pltpu.CompilerParams(dimension_semantics=sem, vmem_limit_bytes=VMEM_LIMIT)


def _f32_key(v):
    b = int(np.array(v, np.float32).view(np.int32))
    return b ^ ((b >> 31) & 0x7FFFFFFF)


KEY_NEG = _f32_key(NEG)
INT_MIN = -(2 ** 31)


def _adaln_kernel(c_ref, w_ref, b_ref, o_ref):
    c = c_ref[...]
    cs = c * jax.nn.sigmoid(c)
    o_ref[0] = jnp.dot(cs, w_ref[0], preferred_element_type=F32,
                       precision=lax.Precision.HIGHEST) + b_ref[0]


def _adaln(c, w_ada, b_ada):
    depth, d, n = w_ada.shape
    bsz = c.shape[0]
    tn = 1536
    return pl.pallas_call(
        _adaln_kernel,
        grid=(depth, n // tn),
        in_specs=[pl.BlockSpec((bsz, d), lambda i, j: (0, 0)),
                  pl.BlockSpec((1, d, tn), lambda i, j: (i, 0, j)),
                  pl.BlockSpec((1, 1, tn), lambda i, j: (i, 0, j))],
        out_specs=pl.BlockSpec((1, bsz, tn), lambda i, j: (i, 0, j)),
        out_shape=jax.ShapeDtypeStruct((depth, bsz, n), F32),
        compiler_params=_cparams("parallel", "parallel"),
        name="adaln",
    )(c, w_ada, b_ada.reshape(depth, 1, n))


def _normmod(x, g, sh, sc):
    ms = jnp.mean(x * x, axis=-1, keepdims=True)
    y = (x * lax.rsqrt(ms + RMS_EPS)) * g
    return y * (1.0 + sc) + sh


def _a_inproj_kernel(x_ref, g_ref, sh_ref, sc_ref, w_ref, wwi_ref, kvg_ref, ikg_ref, ikb_ref,
                     q_ref, ckv_ref, qi_ref, ki_ref, wit_ref):
    h = _normmod(x_ref[0], g_ref[...], sh_ref[0], sc_ref[0]).astype(BF16)
    proj = jnp.dot(h, w_ref[...], preferred_element_type=F32)
    q_ref[0] = (proj[:, A_Q0:A_KV0] * (HEAD_DIM ** -0.5)).astype(BF16)
    ckv = proj[:, A_KV0:A_QI0]
    ckv = (ckv * lax.rsqrt(jnp.mean(ckv * ckv, axis=-1, keepdims=True) + RMS_EPS)) * kvg_ref[...]
    ckv_ref[0] = ckv.astype(BF16)
    qi_ref[0] = proj[:, A_QI0:A_KI0].astype(BF16)
    ki = proj[:, A_KI0:A_WI0]
    mu = jnp.mean(ki, axis=-1, keepdims=True)
    var = jnp.mean(jnp.square(ki - mu), axis=-1, keepdims=True)
    ki = ((ki - mu) * lax.rsqrt(var + RMS_EPS)) * ikg_ref[...] + ikb_ref[...]
    ki_ref[0] = ki.astype(BF16)
    wit = lax.dot_general(wwi_ref[...], h, _NT, preferred_element_type=F32)
    wit_ref[0] = wit[:IDX_HEADS] * (IDX_HEADS ** -0.5 * IDX_DIM ** -0.5)


def _a_inproj(x, g, sh, sc, w_main, w_wi_t, kv_g, ik_g, ik_b, tm=256):
    bsz, L, d = x.shape
    row = lambda b, i: (b, i, 0)
    per_b = lambda b, i: (b, 0, 0)
    const2 = lambda b, i: (0, 0)
    return pl.pallas_call(
        _a_inproj_kernel,
        grid=(bsz, L // tm),
        in_specs=[pl.BlockSpec((1, tm, d), row),
                  pl.BlockSpec((1, d), const2),
                  pl.BlockSpec((1, 1, d), per_b),
                  pl.BlockSpec((1, 1, d), per_b),
                  pl.BlockSpec((d, A_MAIN_PAD), const2),
                  pl.BlockSpec((16, d), const2),
                  pl.BlockSpec((1, KV_RANK), const2),
                  pl.BlockSpec((1, IDX_DIM), const2),
                  pl.BlockSpec((1, IDX_DIM), const2)],
        out_specs=[pl.BlockSpec((1, tm, HD), row),
                   pl.BlockSpec((1, tm, KV_RANK), row),
                   pl.BlockSpec((1, tm, IDX_HEADS * IDX_DIM), row),
                   pl.BlockSpec((1, tm, IDX_DIM), row),
                   pl.BlockSpec((1, IDX_HEADS, tm), lambda b, i: (b, 0, i))],
        out_shape=[jax.ShapeDtypeStruct((bsz, L, HD), BF16),
                   jax.ShapeDtypeStruct((bsz, L, KV_RANK), BF16),
                   jax.ShapeDtypeStruct((bsz, L, IDX_HEADS * IDX_DIM), BF16),
                   jax.ShapeDtypeStruct((bsz, L, IDX_DIM), BF16),
                   jax.ShapeDtypeStruct((bsz, IDX_HEADS, L), F32)],
        compiler_params=_cparams("parallel", "parallel"),
        name="a_inproj",
    )(x, g, sh, sc, w_main, w_wi_t, kv_g, ik_g, ik_b)


def _a_index_kernel(qi_ref, wit_ref, ki_ref, tri_ref, mask_ref, key_scr, *, topk, seq_len):
    j = pl.program_id(1)
    nchunk = seq_len // CHUNK
    qi = qi_ref[0]
    wit = wit_ref[0]
    t_glob = j * CHUNK + lax.broadcasted_iota(I32, (CHUNK, CHUNK), 1)
    s_loc = lax.broadcasted_iota(I32, (CHUNK, CHUNK), 0)

    def score_chunk(kc, carry):
        kik = ki_ref[0, pl.ds(pl.multiple_of(kc * CHUNK, CHUNK), CHUNK), :]
        acc = jnp.zeros((CHUNK, CHUNK), F32)
        for h in range(IDX_HEADS):
            r = lax.dot_general(kik, qi[:, h * IDX_DIM:(h + 1) * IDX_DIM], _NT,
                                preferred_element_type=F32)
            acc = acc + jnp.maximum(r, 0.0) * wit[h:h + 1, :]
        sc = jnp.where(kc * CHUNK + s_loc <= t_glob, acc, NEG)
        bits = lax.bitcast_convert_type(sc, I32)
        key_scr[kc] = bits ^ ((bits >> 31) & 0x7FFFFFFF)
        return carry

    lax.fori_loop(0, j + 1, score_chunk, 0)

    n_beyond = seq_len - (j + 1) * CHUNK

    def count(pred, thr):
        def body(kc, acc):
            m = jnp.where(pred(key_scr[kc], thr), jnp.int32(1), jnp.int32(0))
            return acc + m.reshape(CHUNK // 8, 8, CHUNK).sum(axis=0)
        acc = lax.fori_loop(0, j + 1, body, jnp.zeros((8, CHUNK), I32))
        cnt = acc.sum(axis=0, keepdims=True)
        return cnt + jnp.where(pred(KEY_NEG, thr), n_beyond, 0)

    ge = lambda a, b: a >= b
    gt = lambda a, b: a > b

    zero = jnp.zeros((1, CHUNK), I32)
    thr = jnp.where(count(ge, zero) >= topk, zero, jnp.full((1, CHUNK), INT_MIN, I32))

    def bisect(i, thr):
        cand = thr + lax.shift_left(jnp.int32(1), 30 - i)
        return jnp.where(count(ge, cand) >= topk, cand, thr)

    thr = lax.fori_loop(0, 31, bisect, thr)

    need = (topk - count(gt, thr)).astype(F32)
    tri = tri_ref[...]

    def emit(kc, seen):
        key = key_scr[kc]
        eq = jnp.where(key == thr, 1.0, 0.0)
        rank = jnp.dot(tri, eq.astype(BF16), preferred_element_type=F32) + seen
        take = jnp.where(key > thr, 1.0, jnp.where(rank < need, eq, 0.0))
        causal = kc * CHUNK + s_loc <= t_glob
        m = jnp.where(causal, jnp.where(take > 0.5, 0.0, NEG), NEG)
        mask_ref[0, kc] = m.T
        return seen + jnp.sum(eq, axis=0, keepdims=True)

    lax.fori_loop(0, j + 1, emit, jnp.zeros((1, CHUNK), F32))

    def fill(kc, carry):
        mask_ref[0, kc] = jnp.full((CHUNK, CHUNK), NEG, F32)
        return carry

    lax.fori_loop(j + 1, nchunk, fill, 0)


def _a_index(qi, wit, ki, topk):
    bsz, L, _ = qi.shape
    nchunk = L // CHUNK
    tri = jnp.tril(jnp.ones((CHUNK, CHUNK), BF16), -1)
    return pl.pallas_call(
        functools.partial(_a_index_kernel, topk=topk, seq_len=L),
        grid=(bsz, nchunk),
        in_specs=[pl.BlockSpec((1, CHUNK, IDX_HEADS * IDX_DIM), lambda b, j: (b, j, 0)),
                  pl.BlockSpec((1, IDX_HEADS, CHUNK), lambda b, j: (b, 0, j)),
                  pl.BlockSpec((1, L, IDX_DIM), lambda b, j: (b, 0, 0)),
                  pl.BlockSpec((CHUNK, CHUNK), lambda b, j: (0, 0))],
        out_specs=pl.BlockSpec((1, nchunk, CHUNK, CHUNK), lambda b, j: (b, 0, j, 0)),
        out_shape=jax.ShapeDtypeStruct((bsz, nchunk, L, CHUNK), F32),
        scratch_shapes=[pltpu.VMEM((nchunk, CHUNK, CHUNK), I32)],
        compiler_params=_cparams("parallel", "parallel"),
        name="a_index",
    )(qi, wit, ki, tri)


def _a_attn_kernel(q_ref, ckv_ref, mask_ref, wuk_ref, wuv_ref, bias_ref, o_ref,
                   qabs_scr, lg_scr, mx_scr, l_scr, acc_scr, *, hc):
    j = pl.program_id(1)
    tq = CHUNK
    q = q_ref[0]
    for i in range(hc):
        qa = lax.dot_general(q[:, i * HEAD_DIM:(i + 1) * HEAD_DIM], wuk_ref[i], _NT,
                             preferred_element_type=F32)
        qabs_scr[i * tq:(i + 1) * tq, :] = qa.astype(BF16)
    qg = qabs_scr[...]

    def logits(kc):
        lg = lax.dot_general(qg, ckv_ref[0, kc], _NT, preferred_element_type=F32)
        mk = mask_ref[0, kc]
        return lg + jnp.concatenate([mk] * hc, axis=0)

    def bias_tile(i, diag):
        d0, d1 = bias_ref[i, 0], bias_ref[i, 1]
        z = jnp.zeros_like(d0)
        if diag:
            return jnp.concatenate([jnp.concatenate([d0, z], axis=1),
                                    jnp.concatenate([d1, d0], axis=1)], axis=0)
        return jnp.concatenate([jnp.concatenate([z, d1], axis=1),
                                jnp.concatenate([z, z], axis=1)], axis=0)

    lg = logits(j) + jnp.concatenate([bias_tile(i, True) for i in range(hc)], axis=0)
    lg_scr[j] = lg
    mx_scr[...] = lg

    @pl.when(j >= 1)
    def _():
        lg = logits(j - 1) + jnp.concatenate([bias_tile(i, False) for i in range(hc)], axis=0)
        lg_scr[j - 1] = lg
        mx_scr[...] = jnp.maximum(mx_scr[...], lg)

    def far(kc, carry):
        lg = logits(kc)
        lg_scr[kc] = lg
        mx_scr[...] = jnp.maximum(mx_scr[...], lg)
        return carry

    lax.fori_loop(0, j - 1, far, 0)

    mrow = jnp.max(mx_scr[...], axis=-1, keepdims=True)
    l_scr[...] = jnp.zeros_like(l_scr)
    acc_scr[...] = jnp.zeros_like(acc_scr)

    def pv(kc, carry):
        p = jnp.exp(lg_scr[kc] - mrow)
        l_scr[...] += p
        acc_scr[...] += jnp.dot(p.astype(BF16), ckv_ref[0, kc], preferred_element_type=F32)
        return carry

    lax.fori_loop(0, j + 1, pv, 0)

    o = acc_scr[...] / jnp.sum(l_scr[...], axis=-1, keepdims=True)
    outs = [jnp.dot(o[i * tq:(i + 1) * tq].astype(BF16), wuv_ref[i], preferred_element_type=F32)
            for i in range(hc)]
    o_ref[0] = jnp.concatenate(outs, axis=1).astype(BF16)


def _a_attn(q, ckv, mask, w_uk, w_uv, bias_nd, hc=4):
    bsz, L, _ = q.shape
    nchunk = L // CHUNK
    hg = N_HEADS // hc
    m = hc * CHUNK
    ckv4 = ckv.reshape(bsz, nchunk, CHUNK, KV_RANK)
    return pl.pallas_call(
        functools.partial(_a_attn_kernel, hc=hc),
        grid=(bsz, nchunk, hg),
        in_specs=[pl.BlockSpec((1, CHUNK, hc * HEAD_DIM), lambda b, j, g: (b, j, g)),
                  pl.BlockSpec((1, nchunk, CHUNK, KV_RANK), lambda b, j, g: (b, 0, 0, 0)),
                  pl.BlockSpec((1, nchunk, CHUNK, CHUNK), lambda b, j, g: (b, 0, j, 0)),
                  pl.BlockSpec((hc, KV_RANK, HEAD_DIM), lambda b, j, g: (g, 0, 0)),
                  pl.BlockSpec((hc, KV_RANK, HEAD_DIM), lambda b, j, g: (g, 0, 0)),
                  pl.BlockSpec((hc, 2, BIAS_BLK, BIAS_BLK), lambda b, j, g: (g, 0, 0, 0))],
        out_specs=pl.BlockSpec((1, CHUNK, hc * HEAD_DIM), lambda b, j, g: (b, j, g)),
        out_shape=jax.ShapeDtypeStruct((bsz, L, HD), BF16),
        scratch_shapes=[pltpu.VMEM((m, KV_RANK), BF16),
                        pltpu.VMEM((nchunk, m, CHUNK), F32),
                        pltpu.VMEM((m, CHUNK), F32),
                        pltpu.VMEM((m, CHUNK), F32),
                        pltpu.VMEM((m, KV_RANK), F32)],
        compiler_params=_cparams("parallel", "parallel", "arbitrary"),
        name="a_attn",
    )(q, ckv4, mask, w_uk, w_uv, bias_nd)


def _bias_kernel(rb_ref, ids_ref, o_ref, *, n_tiles, far_bucket, subtract_far):
    h = pl.program_id(0)
    far = rb_ref[far_bucket, h] if subtract_far else 0.0
    for t in range(n_tiles):
        ids = ids_ref[t]
        out = jnp.zeros(ids.shape, F32)
        for b in range(N_BUCKETS):
            out = jnp.where(ids == b, rb_ref[b, h] - far, out)
        o_ref[0, t] = out


def _t5_bucket(dist):
    max_exact = N_BUCKETS // 2
    d = jnp.maximum(dist, 0)
    large = max_exact + (jnp.log(jnp.maximum(d, 1).astype(F32) / max_exact)
                         / math.log(MAX_DISTANCE / max_exact)
                         * (N_BUCKETS - max_exact)).astype(I32)
    large = jnp.minimum(large, N_BUCKETS - 1)
    return jnp.where(d < max_exact, d, large)


def _bias_tiles(rel_bias, ids, subtract_far):
    n_tiles, r, c = ids.shape
    return pl.pallas_call(
        functools.partial(_bias_kernel, n_tiles=n_tiles, far_bucket=N_BUCKETS - 1,
                          subtract_far=subtract_far),
        grid=(N_HEADS,),
        in_specs=[pl.BlockSpec(memory_space=pltpu.SMEM),
                  pl.BlockSpec((n_tiles, r, c), lambda h: (0, 0, 0))],
        out_specs=pl.BlockSpec((1, n_tiles, r, c), lambda h: (h, 0, 0, 0)),
        out_shape=jax.ShapeDtypeStruct((N_HEADS, n_tiles, r, c), F32),
        compiler_params=_cparams("parallel"),
        name="bias_tiles",
    )(rel_bias, ids)


def _b_inproj_kernel(x_ref, g_ref, sh_ref, sc_ref, w_ref, b_ref, q_ref, k_ref, v_ref):
    h = _normmod(x_ref[0], g_ref[...], sh_ref[0], sc_ref[0]).astype(BF16)
    proj = jnp.dot(h, w_ref[...], preferred_element_type=F32) + b_ref[...]
    q_ref[0] = (proj[:, :HD] * (HEAD_DIM ** -0.5)).astype(BF16)
    k_ref[0] = proj[:, HD:HD + KV_HEADS * HEAD_DIM].astype(BF16)
    v_ref[0] = proj[:, HD + KV_HEADS * HEAD_DIM:].astype(BF16)


def _b_inproj(x, g, sh, sc, w, b, tm=256):
    bsz, L, d = x.shape
    kvw = KV_HEADS * HEAD_DIM
    row = lambda b_, i: (b_, i, 0)
    per_b = lambda b_, i: (b_, 0, 0)
    const2 = lambda b_, i: (0, 0)
    return pl.pallas_call(
        _b_inproj_kernel,
        grid=(bsz, L // tm),
        in_specs=[pl.BlockSpec((1, tm, d), row),
                  pl.BlockSpec((1, d), const2),
                  pl.BlockSpec((1, 1, d), per_b),
                  pl.BlockSpec((1, 1, d), per_b),
                  pl.BlockSpec((d, B_IN), const2),
                  pl.BlockSpec((1, B_IN), const2)],
        out_specs=[pl.BlockSpec((1, tm, HD), row),
                   pl.BlockSpec((1, tm, kvw), row),
                   pl.BlockSpec((1, tm, kvw), row)],
        out_shape=[jax.ShapeDtypeStruct((bsz, L, HD), BF16),
                   jax.ShapeDtypeStruct((bsz, L, kvw), BF16),
                   jax.ShapeDtypeStruct((bsz, L, kvw), BF16)],
        compiler_params=_cparams("parallel", "parallel"),
        name="b_inproj",
    )(x, g, sh, sc, w, b)


def _b_attn_kernel(sink_ref, q_ref, kp_ref, kc_ref, vp_ref, vc_ref, bias_ref, o_ref):
    n = pl.program_id(1)
    w = WINDOW
    g = N_HEADS // KV_HEADS
    q = q_ref[0]
    kb = jnp.concatenate([kp_ref[0], kc_ref[0]], axis=0)
    vb = jnp.concatenate([vp_ref[0], vc_ref[0]], axis=0)
    qpos = lax.broadcasted_iota(I32, (w, 2 * w), 0)
    kpos = lax.broadcasted_iota(I32, (w, 2 * w), 1)
    dist = qpos + w - kpos
    first_key = jnp.where(n > 0, 0, w)
    allowed = (dist >= 0) & (dist < WINDOW) & (kpos >= first_key)
    outs = []
    for h in range(N_HEADS):
        kh = h // g
        lg = lax.dot_general(q[:, h * HEAD_DIM:(h + 1) * HEAD_DIM],
                             kb[:, kh * HEAD_DIM:(kh + 1) * HEAD_DIM], _NT,
                             preferred_element_type=F32)
        lg = jnp.where(allowed, lg + bias_ref[h, 0], NEG)
        sink = sink_ref[0, h]
        m = jnp.maximum(jnp.max(lg, axis=-1, keepdims=True), sink)
        e = jnp.exp(lg - m)
        denom = jnp.sum(e, axis=-1, keepdims=True) + jnp.exp(sink - m)
        o = jnp.dot(e.astype(BF16), vb[:, kh * HEAD_DIM:(kh + 1) * HEAD_DIM],
                    preferred_element_type=F32)
        outs.append(o / denom)
    o_ref[0] = jnp.concatenate(outs, axis=1).astype(BF16)


def _b_attn(q, k, v, sinks, bias_b):
    bsz, L, _ = q.shape
    w = WINDOW
    kvw = KV_HEADS * HEAD_DIM
    cur = lambda b, n: (b, n, 0)
    prev = lambda b, n: (b, jnp.maximum(n - 1, 0), 0)
    return pl.pallas_call(
        _b_attn_kernel,
        grid=(bsz, L // w),
        in_specs=[pl.BlockSpec(memory_space=pltpu.SMEM),
                  pl.BlockSpec((1, w, HD), cur),
                  pl.BlockSpec((1, w, kvw), prev),
                  pl.BlockSpec((1, w, kvw), cur),
                  pl.BlockSpec((1, w, kvw), prev),
                  pl.BlockSpec((1, w, kvw), cur),
                  pl.BlockSpec((N_HEADS, 1, w, 2 * w), lambda b, n: (0, 0, 0, 0))],
        out_specs=pl.BlockSpec((1, w, HD), cur),
        out_shape=jax.ShapeDtypeStruct((bsz, L, HD), BF16),
        compiler_params=_cparams("parallel", "parallel"),
        name="b_attn",
    )(sinks.reshape(1, N_HEADS), q, k, k, v, v, bias_b)


def _out_proj_kernel(o_ref, w_ref, b_ref, gate_ref, x_ref, y_ref):
    y = jnp.dot(o_ref[0], w_ref[...], preferred_element_type=F32) + b_ref[...]
    y_ref[0] = x_ref[0] + gate_ref[0] * y


def _out_proj(o, w, b, gate, x, tm=512):
    bsz, L, d = x.shape
    row = lambda b_, i: (b_, i, 0)
    return pl.pallas_call(
        _out_proj_kernel,
        grid=(bsz, L // tm),
        in_specs=[pl.BlockSpec((1, tm, HD), row),
                  pl.BlockSpec((HD, d), lambda b_, i: (0, 0)),
                  pl.BlockSpec((1, d), lambda b_, i: (0, 0)),
                  pl.BlockSpec((1, 1, d), lambda b_, i: (b_, 0, 0)),
                  pl.BlockSpec((1, tm, d), row)],
        out_specs=pl.BlockSpec((1, tm, d), row),
        out_shape=jax.ShapeDtypeStruct((bsz, L, d), F32),
        compiler_params=_cparams("parallel", "parallel"),
        name="out_proj",
    )(o, w, b, gate, x)


FF_CHUNK = 256


def _ffn_kernel(x_ref, g_ref, sh_ref, sc_ref, gate_ref, w1_ref, w3_ref, w2_ref, fg_ref, y_ref,
                *, final_norm):
    x = x_ref[0]
    h = _normmod(x, g_ref[...], sh_ref[0], sc_ref[0]).astype(BF16)
    acc = jnp.zeros(x.shape, F32)
    for c in range(D_FF // FF_CHUNK):
        cs = slice(c * FF_CHUNK, (c + 1) * FF_CHUNK)
        a1 = jnp.dot(h, w1_ref[:, cs], preferred_element_type=F32)
        a3 = jnp.dot(h, w3_ref[:, cs], preferred_element_type=F32)
        act = (a1 * jax.nn.sigmoid(a1)) * a3
        acc = acc + jnp.dot(act.astype(BF16), w2_ref[cs, :], preferred_element_type=F32)
    y = x + gate_ref[0] * acc
    if final_norm:
        y = (y * lax.rsqrt(jnp.mean(y * y, axis=-1, keepdims=True) + RMS_EPS)) * fg_ref[...]
    y_ref[0] = y


def _ffn(x, g, sh, sc, gate, w1, w3, w2, final_g, final_norm, tm=256):
    bsz, L, d = x.shape
    row = lambda b_, i: (b_, i, 0)
    per_b = lambda b_, i: (b_, 0, 0)
    const2 = lambda b_, i: (0, 0)
    return pl.pallas_call(
        functools.partial(_ffn_kernel, final_norm=final_norm),
        grid=(bsz, L // tm),
        in_specs=[pl.BlockSpec((1, tm, d), row),
                  pl.BlockSpec((1, d), const2),
                  pl.BlockSpec((1, 1, d), per_b),
                  pl.BlockSpec((1, 1, d), per_b),
                  pl.BlockSpec((1, 1, d), per_b),
                  pl.BlockSpec((d, D_FF), const2),
                  pl.BlockSpec((d, D_FF), const2),
                  pl.BlockSpec((D_FF, d), const2),
                  pl.BlockSpec((1, d), const2)],
        out_specs=pl.BlockSpec((1, tm, d), row),
        out_shape=jax.ShapeDtypeStruct((bsz, L, d), F32),
        compiler_params=_cparams("parallel", "parallel"),
        name="ffn",
    )(x, g, sh, sc, gate, w1, w3, w2, final_g)


def kernel(x, c, rel_bias, w_ada, b_ada, norm_mix_g, norm_ffn_g, a_w_in, a_kv_norm_g, a_w_uk,
           a_w_uv, a_idx_k_g, a_idx_k_b, a_w_out, b_w_in, b_b_in, b_sinks, b_w_out, b_b_out,
           ffn_w1, ffn_w3, ffn_w2, norm_final_g):
    bsz, L, d = x.shape
    depth = w_ada.shape[0]
    topk = min(INDEX_TOPK, L // 4)

    mod = _adaln(c, w_ada, b_ada)

    r = jnp.arange(BIAS_BLK)
    ids_a = jnp.stack([_t5_bucket(r[:, None] - r[None, :]),
                       _t5_bucket(r[:, None] - r[None, :] + BIAS_BLK)])
    bias_a = _bias_tiles(rel_bias, ids_a, subtract_far=True)
    ids_b = _t5_bucket(jnp.arange(WINDOW)[:, None] + WINDOW - jnp.arange(2 * WINDOW)[None, :])
    bias_b = _bias_tiles(rel_bias, ids_b[None], subtract_far=False)

    zero_bias = jnp.zeros((1, d), F32)
    for i in range(depth):
        sh1, sc1, g1, sh2, sc2, g2 = [m.reshape(bsz, 1, d) for m in jnp.split(mod[i], 6, axis=-1)]
        jm = i // 2
        if i % 2 == 0:
            w_in = a_w_in[jm]
            w_main = jnp.pad(w_in[:, :A_MAIN], ((0, 0), (0, A_MAIN_PAD - A_MAIN))).astype(BF16)
            w_wi_t = jnp.pad(w_in[:, A_WI0:A_WI0 + IDX_HEADS].T,
                             ((0, 16 - IDX_HEADS), (0, 0))).astype(BF16)
            q, ckv, qi, ki, wit = _a_inproj(
                x, norm_mix_g[i][None], sh1, sc1, w_main, w_wi_t, a_kv_norm_g[jm][None],
                a_idx_k_g[jm][None], a_idx_k_b[jm][None])
            mask = _a_index(qi, wit, ki, topk)
            o = _a_attn(q, ckv, mask, a_w_uk[jm].astype(BF16), a_w_uv[jm].astype(BF16), bias_a)
            x = _out_proj(o, a_w_out[jm].astype(BF16), zero_bias, g1, x)
        else:
            q, k, v = _b_inproj(x, norm_mix_g[i][None], sh1, sc1, b_w_in[jm].astype(BF16),
                                b_b_in[jm][None])
            o = _b_attn(q, k, v, b_sinks[jm], bias_b)
            x = _out_proj(o, b_w_out[jm].astype(BF16), b_b_out[jm][None], g1, x)
        x = _ffn(x, norm_ffn_g[i][None], sh2, sc2, g2, ffn_w1[i].astype(BF16),
                 ffn_w3[i].astype(BF16), ffn_w2[i].astype(BF16), norm_final_g[None],
                 final_norm=(i == depth - 1))
    return x
```

```python
import functools
import math

import numpy as np
import jax
import jax.numpy as jnp
from jax import lax
from jax.experimental import pallas as pl
from jax.experimental.pallas import tpu as pltpu

D_MODEL = 1024
N_HEADS = 16
HEAD_DIM = 64
KV_RANK = 256
IDX_HEADS = 8
IDX_DIM = 64
INDEX_TOPK = 256
KV_HEADS = 2
WINDOW = 128
N_BUCKETS = 32
MAX_DISTANCE = 128
D_FF = 2816
RMS_EPS = 1e-6
NEG = -1e30

HD = N_HEADS * HEAD_DIM
A_Q0, A_KV0, A_QI0, A_KI0, A_WI0 = 0, HD, HD + KV_RANK, HD + KV_RANK + IDX_HEADS * IDX_DIM, \
    HD + KV_RANK + IDX_HEADS * IDX_DIM + IDX_DIM
A_MAIN = A_WI0
A_MAIN_PAD = 1920
B_IN = (N_HEADS + 2 * KV_HEADS) * HEAD_DIM

CHUNK = 256
PV_ROWS = 64
LANES = 128
BIAS_BLK = 128
VMEM_LIMIT = 56 * 1024 * 1024

F32 = jnp.float32
BF16 = jnp.bfloat16
I32 = jnp.int32

_NT = (((1,), (1,)), ((), ()))


def _cparams(*sem):
    return pltpu.CompilerParams(dimension_semantics=sem, vmem_limit_bytes=VMEM_LIMIT)


def _f32_key(v):
    b = int(np.array(v, np.float32).view(np.int32))
    return b ^ ((b >> 31) & 0x7FFFFFFF)


LOG2E = math.log2(math.e)
KEY_NEG = _f32_key(NEG)
INT_MIN = -(2 ** 31)


def _adaln_kernel(c_ref, w_ref, b_ref, o_ref):
    c = c_ref[...]
    cs = c * jax.nn.sigmoid(c)
    o_ref[0] = jnp.dot(cs, w_ref[0], preferred_element_type=F32,
                       precision=lax.Precision.HIGHEST) + b_ref[0]


def _adaln(c, w_ada, b_ada):
    depth, d, n = w_ada.shape
    bsz = c.shape[0]
    tn = 1536
    return pl.pallas_call(
        _adaln_kernel,
        grid=(depth, n // tn),
        in_specs=[pl.BlockSpec((bsz, d), lambda i, j: (0, 0)),
                  pl.BlockSpec((1, d, tn), lambda i, j: (i, 0, j)),
                  pl.BlockSpec((1, 1, tn), lambda i, j: (i, 0, j))],
        out_specs=pl.BlockSpec((1, bsz, tn), lambda i, j: (i, 0, j)),
        out_shape=jax.ShapeDtypeStruct((depth, bsz, n), F32),
        compiler_params=_cparams("parallel", "parallel"),
        name="adaln",
    )(c, w_ada, b_ada.reshape(depth, 1, n))


def _normmod(x, g, sh, sc):
    ms = jnp.mean(x * x, axis=-1, keepdims=True)
    y = (x * lax.rsqrt(ms + RMS_EPS)) * g
    return y * (1.0 + sc) + sh


def _a_inproj_kernel(x_ref, g_ref, sh_ref, sc_ref, w_ref, wwi_ref, kvg_ref, ikg_ref, ikb_ref,
                     q_ref, ckv_ref, qi_ref, ki_ref, wit_ref):
    h = _normmod(x_ref[0], g_ref[...], sh_ref[0], sc_ref[0]).astype(BF16)
    proj = jnp.dot(h, w_ref[...], preferred_element_type=F32)
    q_ref[0] = (proj[:, A_Q0:A_KV0] * (HEAD_DIM ** -0.5)).astype(BF16)
    ckv = proj[:, A_KV0:A_QI0]
    ckv = (ckv * lax.rsqrt(jnp.mean(ckv * ckv, axis=-1, keepdims=True) + RMS_EPS)) * kvg_ref[...]
    ckv_ref[0] = ckv.astype(BF16)
    qi_ref[0] = proj[:, A_QI0:A_KI0].astype(BF16)
    ki = proj[:, A_KI0:A_WI0]
    mu = jnp.mean(ki, axis=-1, keepdims=True)
    var = jnp.mean(jnp.square(ki - mu), axis=-1, keepdims=True)
    ki = ((ki - mu) * lax.rsqrt(var + RMS_EPS)) * ikg_ref[...] + ikb_ref[...]
    ki_ref[0] = ki.astype(BF16)
    wit = lax.dot_general(wwi_ref[...], h, _NT, preferred_element_type=F32)
    wit_ref[0] = wit[:IDX_HEADS] * (IDX_HEADS ** -0.5 * IDX_DIM ** -0.5)


def _a_inproj(x, g, sh, sc, w_main, w_wi_t, kv_g, ik_g, ik_b, tm=256):
    bsz, L, d = x.shape
    row = lambda b, i: (b, i, 0)
    per_b = lambda b, i: (b, 0, 0)
    const2 = lambda b, i: (0, 0)
    return pl.pallas_call(
        _a_inproj_kernel,
        grid=(bsz, L // tm),
        in_specs=[pl.BlockSpec((1, tm, d), row),
                  pl.BlockSpec((1, d), const2),
                  pl.BlockSpec((1, 1, d), per_b),
                  pl.BlockSpec((1, 1, d), per_b),
                  pl.BlockSpec((d, A_MAIN_PAD), const2),
                  pl.BlockSpec((16, d), const2),
                  pl.BlockSpec((1, KV_RANK), const2),
                  pl.BlockSpec((1, IDX_DIM), const2),
                  pl.BlockSpec((1, IDX_DIM), const2)],
        out_specs=[pl.BlockSpec((1, tm, HD), row),
                   pl.BlockSpec((1, tm, KV_RANK), row),
                   pl.BlockSpec((1, tm, IDX_HEADS * IDX_DIM), row),
                   pl.BlockSpec((1, tm, IDX_DIM), row),
                   pl.BlockSpec((1, IDX_HEADS, tm), lambda b, i: (b, 0, i))],
        out_shape=[jax.ShapeDtypeStruct((bsz, L, HD), BF16),
                   jax.ShapeDtypeStruct((bsz, L, KV_RANK), BF16),
                   jax.ShapeDtypeStruct((bsz, L, IDX_HEADS * IDX_DIM), BF16),
                   jax.ShapeDtypeStruct((bsz, L, IDX_DIM), BF16),
                   jax.ShapeDtypeStruct((bsz, IDX_HEADS, L), F32)],
        compiler_params=_cparams("parallel", "parallel"),
        name="a_inproj",
    )(x, g, sh, sc, w_main, w_wi_t, kv_g, ik_g, ik_b)


def _a_index_kernel(qi_ref, wit_ref, ki_ref, tri_ref, mask_ref, key_scr, *, topk, seq_len):
    j = pl.program_id(1)
    nchunk = seq_len // CHUNK
    qi = qi_ref[0]
    wit = wit_ref[0]
    t_glob = j * CHUNK + lax.broadcasted_iota(I32, (CHUNK, CHUNK), 1)
    s_loc = lax.broadcasted_iota(I32, (CHUNK, CHUNK), 0)

    def score_chunk(kc, carry):
        kik = ki_ref[0, pl.ds(pl.multiple_of(kc * CHUNK, CHUNK), CHUNK), :]
        acc = jnp.zeros((CHUNK, CHUNK), F32)
        for h in range(IDX_HEADS):
            r = lax.dot_general(kik, qi[:, h * IDX_DIM:(h + 1) * IDX_DIM], _NT,
                                preferred_element_type=F32)
            acc = acc + jnp.maximum(r, 0.0) * wit[h:h + 1, :]
        sc = jnp.where(kc * CHUNK + s_loc <= t_glob, acc, NEG)
        bits = lax.bitcast_convert_type(sc, I32)
        key_scr[kc] = bits ^ ((bits >> 31) & 0x7FFFFFFF)
        return carry

    lax.fori_loop(0, j + 1, score_chunk, 0)

    n_beyond = seq_len - (j + 1) * CHUNK

    def count(pred, thr):
        def body(kc, acc):
            m = jnp.where(pred(key_scr[kc], thr), jnp.int32(1), jnp.int32(0))
            return acc + m.reshape(CHUNK // 8, 8, CHUNK).sum(axis=0)
        acc = lax.fori_loop(0, j + 1, body, jnp.zeros((8, CHUNK), I32))
        cnt = acc.sum(axis=0, keepdims=True)
        return cnt + jnp.where(pred(KEY_NEG, thr), n_beyond, 0)

    ge = lambda a, b: a >= b
    gt = lambda a, b: a > b

    zero = jnp.zeros((1, CHUNK), I32)
    thr = jnp.where(count(ge, zero) >= topk, zero, jnp.full((1, CHUNK), INT_MIN, I32))

    def bisect(i, thr):
        cand = thr + lax.shift_left(jnp.int32(1), 30 - i)
        return jnp.where(count(ge, cand) >= topk, cand, thr)

    thr = lax.fori_loop(0, 31, bisect, thr)

    need = (topk - count(gt, thr)).astype(F32)
    tri = tri_ref[...]

    def emit(kc, seen):
        key = key_scr[kc]
        eq = jnp.where(key == thr, 1.0, 0.0)
        rank = jnp.dot(tri, eq.astype(BF16), preferred_element_type=F32) + seen
        take = jnp.where(key > thr, 1.0, jnp.where(rank < need, eq, 0.0))
        causal = kc * CHUNK + s_loc <= t_glob
        m = jnp.where(causal, jnp.where(take > 0.5, 0.0, NEG), NEG)
        mask_ref[0, kc] = m.T
        return seen + jnp.sum(eq, axis=0, keepdims=True)

    lax.fori_loop(0, j + 1, emit, jnp.zeros((1, CHUNK), F32))

    def fill(kc, carry):
        mask_ref[0, kc] = jnp.full((CHUNK, CHUNK), NEG, F32)
        return carry

    lax.fori_loop(j + 1, nchunk, fill, 0)


def _a_index(qi, wit, ki, topk):
    bsz, L, _ = qi.shape
    nchunk = L // CHUNK
    tri = jnp.tril(jnp.ones((CHUNK, CHUNK), BF16), -1)
    return pl.pallas_call(
        functools.partial(_a_index_kernel, topk=topk, seq_len=L),
        grid=(bsz, nchunk),
        in_specs=[pl.BlockSpec((1, CHUNK, IDX_HEADS * IDX_DIM), lambda b, j: (b, j, 0)),
                  pl.BlockSpec((1, IDX_HEADS, CHUNK), lambda b, j: (b, 0, j)),
                  pl.BlockSpec((1, L, IDX_DIM), lambda b, j: (b, 0, 0)),
                  pl.BlockSpec((CHUNK, CHUNK), lambda b, j: (0, 0))],
        out_specs=pl.BlockSpec((1, nchunk, CHUNK, CHUNK), lambda b, j: (b, 0, j, 0)),
        out_shape=jax.ShapeDtypeStruct((bsz, nchunk, L, CHUNK), F32),
        scratch_shapes=[pltpu.VMEM((nchunk, CHUNK, CHUNK), I32)],
        compiler_params=_cparams("parallel", "parallel"),
        name="a_index",
    )(qi, wit, ki, tri)


def _a_attn_kernel(q_ref, ckv_ref, mask_ref, wuk_ref, wuv_ref, bias_ref, o_ref,
                   qabs_scr, lg_scr, m_scr, l_scr, acc_scr, p_scr, *, hc):
    j = pl.program_id(1)
    tq = CHUNK
    q = q_ref[0]
    for i in range(hc):
        qa = lax.dot_general(q[:, i * HEAD_DIM:(i + 1) * HEAD_DIM], wuk_ref[i], _NT,
                             preferred_element_type=F32)
        qabs_scr[i * tq:(i + 1) * tq, :] = (qa * LOG2E).astype(BF16)
    qg = qabs_scr[...]

    n_far_pairs = lax.shift_right_arithmetic(j - 1, 1)
    odd = (j & 1) == 0

    def keys(kc, width):
        return ckv_ref[0, pl.ds(kc, width)].reshape(width * CHUNK, KV_RANK)

    def logits(kc, width):
        lg = lax.dot_general(qg, keys(kc, width), _NT, preferred_element_type=F32)
        mk = jnp.concatenate([mask_ref[0, kc + w] for w in range(width)], axis=1)
        return lg + jnp.concatenate([mk] * hc, axis=0)

    def near_bias(i, with_prev):
        d0, d1 = bias_ref[i, 0], bias_ref[i, 1]
        z = jnp.zeros_like(d0)
        top, bot = [d0, z], [d1, d0]
        if with_prev:
            top, bot = [z, d1] + top, [z, z] + bot
        return jnp.concatenate([jnp.concatenate(top, axis=1), jnp.concatenate(bot, axis=1)], axis=0)

    def put_logits(kc, width, lg):
        for w in range(width):
            lg_scr[kc + w] = lg[:, w * CHUNK:(w + 1) * CHUNK]
        m = m_scr[...]
        for c in range(width * CHUNK // LANES):
            m = jnp.maximum(m, lg[:, c * LANES:(c + 1) * LANES])
        m_scr[...] = m

    m_scr[...] = jnp.full(m_scr.shape, -jnp.inf, F32)

    @pl.when(j >= 1)
    def _():
        bias = jnp.concatenate([near_bias(i, True) for i in range(hc)], axis=0)
        put_logits(j - 1, 2, logits(j - 1, 2) + bias)

    def far_pair(i, carry):
        kc = j - 3 - 2 * i
        put_logits(kc, 2, logits(kc, 2))
        return carry

    lax.fori_loop(0, n_far_pairs, far_pair, 0)

    @pl.when(odd)
    def _():
        bias = jnp.concatenate([near_bias(i, False) for i in range(hc)], axis=0)
        put_logits(0, 1, logits(0, 1) + jnp.where(j == 0, bias, 0.0))

    m_scr[...] = jnp.broadcast_to(jnp.max(m_scr[...], axis=-1, keepdims=True), m_scr.shape)
    l_scr[...] = jnp.zeros_like(l_scr)
    acc_scr[...] = jnp.zeros_like(acc_scr)

    def pv(kc, width):
        for r in range(hc * tq // PV_ROWS):
            rows = slice(r * PV_ROWS, (r + 1) * PV_ROWS)
            m = m_scr[rows, :]
            l = l_scr[rows, :]
            for w in range(width):
                for c in range(CHUNK // LANES):
                    cols = slice(c * LANES, (c + 1) * LANES)
                    p = jnp.exp2(lg_scr[kc + w, rows, cols] - m)
                    l = l + p
                    p_scr[rows, w * CHUNK + c * LANES:w * CHUNK + (c + 1) * LANES] = p.astype(BF16)
            l_scr[rows, :] = l
        acc_scr[...] += jnp.dot(p_scr[:, :width * CHUNK], keys(kc, width),
                                preferred_element_type=F32)

    @pl.when(j >= 1)
    def _():
        pv(j - 1, 2)

    def far_pv(i, carry):
        pv(j - 3 - 2 * i, 2)
        return carry

    lax.fori_loop(0, n_far_pairs, far_pv, 0)

    @pl.when(odd)
    def _():
        pv(0, 1)

    o = acc_scr[...] / jnp.sum(l_scr[...], axis=-1, keepdims=True)
    outs = [jnp.dot(o[i * tq:(i + 1) * tq].astype(BF16), wuv_ref[i], preferred_element_type=F32)
            for i in range(hc)]
    o_ref[0] = jnp.concatenate(outs, axis=1).astype(BF16)


def _a_attn(q, ckv, mask, w_uk, w_uv, bias_nd, hc=8):
    bsz, L, _ = q.shape
    nchunk = L // CHUNK
    hg = N_HEADS // hc
    m = hc * CHUNK
    ckv4 = ckv.reshape(bsz, nchunk, CHUNK, KV_RANK)
    return pl.pallas_call(
        functools.partial(_a_attn_kernel, hc=hc),
        grid=(bsz, nchunk, hg),
        in_specs=[pl.BlockSpec((1, CHUNK, hc * HEAD_DIM), lambda b, j, g: (b, j, g)),
                  pl.BlockSpec((1, nchunk, CHUNK, KV_RANK), lambda b, j, g: (b, 0, 0, 0)),
                  pl.BlockSpec((1, nchunk, CHUNK, CHUNK), lambda b, j, g: (b, 0, j, 0)),
                  pl.BlockSpec((hc, KV_RANK, HEAD_DIM), lambda b, j, g: (g, 0, 0)),
                  pl.BlockSpec((hc, KV_RANK, HEAD_DIM), lambda b, j, g: (g, 0, 0)),
                  pl.BlockSpec((hc, 2, BIAS_BLK, BIAS_BLK), lambda b, j, g: (g, 0, 0, 0))],
        out_specs=pl.BlockSpec((1, CHUNK, hc * HEAD_DIM), lambda b, j, g: (b, j, g)),
        out_shape=jax.ShapeDtypeStruct((bsz, L, HD), BF16),
        scratch_shapes=[pltpu.VMEM((m, KV_RANK), BF16),
                        pltpu.VMEM((nchunk, m, CHUNK), F32),
                        pltpu.VMEM((m, LANES), F32),
                        pltpu.VMEM((m, LANES), F32),
                        pltpu.VMEM((m, KV_RANK), F32),
                        pltpu.VMEM((m, 2 * CHUNK), BF16)],
        compiler_params=_cparams("parallel", "parallel", "arbitrary"),
        name="a_attn",
    )(q, ckv4, mask, w_uk, w_uv, bias_nd)


def _bias_kernel(rb_ref, ids_ref, o_ref, *, n_tiles, far_bucket, log2_shifted):
    h = pl.program_id(0)
    far = rb_ref[far_bucket, h] if log2_shifted else 0.0
    scale = LOG2E if log2_shifted else 1.0
    for t in range(n_tiles):
        ids = ids_ref[t]
        out = jnp.zeros(ids.shape, F32)
        for b in range(N_BUCKETS):
            out = jnp.where(ids == b, rb_ref[b, h] - far, out)
        o_ref[0, t] = out * scale


def _t5_bucket(dist):
    max_exact = N_BUCKETS // 2
    d = jnp.maximum(dist, 0)
    large = max_exact + (jnp.log(jnp.maximum(d, 1).astype(F32) / max_exact)
                         / math.log(MAX_DISTANCE / max_exact)
                         * (N_BUCKETS - max_exact)).astype(I32)
    large = jnp.minimum(large, N_BUCKETS - 1)
    return jnp.where(d < max_exact, d, large)


def _bias_tiles(rel_bias, ids, log2_shifted):
    n_tiles, r, c = ids.shape
    return pl.pallas_call(
        functools.partial(_bias_kernel, n_tiles=n_tiles, far_bucket=N_BUCKETS - 1,
                          log2_shifted=log2_shifted),
        grid=(N_HEADS,),
        in_specs=[pl.BlockSpec(memory_space=pltpu.SMEM),
                  pl.BlockSpec((n_tiles, r, c), lambda h: (0, 0, 0))],
        out_specs=pl.BlockSpec((1, n_tiles, r, c), lambda h: (h, 0, 0, 0)),
        out_shape=jax.ShapeDtypeStruct((N_HEADS, n_tiles, r, c), F32),
        compiler_params=_cparams("parallel"),
        name="bias_tiles",
    )(rel_bias, ids)


def _b_inproj_kernel(x_ref, g_ref, sh_ref, sc_ref, w_ref, b_ref, q_ref, k_ref, v_ref):
    h = _normmod(x_ref[0], g_ref[...], sh_ref[0], sc_ref[0]).astype(BF16)
    proj = jnp.dot(h, w_ref[...], preferred_element_type=F32) + b_ref[...]
    q_ref[0] = (proj[:, :HD] * (HEAD_DIM ** -0.5)).astype(BF16)
    k_ref[0] = proj[:, HD:HD + KV_HEADS * HEAD_DIM].astype(BF16)
    v_ref[0] = proj[:, HD + KV_HEADS * HEAD_DIM:].astype(BF16)


def _b_inproj(x, g, sh, sc, w, b, tm=256):
    bsz, L, d = x.shape
    kvw = KV_HEADS * HEAD_DIM
    row = lambda b_, i: (b_, i, 0)
    per_b = lambda b_, i: (b_, 0, 0)
    const2 = lambda b_, i: (0, 0)
    return pl.pallas_call(
        _b_inproj_kernel,
        grid=(bsz, L // tm),
        in_specs=[pl.BlockSpec((1, tm, d), row),
                  pl.BlockSpec((1, d), const2),
                  pl.BlockSpec((1, 1, d), per_b),
                  pl.BlockSpec((1, 1, d), per_b),
                  pl.BlockSpec((d, B_IN), const2),
                  pl.BlockSpec((1, B_IN), const2)],
        out_specs=[pl.BlockSpec((1, tm, HD), row),
                   pl.BlockSpec((1, tm, kvw), row),
                   pl.BlockSpec((1, tm, kvw), row)],
        out_shape=[jax.ShapeDtypeStruct((bsz, L, HD), BF16),
                   jax.ShapeDtypeStruct((bsz, L, kvw), BF16),
                   jax.ShapeDtypeStruct((bsz, L, kvw), BF16)],
        compiler_params=_cparams("parallel", "parallel"),
        name="b_inproj",
    )(x, g, sh, sc, w, b)


def _b_attn_kernel(sink_ref, q_ref, kp_ref, kc_ref, vp_ref, vc_ref, bias_ref, o_ref):
    n = pl.program_id(1)
    w = WINDOW
    g = N_HEADS // KV_HEADS
    q = q_ref[0]
    kb = jnp.concatenate([kp_ref[0], kc_ref[0]], axis=0)
    vb = jnp.concatenate([vp_ref[0], vc_ref[0]], axis=0)
    qpos = lax.broadcasted_iota(I32, (w, 2 * w), 0)
    kpos = lax.broadcasted_iota(I32, (w, 2 * w), 1)
    dist = qpos + w - kpos
    first_key = jnp.where(n > 0, 0, w)
    allowed = (dist >= 0) & (dist < WINDOW) & (kpos >= first_key)
    outs = []
    for h in range(N_HEADS):
        kh = h // g
        lg = lax.dot_general(q[:, h * HEAD_DIM:(h + 1) * HEAD_DIM],
                             kb[:, kh * HEAD_DIM:(kh + 1) * HEAD_DIM], _NT,
                             preferred_element_type=F32)
        lg = jnp.where(allowed, lg + bias_ref[h, 0], NEG)
        sink = sink_ref[0, h]
        m = jnp.maximum(jnp.max(lg, axis=-1, keepdims=True), sink)
        e = jnp.exp(lg - m)
        denom = jnp.sum(e, axis=-1, keepdims=True) + jnp.exp(sink - m)
        o = jnp.dot(e.astype(BF16), vb[:, kh * HEAD_DIM:(kh + 1) * HEAD_DIM],
                    preferred_element_type=F32)
        outs.append(o / denom)
    o_ref[0] = jnp.concatenate(outs, axis=1).astype(BF16)


def _b_attn(q, k, v, sinks, bias_b):
    bsz, L, _ = q.shape
    w = WINDOW
    kvw = KV_HEADS * HEAD_DIM
    cur = lambda b, n: (b, n, 0)
    prev = lambda b, n: (b, jnp.maximum(n - 1, 0), 0)
    return pl.pallas_call(
        _b_attn_kernel,
        grid=(bsz, L // w),
        in_specs=[pl.BlockSpec(memory_space=pltpu.SMEM),
                  pl.BlockSpec((1, w, HD), cur),
                  pl.BlockSpec((1, w, kvw), prev),
                  pl.BlockSpec((1, w, kvw), cur),
                  pl.BlockSpec((1, w, kvw), prev),
                  pl.BlockSpec((1, w, kvw), cur),
                  pl.BlockSpec((N_HEADS, 1, w, 2 * w), lambda b, n: (0, 0, 0, 0))],
        out_specs=pl.BlockSpec((1, w, HD), cur),
        out_shape=jax.ShapeDtypeStruct((bsz, L, HD), BF16),
        compiler_params=_cparams("parallel", "parallel"),
        name="b_attn",
    )(sinks.reshape(1, N_HEADS), q, k, k, v, v, bias_b)


def _out_proj_kernel(o_ref, w_ref, b_ref, gate_ref, x_ref, y_ref):
    y = jnp.dot(o_ref[0], w_ref[...], preferred_element_type=F32) + b_ref[...]
    y_ref[0] = x_ref[0] + gate_ref[0] * y


def _out_proj(o, w, b, gate, x, tm=512):
    bsz, L, d = x.shape
    row = lambda b_, i: (b_, i, 0)
    return pl.pallas_call(
        _out_proj_kernel,
        grid=(bsz, L // tm),
        in_specs=[pl.BlockSpec((1, tm, HD), row),
                  pl.BlockSpec((HD, d), lambda b_, i: (0, 0)),
                  pl.BlockSpec((1, d), lambda b_, i: (0, 0)),
                  pl.BlockSpec((1, 1, d), lambda b_, i: (b_, 0, 0)),
                  pl.BlockSpec((1, tm, d), row)],
        out_specs=pl.BlockSpec((1, tm, d), row),
        out_shape=jax.ShapeDtypeStruct((bsz, L, d), F32),
        compiler_params=_cparams("parallel", "parallel"),
        name="out_proj",
    )(o, w, b, gate, x)


FF_CHUNK = 256


def _ffn_kernel(x_ref, g_ref, sh_ref, sc_ref, gate_ref, w1_ref, w3_ref, w2_ref, fg_ref, y_ref,
                *, final_norm):
    x = x_ref[0]
    h = _normmod(x, g_ref[...], sh_ref[0], sc_ref[0]).astype(BF16)
    acc = jnp.zeros(x.shape, F32)
    for c in range(D_FF // FF_CHUNK):
        cs = slice(c * FF_CHUNK, (c + 1) * FF_CHUNK)
        a1 = jnp.dot(h, w1_ref[:, cs], preferred_element_type=F32)
        a3 = jnp.dot(h, w3_ref[:, cs], preferred_element_type=F32)
        act = (a1 * jax.nn.sigmoid(a1)) * a3
        acc = acc + jnp.dot(act.astype(BF16), w2_ref[cs, :], preferred_element_type=F32)
    y = x + gate_ref[0] * acc
    if final_norm:
        y = (y * lax.rsqrt(jnp.mean(y * y, axis=-1, keepdims=True) + RMS_EPS)) * fg_ref[...]
    y_ref[0] = y


def _ffn(x, g, sh, sc, gate, w1, w3, w2, final_g, final_norm, tm=512):
    bsz, L, d = x.shape
    row = lambda b_, i: (b_, i, 0)
    per_b = lambda b_, i: (b_, 0, 0)
    const2 = lambda b_, i: (0, 0)
    return pl.pallas_call(
        functools.partial(_ffn_kernel, final_norm=final_norm),
        grid=(bsz, L // tm),
        in_specs=[pl.BlockSpec((1, tm, d), row),
                  pl.BlockSpec((1, d), const2),
                  pl.BlockSpec((1, 1, d), per_b),
                  pl.BlockSpec((1, 1, d), per_b),
                  pl.BlockSpec((1, 1, d), per_b),
                  pl.BlockSpec((d, D_FF), const2),
                  pl.BlockSpec((d, D_FF), const2),
                  pl.BlockSpec((D_FF, d), const2),
                  pl.BlockSpec((1, d), const2)],
        out_specs=pl.BlockSpec((1, tm, d), row),
        out_shape=jax.ShapeDtypeStruct((bsz, L, d), F32),
        compiler_params=_cparams("parallel", "parallel"),
        name="ffn",
    )(x, g, sh, sc, gate, w1, w3, w2, final_g)


def kernel(x, c, rel_bias, w_ada, b_ada, norm_mix_g, norm_ffn_g, a_w_in, a_kv_norm_g, a_w_uk,
           a_w_uv, a_idx_k_g, a_idx_k_b, a_w_out, b_w_in, b_b_in, b_sinks, b_w_out, b_b_out,
           ffn_w1, ffn_w3, ffn_w2, norm_final_g):
    bsz, L, d = x.shape
    depth = w_ada.shape[0]
    topk = min(INDEX_TOPK, L // 4)

    mod = _adaln(c, w_ada, b_ada)

    r = jnp.arange(BIAS_BLK)
    ids_a = jnp.stack([_t5_bucket(r[:, None] - r[None, :]),
                       _t5_bucket(r[:, None] - r[None, :] + BIAS_BLK)])
    bias_a = _bias_tiles(rel_bias, ids_a, log2_shifted=True)
    ids_b = _t5_bucket(jnp.arange(WINDOW)[:, None] + WINDOW - jnp.arange(2 * WINDOW)[None, :])
    bias_b = _bias_tiles(rel_bias, ids_b[None], log2_shifted=False)

    zero_bias = jnp.zeros((1, d), F32)
    for i in range(depth):
        sh1, sc1, g1, sh2, sc2, g2 = [m.reshape(bsz, 1, d) for m in jnp.split(mod[i], 6, axis=-1)]
        jm = i // 2
        if i % 2 == 0:
            w_in = a_w_in[jm]
            w_main = jnp.pad(w_in[:, :A_MAIN], ((0, 0), (0, A_MAIN_PAD - A_MAIN))).astype(BF16)
            w_wi_t = jnp.pad(w_in[:, A_WI0:A_WI0 + IDX_HEADS].T,
                             ((0, 16 - IDX_HEADS), (0, 0))).astype(BF16)
            q, ckv, qi, ki, wit = _a_inproj(
                x, norm_mix_g[i][None], sh1, sc1, w_main, w_wi_t, a_kv_norm_g[jm][None],
                a_idx_k_g[jm][None], a_idx_k_b[jm][None])
            mask = _a_index(qi, wit, ki, topk)
            o = _a_attn(q, ckv, mask, a_w_uk[jm].astype(BF16), a_w_uv[jm].astype(BF16), bias_a)
            x = _out_proj(o, a_w_out[jm].astype(BF16), zero_bias, g1, x)
        else:
            q, k, v = _b_inproj(x, norm_mix_g[i][None], sh1, sc1, b_w_in[jm].astype(BF16),
                                b_b_in[jm][None])
            o = _b_attn(q, k, v, b_sinks[jm], bias_b)
            x = _out_proj(o, b_w_out[jm].astype(BF16), b_b_out[jm][None], g1, x)
        x = _ffn(x, norm_ffn_g[i][None], sh2, sc2, g2, ffn_w1[i].astype(BF16),
                 ffn_w3[i].astype(BF16), ffn_w2[i].astype(BF16), norm_final_g[None],
                 final_norm=(i == depth - 1))
    return x
```

```python
import functools
import math

import numpy as np
import jax
import jax.numpy as jnp
from jax import lax
from jax.experimental import pallas as pl
from jax.experimental.pallas import tpu as pltpu

D_MODEL = 1024
N_HEADS = 16
HEAD_DIM = 64
KV_RANK = 256
IDX_HEADS = 8
IDX_DIM = 64
INDEX_TOPK = 256
KV_HEADS = 2
WINDOW = 128
N_BUCKETS = 32
MAX_DISTANCE = 128
D_FF = 2816
RMS_EPS = 1e-6
NEG = -1e30

HD = N_HEADS * HEAD_DIM
A_Q0, A_KV0, A_QI0, A_KI0, A_WI0 = 0, HD, HD + KV_RANK, HD + KV_RANK + IDX_HEADS * IDX_DIM, \
    HD + KV_RANK + IDX_HEADS * IDX_DIM + IDX_DIM
A_MAIN = A_WI0
A_MAIN_PAD = 1920
B_IN = (N_HEADS + 2 * KV_HEADS) * HEAD_DIM

CHUNK = 256
PV_ROWS = 64
LANES = 128
BIAS_BLK = 128
VMEM_LIMIT = 56 * 1024 * 1024

F32 = jnp.float32
BF16 = jnp.bfloat16
I32 = jnp.int32

_NT = (((1,), (1,)), ((), ()))


def _cparams(*sem):
    return pltpu.CompilerParams(dimension_semantics=sem, vmem_limit_bytes=VMEM_LIMIT)


def _f32_key(v):
    b = int(np.array(v, np.float32).view(np.int32))
    return b ^ ((b >> 31) & 0x7FFFFFFF)


LOG2E = math.log2(math.e)
KEY_NEG = _f32_key(NEG)
INT_MIN = -(2 ** 31)


def _adaln_kernel(c_ref, w_ref, b_ref, o_ref):
    c = c_ref[...]
    cs = c * jax.nn.sigmoid(c)
    o_ref[0] = jnp.dot(cs, w_ref[0], preferred_element_type=F32,
                       precision=lax.Precision.HIGHEST) + b_ref[0]


def _adaln(c, w_ada, b_ada):
    depth, d, n = w_ada.shape
    bsz = c.shape[0]
    tn = 1536
    return pl.pallas_call(
        _adaln_kernel,
        grid=(depth, n // tn),
        in_specs=[pl.BlockSpec((bsz, d), lambda i, j: (0, 0)),
                  pl.BlockSpec((1, d, tn), lambda i, j: (i, 0, j)),
                  pl.BlockSpec((1, 1, tn), lambda i, j: (i, 0, j))],
        out_specs=pl.BlockSpec((1, bsz, tn), lambda i, j: (i, 0, j)),
        out_shape=jax.ShapeDtypeStruct((depth, bsz, n), F32),
        compiler_params=_cparams("parallel", "parallel"),
        name="adaln",
    )(c, w_ada, b_ada.reshape(depth, 1, n))


def _normmod(x, g, sh, sc):
    ms = jnp.mean(x * x, axis=-1, keepdims=True)
    y = (x * lax.rsqrt(ms + RMS_EPS)) * g
    return y * (1.0 + sc) + sh


def _a_inproj_kernel(x_ref, g_ref, sh_ref, sc_ref, w_ref, wwi_ref, kvg_ref, ikg_ref, ikb_ref,
                     q_ref, ckv_ref, qi_ref, ki_ref, wit_ref):
    h = _normmod(x_ref[0], g_ref[...], sh_ref[0], sc_ref[0]).astype(BF16)
    proj = jnp.dot(h, w_ref[...], preferred_element_type=F32)
    q_ref[0] = (proj[:, A_Q0:A_KV0] * (HEAD_DIM ** -0.5)).astype(BF16)
    ckv = proj[:, A_KV0:A_QI0]
    ckv = (ckv * lax.rsqrt(jnp.mean(ckv * ckv, axis=-1, keepdims=True) + RMS_EPS)) * kvg_ref[...]
    ckv_ref[0] = ckv.astype(BF16)
    qi_ref[0] = proj[:, A_QI0:A_KI0].astype(BF16)
    ki = proj[:, A_KI0:A_WI0]
    mu = jnp.mean(ki, axis=-1, keepdims=True)
    var = jnp.mean(jnp.square(ki - mu), axis=-1, keepdims=True)
    ki = ((ki - mu) * lax.rsqrt(var + RMS_EPS)) * ikg_ref[...] + ikb_ref[...]
    ki_ref[0] = ki.astype(BF16)
    wit = lax.dot_general(wwi_ref[...], h, _NT, preferred_element_type=F32)
    wit_ref[0] = wit[:IDX_HEADS] * (IDX_HEADS ** -0.5 * IDX_DIM ** -0.5)


def _a_inproj(x, g, sh, sc, w_main, w_wi_t, kv_g, ik_g, ik_b, tm=512):
    bsz, L, d = x.shape
    row = lambda b, i: (b, i, 0)
    per_b = lambda b, i: (b, 0, 0)
    const2 = lambda b, i: (0, 0)
    return pl.pallas_call(
        _a_inproj_kernel,
        grid=(bsz, L // tm),
        in_specs=[pl.BlockSpec((1, tm, d), row),
                  pl.BlockSpec((1, d), const2),
                  pl.BlockSpec((1, 1, d), per_b),
                  pl.BlockSpec((1, 1, d), per_b),
                  pl.BlockSpec((d, A_MAIN_PAD), const2),
                  pl.BlockSpec((16, d), const2),
                  pl.BlockSpec((1, KV_RANK), const2),
                  pl.BlockSpec((1, IDX_DIM), const2),
                  pl.BlockSpec((1, IDX_DIM), const2)],
        out_specs=[pl.BlockSpec((1, tm, HD), row),
                   pl.BlockSpec((1, tm, KV_RANK), row),
                   pl.BlockSpec((1, tm, IDX_HEADS * IDX_DIM), row),
                   pl.BlockSpec((1, tm, IDX_DIM), row),
                   pl.BlockSpec((1, IDX_HEADS, tm), lambda b, i: (b, 0, i))],
        out_shape=[jax.ShapeDtypeStruct((bsz, L, HD), BF16),
                   jax.ShapeDtypeStruct((bsz, L, KV_RANK), BF16),
                   jax.ShapeDtypeStruct((bsz, L, IDX_HEADS * IDX_DIM), BF16),
                   jax.ShapeDtypeStruct((bsz, L, IDX_DIM), BF16),
                   jax.ShapeDtypeStruct((bsz, IDX_HEADS, L), F32)],
        compiler_params=_cparams("parallel", "parallel"),
        name="a_inproj",
    )(x, g, sh, sc, w_main, w_wi_t, kv_g, ik_g, ik_b)


def _a_index_kernel(qi_ref, wit_ref, ki_ref, tri_ref, mask_ref, key_scr, *, topk, seq_len):
    j = pl.program_id(1)
    nchunk = seq_len // CHUNK
    qi = qi_ref[0]
    wit = wit_ref[0]
    t_glob = j * CHUNK + lax.broadcasted_iota(I32, (CHUNK, CHUNK), 1)
    s_loc = lax.broadcasted_iota(I32, (CHUNK, CHUNK), 0)

    def score_chunk(kc, carry):
        kik = ki_ref[0, pl.ds(pl.multiple_of(kc * CHUNK, CHUNK), CHUNK), :]
        acc = jnp.zeros((CHUNK, CHUNK), F32)
        for h in range(IDX_HEADS):
            r = lax.dot_general(kik, qi[:, h * IDX_DIM:(h + 1) * IDX_DIM], _NT,
                                preferred_element_type=F32)
            acc = acc + jnp.maximum(r, 0.0) * wit[h:h + 1, :]
        sc = jnp.where(kc * CHUNK + s_loc <= t_glob, acc, NEG)
        bits = lax.bitcast_convert_type(sc, I32)
        key_scr[kc] = bits ^ ((bits >> 31) & 0x7FFFFFFF)
        return carry

    lax.fori_loop(0, j + 1, score_chunk, 0)

    n_beyond = seq_len - (j + 1) * CHUNK

    def count(pred, thr):
        def body(kc, acc):
            m = jnp.where(pred(key_scr[kc], thr), jnp.int32(1), jnp.int32(0))
            return acc + m.reshape(CHUNK // 8, 8, CHUNK).sum(axis=0)
        acc = lax.fori_loop(0, j + 1, body, jnp.zeros((8, CHUNK), I32))
        cnt = acc.sum(axis=0, keepdims=True)
        return cnt + jnp.where(pred(KEY_NEG, thr), n_beyond, 0)

    ge = lambda a, b: a >= b
    gt = lambda a, b: a > b

    zero = jnp.zeros((1, CHUNK), I32)
    thr = jnp.where(count(ge, zero) >= topk, zero, jnp.full((1, CHUNK), INT_MIN, I32))

    def bisect(i, thr):
        cand = thr + lax.shift_left(jnp.int32(1), 30 - i)
        return jnp.where(count(ge, cand) >= topk, cand, thr)

    thr = lax.fori_loop(0, 31, bisect, thr)

    need = (topk - count(gt, thr)).astype(F32)
    tri = tri_ref[...]

    def emit(kc, seen):
        key = key_scr[kc]
        eq = jnp.where(key == thr, 1.0, 0.0)
        rank = jnp.dot(tri, eq.astype(BF16), preferred_element_type=F32) + seen
        take = jnp.where(key > thr, 1.0, jnp.where(rank < need, eq, 0.0))
        causal = kc * CHUNK + s_loc <= t_glob
        m = jnp.where(causal, jnp.where(take > 0.5, 0.0, NEG), NEG)
        mask_ref[0, kc] = m.T
        return seen + jnp.sum(eq, axis=0, keepdims=True)

    lax.fori_loop(0, j + 1, emit, jnp.zeros((1, CHUNK), F32))

    def fill(kc, carry):
        mask_ref[0, kc] = jnp.full((CHUNK, CHUNK), NEG, F32)
        return carry

    lax.fori_loop(j + 1, nchunk, fill, 0)


def _a_index(qi, wit, ki, topk):
    bsz, L, _ = qi.shape
    nchunk = L // CHUNK
    tri = jnp.tril(jnp.ones((CHUNK, CHUNK), BF16), -1)
    return pl.pallas_call(
        functools.partial(_a_index_kernel, topk=topk, seq_len=L),
        grid=(bsz, nchunk),
        in_specs=[pl.BlockSpec((1, CHUNK, IDX_HEADS * IDX_DIM), lambda b, j: (b, j, 0)),
                  pl.BlockSpec((1, IDX_HEADS, CHUNK), lambda b, j: (b, 0, j)),
                  pl.BlockSpec((1, L, IDX_DIM), lambda b, j: (b, 0, 0)),
                  pl.BlockSpec((CHUNK, CHUNK), lambda b, j: (0, 0))],
        out_specs=pl.BlockSpec((1, nchunk, CHUNK, CHUNK), lambda b, j: (b, 0, j, 0)),
        out_shape=jax.ShapeDtypeStruct((bsz, nchunk, L, CHUNK), F32),
        scratch_shapes=[pltpu.VMEM((nchunk, CHUNK, CHUNK), I32)],
        compiler_params=_cparams("parallel", "parallel"),
        name="a_index",
    )(qi, wit, ki, tri)


def _a_attn_kernel(q_ref, ckv_ref, mask_ref, wuk_ref, wuv_ref, bias_ref, o_ref,
                   qabs_scr, lg_scr, m_scr, l_scr, acc_scr, p_scr, *, hc):
    j = pl.program_id(1)
    tq = CHUNK
    q = q_ref[0]
    for i in range(hc):
        qa = lax.dot_general(q[:, i * HEAD_DIM:(i + 1) * HEAD_DIM], wuk_ref[i], _NT,
                             preferred_element_type=F32)
        qabs_scr[i * tq:(i + 1) * tq, :] = (qa * LOG2E).astype(BF16)
    qg = qabs_scr[...]

    n_far_pairs = lax.shift_right_arithmetic(j - 1, 1)
    odd = (j & 1) == 0

    def keys(kc, width):
        return ckv_ref[0, pl.ds(kc, width)].reshape(width * CHUNK, KV_RANK)

    def logits(kc, width):
        lg = lax.dot_general(qg, keys(kc, width), _NT, preferred_element_type=F32)
        mk = jnp.concatenate([mask_ref[0, kc + w] for w in range(width)], axis=1)
        return lg + jnp.concatenate([mk] * hc, axis=0)

    def near_bias(i, with_prev):
        d0, d1 = bias_ref[i, 0], bias_ref[i, 1]
        z = jnp.zeros_like(d0)
        top, bot = [d0, z], [d1, d0]
        if with_prev:
            top, bot = [z, d1] + top, [z, z] + bot
        return jnp.concatenate([jnp.concatenate(top, axis=1), jnp.concatenate(bot, axis=1)], axis=0)

    def put_logits(kc, width, lg):
        for w in range(width):
            lg_scr[kc + w] = lg[:, w * CHUNK:(w + 1) * CHUNK]
        m = m_scr[...]
        for c in range(width * CHUNK // LANES):
            m = jnp.maximum(m, lg[:, c * LANES:(c + 1) * LANES])
        m_scr[...] = m

    m_scr[...] = jnp.full(m_scr.shape, -jnp.inf, F32)

    @pl.when(j >= 1)
    def _():
        bias = jnp.concatenate([near_bias(i, True) for i in range(hc)], axis=0)
        put_logits(j - 1, 2, logits(j - 1, 2) + bias)

    def far_pair(i, carry):
        kc = j - 3 - 2 * i
        put_logits(kc, 2, logits(kc, 2))
        return carry

    lax.fori_loop(0, n_far_pairs, far_pair, 0)

    @pl.when(odd)
    def _():
        bias = jnp.concatenate([near_bias(i, False) for i in range(hc)], axis=0)
        put_logits(0, 1, logits(0, 1) + jnp.where(j == 0, bias, 0.0))

    m_scr[...] = jnp.broadcast_to(jnp.max(m_scr[...], axis=-1, keepdims=True), m_scr.shape)
    l_scr[...] = jnp.zeros_like(l_scr)
    acc_scr[...] = jnp.zeros_like(acc_scr)

    def pv(kc, width):
        for r in range(hc * tq // PV_ROWS):
            rows = slice(r * PV_ROWS, (r + 1) * PV_ROWS)
            m = m_scr[rows, :]
            l = l_scr[rows, :]
            for w in range(width):
                for c in range(CHUNK // LANES):
                    cols = slice(c * LANES, (c + 1) * LANES)
                    p = jnp.exp2(lg_scr[kc + w, rows, cols] - m)
                    l = l + p
                    p_scr[rows, w * CHUNK + c * LANES:w * CHUNK + (c + 1) * LANES] = p.astype(BF16)
            l_scr[rows, :] = l
        acc_scr[...] += jnp.dot(p_scr[:, :width * CHUNK], keys(kc, width),
                                preferred_element_type=F32)

    @pl.when(j >= 1)
    def _():
        pv(j - 1, 2)

    def far_pv(i, carry):
        pv(j - 3 - 2 * i, 2)
        return carry

    lax.fori_loop(0, n_far_pairs, far_pv, 0)

    @pl.when(odd)
    def _():
        pv(0, 1)

    o = acc_scr[...] / jnp.sum(l_scr[...], axis=-1, keepdims=True)
    outs = [jnp.dot(o[i * tq:(i + 1) * tq].astype(BF16), wuv_ref[i], preferred_element_type=F32)
            for i in range(hc)]
    o_ref[0] = jnp.concatenate(outs, axis=1).astype(BF16)


def _a_attn(q, ckv, mask, w_uk, w_uv, bias_nd, hc=8):
    bsz, L, _ = q.shape
    nchunk = L // CHUNK
    hg = N_HEADS // hc
    m = hc * CHUNK
    ckv4 = ckv.reshape(bsz, nchunk, CHUNK, KV_RANK)
    return pl.pallas_call(
        functools.partial(_a_attn_kernel, hc=hc),
        grid=(bsz, nchunk, hg),
        in_specs=[pl.BlockSpec((1, CHUNK, hc * HEAD_DIM), lambda b, j, g: (b, j, g)),
                  pl.BlockSpec((1, nchunk, CHUNK, KV_RANK), lambda b, j, g: (b, 0, 0, 0)),
                  pl.BlockSpec((1, nchunk, CHUNK, CHUNK), lambda b, j, g: (b, 0, j, 0)),
                  pl.BlockSpec((hc, KV_RANK, HEAD_DIM), lambda b, j, g: (g, 0, 0)),
                  pl.BlockSpec((hc, KV_RANK, HEAD_DIM), lambda b, j, g: (g, 0, 0)),
                  pl.BlockSpec((hc, 2, BIAS_BLK, BIAS_BLK), lambda b, j, g: (g, 0, 0, 0))],
        out_specs=pl.BlockSpec((1, CHUNK, hc * HEAD_DIM), lambda b, j, g: (b, j, g)),
        out_shape=jax.ShapeDtypeStruct((bsz, L, HD), BF16),
        scratch_shapes=[pltpu.VMEM((m, KV_RANK), BF16),
                        pltpu.VMEM((nchunk, m, CHUNK), F32),
                        pltpu.VMEM((m, LANES), F32),
                        pltpu.VMEM((m, LANES), F32),
                        pltpu.VMEM((m, KV_RANK), F32),
                        pltpu.VMEM((m, 2 * CHUNK), BF16)],
        compiler_params=_cparams("parallel", "parallel", "arbitrary"),
        name="a_attn",
    )(q, ckv4, mask, w_uk, w_uv, bias_nd)


def _bias_kernel(rb_ref, ids_ref, o_ref, *, n_tiles, far_bucket, shift_far):
    h = pl.program_id(0)
    far = rb_ref[far_bucket, h] if shift_far else 0.0
    for t in range(n_tiles):
        ids = ids_ref[t]
        out = jnp.zeros(ids.shape, F32)
        for b in range(N_BUCKETS):
            out = jnp.where(ids == b, rb_ref[b, h] - far, out)
        o_ref[0, t] = jnp.where(ids < 0, NEG, out * LOG2E)


def _t5_bucket(dist):
    max_exact = N_BUCKETS // 2
    d = jnp.maximum(dist, 0)
    large = max_exact + (jnp.log(jnp.maximum(d, 1).astype(F32) / max_exact)
                         / math.log(MAX_DISTANCE / max_exact)
                         * (N_BUCKETS - max_exact)).astype(I32)
    large = jnp.minimum(large, N_BUCKETS - 1)
    return jnp.where(d < max_exact, d, large)


def _bias_tiles(rel_bias, ids, shift_far):
    n_tiles, r, c = ids.shape
    return pl.pallas_call(
        functools.partial(_bias_kernel, n_tiles=n_tiles, far_bucket=N_BUCKETS - 1,
                          shift_far=shift_far),
        grid=(N_HEADS,),
        in_specs=[pl.BlockSpec(memory_space=pltpu.SMEM),
                  pl.BlockSpec((n_tiles, r, c), lambda h: (0, 0, 0))],
        out_specs=pl.BlockSpec((1, n_tiles, r, c), lambda h: (h, 0, 0, 0)),
        out_shape=jax.ShapeDtypeStruct((N_HEADS, n_tiles, r, c), F32),
        compiler_params=_cparams("parallel"),
        name="bias_tiles",
    )(rel_bias, ids)


def _b_inproj_kernel(x_ref, g_ref, sh_ref, sc_ref, w_ref, b_ref, q_ref, k_ref, v_ref):
    h = _normmod(x_ref[0], g_ref[...], sh_ref[0], sc_ref[0]).astype(BF16)
    proj = jnp.dot(h, w_ref[...], preferred_element_type=F32) + b_ref[...]
    q_ref[0] = (proj[:, :HD] * (HEAD_DIM ** -0.5 * LOG2E)).astype(BF16)
    k_ref[0] = proj[:, HD:HD + KV_HEADS * HEAD_DIM].astype(BF16)
    v_ref[0] = proj[:, HD + KV_HEADS * HEAD_DIM:].astype(BF16)


def _b_inproj(x, g, sh, sc, w, b, tm=512):
    bsz, L, d = x.shape
    kvw = KV_HEADS * HEAD_DIM
    row = lambda b_, i: (b_, i, 0)
    per_b = lambda b_, i: (b_, 0, 0)
    const2 = lambda b_, i: (0, 0)
    return pl.pallas_call(
        _b_inproj_kernel,
        grid=(bsz, L // tm),
        in_specs=[pl.BlockSpec((1, tm, d), row),
                  pl.BlockSpec((1, d), const2),
                  pl.BlockSpec((1, 1, d), per_b),
                  pl.BlockSpec((1, 1, d), per_b),
                  pl.BlockSpec((d, B_IN), const2),
                  pl.BlockSpec((1, B_IN), const2)],
        out_specs=[pl.BlockSpec((1, tm, HD), row),
                   pl.BlockSpec((1, tm, kvw), row),
                   pl.BlockSpec((1, tm, kvw), row)],
        out_shape=[jax.ShapeDtypeStruct((bsz, L, HD), BF16),
                   jax.ShapeDtypeStruct((bsz, L, kvw), BF16),
                   jax.ShapeDtypeStruct((bsz, L, kvw), BF16)],
        compiler_params=_cparams("parallel", "parallel"),
        name="b_inproj",
    )(x, g, sh, sc, w, b)


def _b_attn_kernel(sink_ref, q_ref, kp_ref, kc_ref, vp_ref, vc_ref, bias_ref, o_ref):
    n = pl.program_id(1)
    w = WINDOW
    g = N_HEADS // KV_HEADS
    q = q_ref[0]
    kb = jnp.concatenate([kp_ref[0], kc_ref[0]], axis=0)
    vb = jnp.concatenate([vp_ref[0], vc_ref[0]], axis=0)
    variant = jnp.where(n > 0, 0, 1)
    outs = []
    for h in range(N_HEADS):
        kh = h // g
        lg = lax.dot_general(q[:, h * HEAD_DIM:(h + 1) * HEAD_DIM],
                             kb[:, kh * HEAD_DIM:(kh + 1) * HEAD_DIM], _NT,
                             preferred_element_type=F32)
        lg = lg + bias_ref[h, variant]
        sink = sink_ref[0, h] * LOG2E
        m = jnp.maximum(jnp.max(lg, axis=-1, keepdims=True), sink)
        e = jnp.exp2(lg - m)
        denom = jnp.sum(e, axis=-1, keepdims=True) + jnp.exp2(sink - m)
        o = jnp.dot(e.astype(BF16), vb[:, kh * HEAD_DIM:(kh + 1) * HEAD_DIM],
                    preferred_element_type=F32)
        outs.append(o / denom)
    o_ref[0] = jnp.concatenate(outs, axis=1).astype(BF16)


def _b_attn(q, k, v, sinks, bias_b):
    bsz, L, _ = q.shape
    w = WINDOW
    kvw = KV_HEADS * HEAD_DIM
    cur = lambda b, n: (b, n, 0)
    prev = lambda b, n: (b, jnp.maximum(n - 1, 0), 0)
    return pl.pallas_call(
        _b_attn_kernel,
        grid=(bsz, L // w),
        in_specs=[pl.BlockSpec(memory_space=pltpu.SMEM),
                  pl.BlockSpec((1, w, HD), cur),
                  pl.BlockSpec((1, w, kvw), prev),
                  pl.BlockSpec((1, w, kvw), cur),
                  pl.BlockSpec((1, w, kvw), prev),
                  pl.BlockSpec((1, w, kvw), cur),
                  pl.BlockSpec((N_HEADS, 2, w, 2 * w), lambda b, n: (0, 0, 0, 0))],
        out_specs=pl.BlockSpec((1, w, HD), cur),
        out_shape=jax.ShapeDtypeStruct((bsz, L, HD), BF16),
        compiler_params=_cparams("parallel", "parallel"),
        name="b_attn",
    )(sinks.reshape(1, N_HEADS), q, k, k, v, v, bias_b)


FF_CHUNK = 256


def _out_ffn_kernel(o_ref, wo_ref, bo_ref, g1_ref, x_ref, g_ref, sh_ref, sc_ref, gate_ref,
                    w1_ref, w3_ref, w2_ref, fg_ref, y_ref, *, final_norm):
    mix = jnp.dot(o_ref[0], wo_ref[...], preferred_element_type=F32) + bo_ref[...]
    x = x_ref[0] + g1_ref[0] * mix
    h = _normmod(x, g_ref[...], sh_ref[0], sc_ref[0]).astype(BF16)
    acc = jnp.zeros(x.shape, F32)
    for c in range(D_FF // FF_CHUNK):
        cs = slice(c * FF_CHUNK, (c + 1) * FF_CHUNK)
        a1 = jnp.dot(h, w1_ref[:, cs], preferred_element_type=F32)
        a3 = jnp.dot(h, w3_ref[:, cs], preferred_element_type=F32)
        act = (a1 * jax.nn.sigmoid(a1)) * a3
        acc = acc + jnp.dot(act.astype(BF16), w2_ref[cs, :], preferred_element_type=F32)
    y = x + gate_ref[0] * acc
    if final_norm:
        y = (y * lax.rsqrt(jnp.mean(y * y, axis=-1, keepdims=True) + RMS_EPS)) * fg_ref[...]
    y_ref[0] = y


def _out_ffn(o, w_out, b_out, g1, x, g, sh, sc, gate, w1, w3, w2, final_g, final_norm, tm=512):
    bsz, L, d = x.shape
    row = lambda b_, i: (b_, i, 0)
    per_b = lambda b_, i: (b_, 0, 0)
    const2 = lambda b_, i: (0, 0)
    return pl.pallas_call(
        functools.partial(_out_ffn_kernel, final_norm=final_norm),
        grid=(bsz, L // tm),
        in_specs=[pl.BlockSpec((1, tm, HD), row),
                  pl.BlockSpec((HD, d), const2),
                  pl.BlockSpec((1, d), const2),
                  pl.BlockSpec((1, 1, d), per_b),
                  pl.BlockSpec((1, tm, d), row),
                  pl.BlockSpec((1, d), const2),
                  pl.BlockSpec((1, 1, d), per_b),
                  pl.BlockSpec((1, 1, d), per_b),
                  pl.BlockSpec((1, 1, d), per_b),
                  pl.BlockSpec((d, D_FF), const2),
                  pl.BlockSpec((d, D_FF), const2),
                  pl.BlockSpec((D_FF, d), const2),
                  pl.BlockSpec((1, d), const2)],
        out_specs=pl.BlockSpec((1, tm, d), row),
        out_shape=jax.ShapeDtypeStruct((bsz, L, d), F32),
        compiler_params=_cparams("parallel", "parallel"),
        name="out_ffn",
    )(o, w_out, b_out, g1, x, g, sh, sc, gate, w1, w3, w2, final_g)


def kernel(x, c, rel_bias, w_ada, b_ada, norm_mix_g, norm_ffn_g, a_w_in, a_kv_norm_g, a_w_uk,
           a_w_uv, a_idx_k_g, a_idx_k_b, a_w_out, b_w_in, b_b_in, b_sinks, b_w_out, b_b_out,
           ffn_w1, ffn_w3, ffn_w2, norm_final_g):
    bsz, L, d = x.shape
    depth = w_ada.shape[0]
    topk = min(INDEX_TOPK, L // 4)

    mod = _adaln(c, w_ada, b_ada)

    r = jnp.arange(BIAS_BLK)
    ids_a = jnp.stack([_t5_bucket(r[:, None] - r[None, :]),
                       _t5_bucket(r[:, None] - r[None, :] + BIAS_BLK)])
    bias_a = _bias_tiles(rel_bias, ids_a, shift_far=True)
    dist_b = jnp.arange(WINDOW)[:, None] + WINDOW - jnp.arange(2 * WINDOW)[None, :]
    in_window = (dist_b >= 0) & (dist_b < WINDOW)
    ids_b = jnp.where(in_window, _t5_bucket(dist_b), -1)
    ids_b0 = jnp.where(jnp.arange(2 * WINDOW)[None, :] >= WINDOW, ids_b, -1)
    bias_b = _bias_tiles(rel_bias, jnp.stack([ids_b, ids_b0]), shift_far=False)

    zero_bias = jnp.zeros((1, d), F32)
    for i in range(depth):
        sh1, sc1, g1, sh2, sc2, g2 = [m.reshape(bsz, 1, d) for m in jnp.split(mod[i], 6, axis=-1)]
        jm = i // 2
        if i % 2 == 0:
            w_in = a_w_in[jm]
            w_main = jnp.pad(w_in[:, :A_MAIN], ((0, 0), (0, A_MAIN_PAD - A_MAIN))).astype(BF16)
            w_wi_t = jnp.pad(w_in[:, A_WI0:A_WI0 + IDX_HEADS].T,
                             ((0, 16 - IDX_HEADS), (0, 0))).astype(BF16)
            q, ckv, qi, ki, wit = _a_inproj(
                x, norm_mix_g[i][None], sh1, sc1, w_main, w_wi_t, a_kv_norm_g[jm][None],
                a_idx_k_g[jm][None], a_idx_k_b[jm][None])
            mask = _a_index(qi, wit, ki, topk)
            o = _a_attn(q, ckv, mask, a_w_uk[jm].astype(BF16), a_w_uv[jm].astype(BF16), bias_a)
            w_out, b_out = a_w_out[jm], zero_bias
        else:
            q, k, v = _b_inproj(x, norm_mix_g[i][None], sh1, sc1, b_w_in[jm].astype(BF16),
                                b_b_in[jm][None])
            o = _b_attn(q, k, v, b_sinks[jm], bias_b)
            w_out, b_out = b_w_out[jm], b_b_out[jm][None]
        x = _out_ffn(o, w_out.astype(BF16), b_out, g1, x, norm_ffn_g[i][None], sh2, sc2, g2,
                     ffn_w1[i].astype(BF16), ffn_w3[i].astype(BF16), ffn_w2[i].astype(BF16),
                     norm_final_g[None], final_norm=(i == depth - 1))
    return x
```

```python
import functools
import math

import numpy as np
import jax
import jax.numpy as jnp
from jax import lax
from jax.experimental import pallas as pl
from jax.experimental.pallas import tpu as pltpu

D_MODEL = 1024
N_HEADS = 16
HEAD_DIM = 64
KV_RANK = 256
IDX_HEADS = 8
IDX_DIM = 64
INDEX_TOPK = 256
KV_HEADS = 2
WINDOW = 128
N_BUCKETS = 32
MAX_DISTANCE = 128
D_FF = 2816
RMS_EPS = 1e-6
NEG = -1e30

HD = N_HEADS * HEAD_DIM
A_Q0, A_KV0, A_QI0, A_KI0, A_WI0 = 0, HD, HD + KV_RANK, HD + KV_RANK + IDX_HEADS * IDX_DIM, \
    HD + KV_RANK + IDX_HEADS * IDX_DIM + IDX_DIM
A_MAIN = A_WI0
A_MAIN_PAD = 1920
B_IN = (N_HEADS + 2 * KV_HEADS) * HEAD_DIM

CHUNK = 256
PV_ROWS = 64
HALF_ROWS = 16
LANES = 128
BIAS_BLK = 128
VMEM_LIMIT = 56 * 1024 * 1024

F32 = jnp.float32
BF16 = jnp.bfloat16
I32 = jnp.int32
I16 = jnp.int16

_NT = (((1,), (1,)), ((), ()))


def _cparams(*sem):
    return pltpu.CompilerParams(dimension_semantics=sem, vmem_limit_bytes=VMEM_LIMIT)


def _f32_key(v):
    b = int(np.array(v, np.float32).view(np.int32))
    return b ^ ((b >> 31) & 0x7FFFFFFF)


LOG2E = math.log2(math.e)
KEY_NEG = _f32_key(NEG)
INT_MIN = -(2 ** 31)


def _adaln_kernel(c_ref, w_ref, b_ref, o_ref):
    c = c_ref[...]
    cs = c * jax.nn.sigmoid(c)
    o_ref[0] = jnp.dot(cs, w_ref[0], preferred_element_type=F32,
                       precision=lax.Precision.HIGHEST) + b_ref[0]


def _adaln(c, w_ada, b_ada):
    depth, d, n = w_ada.shape
    bsz = c.shape[0]
    tn = 1536
    return pl.pallas_call(
        _adaln_kernel,
        grid=(depth, n // tn),
        in_specs=[pl.BlockSpec((bsz, d), lambda i, j: (0, 0)),
                  pl.BlockSpec((1, d, tn), lambda i, j: (i, 0, j)),
                  pl.BlockSpec((1, 1, tn), lambda i, j: (i, 0, j))],
        out_specs=pl.BlockSpec((1, bsz, tn), lambda i, j: (i, 0, j)),
        out_shape=jax.ShapeDtypeStruct((depth, bsz, n), F32),
        compiler_params=_cparams("parallel", "parallel"),
        name="adaln",
    )(c, w_ada, b_ada.reshape(depth, 1, n))


def _normmod(x, g, sh, sc):
    ms = jnp.mean(x * x, axis=-1, keepdims=True)
    y = (x * lax.rsqrt(ms + RMS_EPS)) * g
    return y * (1.0 + sc) + sh


def _a_inproj_kernel(x_ref, g_ref, sh_ref, sc_ref, w_ref, wwi_ref, kvg_ref, ikg_ref, ikb_ref,
                     q_ref, ckv_ref, qi_ref, ki_ref, wit_ref):
    h = _normmod(x_ref[0], g_ref[...], sh_ref[0], sc_ref[0]).astype(BF16)
    proj = jnp.dot(h, w_ref[...], preferred_element_type=F32)
    q_ref[0] = (proj[:, A_Q0:A_KV0] * (HEAD_DIM ** -0.5)).astype(BF16)
    ckv = proj[:, A_KV0:A_QI0]
    ckv = (ckv * lax.rsqrt(jnp.mean(ckv * ckv, axis=-1, keepdims=True) + RMS_EPS)) * kvg_ref[...]
    ckv_ref[0] = ckv.astype(BF16)
    qi_ref[0] = proj[:, A_QI0:A_KI0].astype(BF16)
    ki = proj[:, A_KI0:A_WI0]
    mu = jnp.mean(ki, axis=-1, keepdims=True)
    var = jnp.mean(jnp.square(ki - mu), axis=-1, keepdims=True)
    ki = ((ki - mu) * lax.rsqrt(var + RMS_EPS)) * ikg_ref[...] + ikb_ref[...]
    ki_ref[0] = ki.astype(BF16)
    wit = lax.dot_general(wwi_ref[...], h, _NT, preferred_element_type=F32)
    wit_ref[0] = wit[:IDX_HEADS] * (IDX_HEADS ** -0.5 * IDX_DIM ** -0.5)


def _a_inproj(x, g, sh, sc, w_main, w_wi_t, kv_g, ik_g, ik_b, tm=512):
    bsz, L, d = x.shape
    row = lambda b, i: (b, i, 0)
    per_b = lambda b, i: (b, 0, 0)
    const2 = lambda b, i: (0, 0)
    return pl.pallas_call(
        _a_inproj_kernel,
        grid=(bsz, L // tm),
        in_specs=[pl.BlockSpec((1, tm, d), row),
                  pl.BlockSpec((1, d), const2),
                  pl.BlockSpec((1, 1, d), per_b),
                  pl.BlockSpec((1, 1, d), per_b),
                  pl.BlockSpec((d, A_MAIN_PAD), const2),
                  pl.BlockSpec((16, d), const2),
                  pl.BlockSpec((1, KV_RANK), const2),
                  pl.BlockSpec((1, IDX_DIM), const2),
                  pl.BlockSpec((1, IDX_DIM), const2)],
        out_specs=[pl.BlockSpec((1, tm, HD), row),
                   pl.BlockSpec((1, tm, KV_RANK), row),
                   pl.BlockSpec((1, tm, IDX_HEADS * IDX_DIM), row),
                   pl.BlockSpec((1, tm, IDX_DIM), row),
                   pl.BlockSpec((1, IDX_HEADS, tm), lambda b, i: (b, 0, i))],
        out_shape=[jax.ShapeDtypeStruct((bsz, L, HD), BF16),
                   jax.ShapeDtypeStruct((bsz, L, KV_RANK), BF16),
                   jax.ShapeDtypeStruct((bsz, L, IDX_HEADS * IDX_DIM), BF16),
                   jax.ShapeDtypeStruct((bsz, L, IDX_DIM), BF16),
                   jax.ShapeDtypeStruct((bsz, IDX_HEADS, L), F32)],
        compiler_params=_cparams("parallel", "parallel"),
        name="a_inproj",
    )(x, g, sh, sc, w_main, w_wi_t, kv_g, ik_g, ik_b)


def _a_index_kernel(qi_ref, wit_ref, ki_ref, tri_ref, mask_ref, key_scr, half_scr, *, topk,
                    seq_len):
    j = pl.program_id(1)
    nchunk = seq_len // CHUNK
    qi = qi_ref[0]
    wit = wit_ref[0]
    t_glob = j * CHUNK + lax.broadcasted_iota(I32, (CHUNK, CHUNK), 1)
    s_loc = lax.broadcasted_iota(I32, (CHUNK, CHUNK), 0)

    def chunks(fn, init):
        def pair(i, carry):
            return fn(2 * i + 1, fn(2 * i, carry))
        carry = lax.fori_loop(0, lax.shift_right_logical(j + 1, 1), pair, init)
        return lax.cond((j & 1) == 0, lambda c: fn(j, c), lambda c: c, carry)

    def score_chunk(kc, carry):
        kik = ki_ref[0, pl.ds(pl.multiple_of(kc * CHUNK, CHUNK), CHUNK), :]
        acc = jnp.zeros((CHUNK, CHUNK), F32)
        for h in range(IDX_HEADS):
            r = lax.dot_general(kik, qi[:, h * IDX_DIM:(h + 1) * IDX_DIM], _NT,
                                preferred_element_type=F32)
            acc = acc + jnp.maximum(r, 0.0) * wit[h:h + 1, :]
        sc = jnp.where(kc * CHUNK + s_loc <= t_glob, acc, NEG)
        bits = lax.bitcast_convert_type(sc, I32)
        key = bits ^ ((bits >> 31) & 0x7FFFFFFF)
        key_scr[kc] = key
        half_scr[kc] = (key >> 16).astype(I16)
        return carry

    chunks(score_chunk, 0)

    n_beyond = seq_len - (j + 1) * CHUNK
    ge = lambda a, b: a >= b
    gt = lambda a, b: a > b

    def count(pred, thr):
        def body(kc, acc):
            m = jnp.where(pred(key_scr[kc], thr), jnp.int32(1), jnp.int32(0))
            return acc + m.reshape(CHUNK // 8, 8, CHUNK).sum(axis=0)
        acc = chunks(body, jnp.zeros((8, CHUNK), I32))
        cnt = acc.sum(axis=0, keepdims=True)
        return cnt + jnp.where(pred(KEY_NEG, thr), n_beyond, 0)

    def count_half(pred, thr16):
        t = jnp.broadcast_to(thr16, (HALF_ROWS, CHUNK)).astype(I16)
        def body(kc, acc):
            m = jnp.where(pred(half_scr[kc].reshape(CHUNK // HALF_ROWS, HALF_ROWS, CHUNK), t[None]),
                          jnp.int16(1), jnp.int16(0))
            for r in range(CHUNK // HALF_ROWS):
                acc = acc + m[r]
            return acc
        acc = chunks(body, jnp.zeros((HALF_ROWS, CHUNK), I16))
        return acc.astype(I32).sum(axis=0, keepdims=True)

    def beyond(cand):
        return jnp.where(KEY_NEG >= cand, n_beyond, 0)

    zero = jnp.zeros((1, CHUNK), I32)
    thr = jnp.where(count_half(ge, zero) + beyond(zero) >= topk, zero,
                    jnp.full((1, CHUNK), INT_MIN, I32))

    def bisect_high(i, thr):
        cand = thr + lax.shift_left(jnp.int32(1), 30 - i)
        cnt = count_half(ge, cand >> 16) + beyond(cand)
        return jnp.where(cnt >= topk, cand, thr)

    thr = lax.fori_loop(0, 15, bisect_high, thr)

    thr_hi = thr >> 16
    n_above = count_half(gt, thr_hi)

    def low_half(kc, carry):
        key = key_scr[kc]
        low = (key & 0xFFFF) - 0x8000
        half_scr[kc] = jnp.where((key >> 16) == thr_hi, low, -0x8000).astype(I16)
        return carry

    chunks(low_half, 0)

    def bisect_low(i, thr):
        cand = thr + lax.shift_left(jnp.int32(1), 15 - i)
        cnt = n_above + count_half(ge, (cand & 0xFFFF) - 0x8000) + beyond(cand)
        return jnp.where(cnt >= topk, cand, thr)

    thr = lax.fori_loop(0, 16, bisect_low, thr)

    need = (topk - count(gt, thr)).astype(F32)
    tri = tri_ref[...]

    def emit(kc, seen):
        key = key_scr[kc]
        eq = jnp.where(key == thr, 1.0, 0.0)
        rank = jnp.dot(tri, eq.astype(BF16), preferred_element_type=F32) + seen
        take = jnp.where(key > thr, 1.0, jnp.where(rank < need, eq, 0.0))
        causal = kc * CHUNK + s_loc <= t_glob
        m = jnp.where(causal, jnp.where(take > 0.5, 0.0, NEG), NEG)
        mask_ref[0, kc] = m.T
        return seen + jnp.sum(eq, axis=0, keepdims=True)

    chunks(emit, jnp.zeros((1, CHUNK), F32))

    def fill(kc, carry):
        mask_ref[0, kc] = jnp.full((CHUNK, CHUNK), NEG, F32)
        return carry

    lax.fori_loop(j + 1, nchunk, fill, 0)


def _a_index(qi, wit, ki, topk):
    bsz, L, _ = qi.shape
    nchunk = L // CHUNK
    tri = jnp.tril(jnp.ones((CHUNK, CHUNK), BF16), -1)
    return pl.pallas_call(
        functools.partial(_a_index_kernel, topk=topk, seq_len=L),
        grid=(bsz, nchunk),
        in_specs=[pl.BlockSpec((1, CHUNK, IDX_HEADS * IDX_DIM), lambda b, j: (b, j, 0)),
                  pl.BlockSpec((1, IDX_HEADS, CHUNK), lambda b, j: (b, 0, j)),
                  pl.BlockSpec((1, L, IDX_DIM), lambda b, j: (b, 0, 0)),
                  pl.BlockSpec((CHUNK, CHUNK), lambda b, j: (0, 0))],
        out_specs=pl.BlockSpec((1, nchunk, CHUNK, CHUNK), lambda b, j: (b, 0, j, 0)),
        out_shape=jax.ShapeDtypeStruct((bsz, nchunk, L, CHUNK), F32),
        scratch_shapes=[pltpu.VMEM((nchunk, CHUNK, CHUNK), I32),
                        pltpu.VMEM((nchunk, CHUNK, CHUNK), I16)],
        compiler_params=_cparams("parallel", "parallel"),
        name="a_index",
    )(qi, wit, ki, tri)


def _a_attn_kernel(q_ref, ckv_ref, mask_ref, wuk_ref, wuv_ref, bias_ref, o_ref,
                   qabs_scr, lg_scr, m_scr, l_scr, acc_scr, p_scr, *, hc):
    j = pl.program_id(1)
    tq = CHUNK
    q = q_ref[0]
    for i in range(hc):
        qa = lax.dot_general(q[:, i * HEAD_DIM:(i + 1) * HEAD_DIM], wuk_ref[i], _NT,
                             preferred_element_type=F32)
        qabs_scr[i * tq:(i + 1) * tq, :] = (qa * LOG2E).astype(BF16)
    qg = qabs_scr[...]

    n_far_pairs = lax.shift_right_arithmetic(j - 1, 1)
    odd = (j & 1) == 0

    def keys(kc, width):
        return ckv_ref[0, pl.ds(kc, width)].reshape(width * CHUNK, KV_RANK)

    def logits(kc, width):
        lg = lax.dot_general(qg, keys(kc, width), _NT, preferred_element_type=F32)
        mk = jnp.concatenate([mask_ref[0, kc + w] for w in range(width)], axis=1)
        return lg + jnp.concatenate([mk] * hc, axis=0)

    def near_bias(i, with_prev):
        d0, d1 = bias_ref[i, 0], bias_ref[i, 1]
        z = jnp.zeros_like(d0)
        top, bot = [d0, z], [d1, d0]
        if with_prev:
            top, bot = [z, d1] + top, [z, z] + bot
        return jnp.concatenate([jnp.concatenate(top, axis=1), jnp.concatenate(bot, axis=1)], axis=0)

    def put_logits(kc, width, lg):
        for w in range(width):
            lg_scr[kc + w] = lg[:, w * CHUNK:(w + 1) * CHUNK]
        m = m_scr[...]
        for c in range(width * CHUNK // LANES):
            m = jnp.maximum(m, lg[:, c * LANES:(c + 1) * LANES])
        m_scr[...] = m

    m_scr[...] = jnp.full(m_scr.shape, -jnp.inf, F32)

    @pl.when(j >= 1)
    def _():
        bias = jnp.concatenate([near_bias(i, True) for i in range(hc)], axis=0)
        put_logits(j - 1, 2, logits(j - 1, 2) + bias)

    def far_pair(i, carry):
        kc = j - 3 - 2 * i
        put_logits(kc, 2, logits(kc, 2))
        return carry

    lax.fori_loop(0, n_far_pairs, far_pair, 0)

    @pl.when(odd)
    def _():
        bias = jnp.concatenate([near_bias(i, False) for i in range(hc)], axis=0)
        put_logits(0, 1, logits(0, 1) + jnp.where(j == 0, bias, 0.0))

    m_scr[...] = jnp.broadcast_to(jnp.max(m_scr[...], axis=-1, keepdims=True), m_scr.shape)
    l_scr[...] = jnp.zeros_like(l_scr)
    acc_scr[...] = jnp.zeros_like(acc_scr)

    def pv(kc, width):
        for r in range(hc * tq // PV_ROWS):
            rows = slice(r * PV_ROWS, (r + 1) * PV_ROWS)
            m = m_scr[rows, :]
            l = l_scr[rows, :]
            for w in range(width):
                for c in range(CHUNK // LANES):
                    cols = slice(c * LANES, (c + 1) * LANES)
                    p = jnp.exp2(lg_scr[kc + w, rows, cols] - m)
                    l = l + p
                    p_scr[rows, w * CHUNK + c * LANES:w * CHUNK + (c + 1) * LANES] = p.astype(BF16)
            l_scr[rows, :] = l
        acc_scr[...] += jnp.dot(p_scr[:, :width * CHUNK], keys(kc, width),
                                preferred_element_type=F32)

    @pl.when(j >= 1)
    def _():
        pv(j - 1, 2)

    def far_pv(i, carry):
        pv(j - 3 - 2 * i, 2)
        return carry

    lax.fori_loop(0, n_far_pairs, far_pv, 0)

    @pl.when(odd)
    def _():
        pv(0, 1)

    o = acc_scr[...] / jnp.sum(l_scr[...], axis=-1, keepdims=True)
    outs = [jnp.dot(o[i * tq:(i + 1) * tq].astype(BF16), wuv_ref[i], preferred_element_type=F32)
            for i in range(hc)]
    o_ref[0] = jnp.concatenate(outs, axis=1).astype(BF16)


def _a_attn(q, ckv, mask, w_uk, w_uv, bias_nd, hc=8):
    bsz, L, _ = q.shape
    nchunk = L // CHUNK
    hg = N_HEADS // hc
    m = hc * CHUNK
    ckv4 = ckv.reshape(bsz, nchunk, CHUNK, KV_RANK)
    return pl.pallas_call(
        functools.partial(_a_attn_kernel, hc=hc),
        grid=(bsz, nchunk, hg),
        in_specs=[pl.BlockSpec((1, CHUNK, hc * HEAD_DIM), lambda b, j, g: (b, j, g)),
                  pl.BlockSpec((1, nchunk, CHUNK, KV_RANK), lambda b, j, g: (b, 0, 0, 0)),
                  pl.BlockSpec((1, nchunk, CHUNK, CHUNK), lambda b, j, g: (b, 0, j, 0)),
                  pl.BlockSpec((hc, KV_RANK, HEAD_DIM), lambda b, j, g: (g, 0, 0)),
                  pl.BlockSpec((hc, KV_RANK, HEAD_DIM), lambda b, j, g: (g, 0, 0)),
                  pl.BlockSpec((hc, 2, BIAS_BLK, BIAS_BLK), lambda b, j, g: (g, 0, 0, 0))],
        out_specs=pl.BlockSpec((1, CHUNK, hc * HEAD_DIM), lambda b, j, g: (b, j, g)),
        out_shape=jax.ShapeDtypeStruct((bsz, L, HD), BF16),
        scratch_shapes=[pltpu.VMEM((m, KV_RANK), BF16),
                        pltpu.VMEM((nchunk, m, CHUNK), F32),
                        pltpu.VMEM((m, LANES), F32),
                        pltpu.VMEM((m, LANES), F32),
                        pltpu.VMEM((m, KV_RANK), F32),
                        pltpu.VMEM((m, 2 * CHUNK), BF16)],
        compiler_params=_cparams("parallel", "parallel", "arbitrary"),
        name="a_attn",
    )(q, ckv4, mask, w_uk, w_uv, bias_nd)


def _bias_kernel(rb_ref, ids_ref, o_ref, *, n_tiles, far_bucket, shift_far):
    h = pl.program_id(0)
    far = rb_ref[far_bucket, h] if shift_far else 0.0
    for t in range(n_tiles):
        ids = ids_ref[t]
        out = jnp.zeros(ids.shape, F32)
        for b in range(N_BUCKETS):
            out = jnp.where(ids == b, rb_ref[b, h] - far, out)
        o_ref[0, t] = jnp.where(ids < 0, NEG, out * LOG2E)


def _t5_bucket(dist):
    max_exact = N_BUCKETS // 2
    d = jnp.maximum(dist, 0)
    large = max_exact + (jnp.log(jnp.maximum(d, 1).astype(F32) / max_exact)
                         / math.log(MAX_DISTANCE / max_exact)
                         * (N_BUCKETS - max_exact)).astype(I32)
    large = jnp.minimum(large, N_BUCKETS - 1)
    return jnp.where(d < max_exact, d, large)


def _bias_tiles(rel_bias, ids, shift_far):
    n_tiles, r, c = ids.shape
    return pl.pallas_call(
        functools.partial(_bias_kernel, n_tiles=n_tiles, far_bucket=N_BUCKETS - 1,
                          shift_far=shift_far),
        grid=(N_HEADS,),
        in_specs=[pl.BlockSpec(memory_space=pltpu.SMEM),
                  pl.BlockSpec((n_tiles, r, c), lambda h: (0, 0, 0))],
        out_specs=pl.BlockSpec((1, n_tiles, r, c), lambda h: (h, 0, 0, 0)),
        out_shape=jax.ShapeDtypeStruct((N_HEADS, n_tiles, r, c), F32),
        compiler_params=_cparams("parallel"),
        name="bias_tiles",
    )(rel_bias, ids)


def _b_inproj_kernel(x_ref, g_ref, sh_ref, sc_ref, w_ref, b_ref, q_ref, k_ref, v_ref):
    h = _normmod(x_ref[0], g_ref[...], sh_ref[0], sc_ref[0]).astype(BF16)
    proj = jnp.dot(h, w_ref[...], preferred_element_type=F32) + b_ref[...]
    q_ref[0] = (proj[:, :HD] * (HEAD_DIM ** -0.5 * LOG2E)).astype(BF16)
    k_ref[0] = proj[:, HD:HD + KV_HEADS * HEAD_DIM].astype(BF16)
    v_ref[0] = proj[:, HD + KV_HEADS * HEAD_DIM:].astype(BF16)


def _b_inproj(x, g, sh, sc, w, b, tm=512):
    bsz, L, d = x.shape
    kvw = KV_HEADS * HEAD_DIM
    row = lambda b_, i: (b_, i, 0)
    per_b = lambda b_, i: (b_, 0, 0)
    const2 = lambda b_, i: (0, 0)
    return pl.pallas_call(
        _b_inproj_kernel,
        grid=(bsz, L // tm),
        in_specs=[pl.BlockSpec((1, tm, d), row),
                  pl.BlockSpec((1, d), const2),
                  pl.BlockSpec((1, 1, d), per_b),
                  pl.BlockSpec((1, 1, d), per_b),
                  pl.BlockSpec((d, B_IN), const2),
                  pl.BlockSpec((1, B_IN), const2)],
        out_specs=[pl.BlockSpec((1, tm, HD), row),
                   pl.BlockSpec((1, tm, kvw), row),
                   pl.BlockSpec((1, tm, kvw), row)],
        out_shape=[jax.ShapeDtypeStruct((bsz, L, HD), BF16),
                   jax.ShapeDtypeStruct((bsz, L, kvw), BF16),
                   jax.ShapeDtypeStruct((bsz, L, kvw), BF16)],
        compiler_params=_cparams("parallel", "parallel"),
        name="b_inproj",
    )(x, g, sh, sc, w, b)


def _b_attn_kernel(sink_ref, q_ref, kp_ref, kc_ref, vp_ref, vc_ref, bias_ref, o_ref):
    n = pl.program_id(1)
    w = WINDOW
    g = N_HEADS // KV_HEADS
    q = q_ref[0]
    kb = jnp.concatenate([kp_ref[0], kc_ref[0]], axis=0)
    vb = jnp.concatenate([vp_ref[0], vc_ref[0]], axis=0)
    variant = jnp.where(n > 0, 0, 1)
    outs = []
    for h in range(N_HEADS):
        kh = h // g
        lg = lax.dot_general(q[:, h * HEAD_DIM:(h + 1) * HEAD_DIM],
                             kb[:, kh * HEAD_DIM:(kh + 1) * HEAD_DIM], _NT,
                             preferred_element_type=F32)
        lg = lg + bias_ref[h, variant]
        sink = sink_ref[0, h] * LOG2E
        m = jnp.maximum(jnp.max(lg, axis=-1, keepdims=True), sink)
        e = jnp.exp2(lg - m)
        denom = jnp.sum(e, axis=-1, keepdims=True) + jnp.exp2(sink - m)
        o = jnp.dot(e.astype(BF16), vb[:, kh * HEAD_DIM:(kh + 1) * HEAD_DIM],
                    preferred_element_type=F32)
        outs.append(o / denom)
    o_ref[0] = jnp.concatenate(outs, axis=1).astype(BF16)


def _b_attn(q, k, v, sinks, bias_b):
    bsz, L, _ = q.shape
    w = WINDOW
    kvw = KV_HEADS * HEAD_DIM
    cur = lambda b, n: (b, n, 0)
    prev = lambda b, n: (b, jnp.maximum(n - 1, 0), 0)
    return pl.pallas_call(
        _b_attn_kernel,
        grid=(bsz, L // w),
        in_specs=[pl.BlockSpec(memory_space=pltpu.SMEM),
                  pl.BlockSpec((1, w, HD), cur),
                  pl.BlockSpec((1, w, kvw), prev),
                  pl.BlockSpec((1, w, kvw), cur),
                  pl.BlockSpec((1, w, kvw), prev),
                  pl.BlockSpec((1, w, kvw), cur),
                  pl.BlockSpec((N_HEADS, 2, w, 2 * w), lambda b, n: (0, 0, 0, 0))],
        out_specs=pl.BlockSpec((1, w, HD), cur),
        out_shape=jax.ShapeDtypeStruct((bsz, L, HD), BF16),
        compiler_params=_cparams("parallel", "parallel"),
        name="b_attn",
    )(sinks.reshape(1, N_HEADS), q, k, k, v, v, bias_b)


FF_CHUNK = 256


def _out_ffn_kernel(o_ref, wo_ref, bo_ref, g1_ref, x_ref, g_ref, sh_ref, sc_ref, gate_ref,
                    w1_ref, w3_ref, w2_ref, fg_ref, y_ref, *, final_norm):
    mix = jnp.dot(o_ref[0], wo_ref[...], preferred_element_type=F32) + bo_ref[...]
    x = x_ref[0] + g1_ref[0] * mix
    h = _normmod(x, g_ref[...], sh_ref[0], sc_ref[0]).astype(BF16)
    acc = jnp.zeros(x.shape, F32)
    for c in range(D_FF // FF_CHUNK):
        cs = slice(c * FF_CHUNK, (c + 1) * FF_CHUNK)
        a1 = jnp.dot(h, w1_ref[:, cs], preferred_element_type=F32)
        a3 = jnp.dot(h, w3_ref[:, cs], preferred_element_type=F32)
        act = (a1 * jax.nn.sigmoid(a1)) * a3
        acc = acc + jnp.dot(act.astype(BF16), w2_ref[cs, :], preferred_element_type=F32)
    y = x + gate_ref[0] * acc
    if final_norm:
        y = (y * lax.rsqrt(jnp.mean(y * y, axis=-1, keepdims=True) + RMS_EPS)) * fg_ref[...]
    y_ref[0] = y


def _out_ffn(o, w_out, b_out, g1, x, g, sh, sc, gate, w1, w3, w2, final_g, final_norm, tm=512):
    bsz, L, d = x.shape
    row = lambda b_, i: (b_, i, 0)
    per_b = lambda b_, i: (b_, 0, 0)
    const2 = lambda b_, i: (0, 0)
    return pl.pallas_call(
        functools.partial(_out_ffn_kernel, final_norm=final_norm),
        grid=(bsz, L // tm),
        in_specs=[pl.BlockSpec((1, tm, HD), row),
                  pl.BlockSpec((HD, d), const2),
                  pl.BlockSpec((1, d), const2),
                  pl.BlockSpec((1, 1, d), per_b),
                  pl.BlockSpec((1, tm, d), row),
                  pl.BlockSpec((1, d), const2),
                  pl.BlockSpec((1, 1, d), per_b),
                  pl.BlockSpec((1, 1, d), per_b),
                  pl.BlockSpec((1, 1, d), per_b),
                  pl.BlockSpec((d, D_FF), const2),
                  pl.BlockSpec((d, D_FF), const2),
                  pl.BlockSpec((D_FF, d), const2),
                  pl.BlockSpec((1, d), const2)],
        out_specs=pl.BlockSpec((1, tm, d), row),
        out_shape=jax.ShapeDtypeStruct((bsz, L, d), F32),
        compiler_params=_cparams("parallel", "parallel"),
        name="out_ffn",
    )(o, w_out, b_out, g1, x, g, sh, sc, gate, w1, w3, w2, final_g)


def kernel(x, c, rel_bias, w_ada, b_ada, norm_mix_g, norm_ffn_g, a_w_in, a_kv_norm_g, a_w_uk,
           a_w_uv, a_idx_k_g, a_idx_k_b, a_w_out, b_w_in, b_b_in, b_sinks, b_w_out, b_b_out,
           ffn_w1, ffn_w3, ffn_w2, norm_final_g):
    bsz, L, d = x.shape
    depth = w_ada.shape[0]
    topk = min(INDEX_TOPK, L // 4)

    mod = _adaln(c, w_ada, b_ada)

    r = jnp.arange(BIAS_BLK)
    ids_a = jnp.stack([_t5_bucket(r[:, None] - r[None, :]),
                       _t5_bucket(r[:, None] - r[None, :] + BIAS_BLK)])
    bias_a = _bias_tiles(rel_bias, ids_a, shift_far=True)
    dist_b = jnp.arange(WINDOW)[:, None] + WINDOW - jnp.arange(2 * WINDOW)[None, :]
    in_window = (dist_b >= 0) & (dist_b < WINDOW)
    ids_b = jnp.where(in_window, _t5_bucket(dist_b), -1)
    ids_b0 = jnp.where(jnp.arange(2 * WINDOW)[None, :] >= WINDOW, ids_b, -1)
    bias_b = _bias_tiles(rel_bias, jnp.stack([ids_b, ids_b0]), shift_far=False)

    zero_bias = jnp.zeros((1, d), F32)
    for i in range(depth):
        sh1, sc1, g1, sh2, sc2, g2 = [m.reshape(bsz, 1, d) for m in jnp.split(mod[i], 6, axis=-1)]
        jm = i // 2
        if i % 2 == 0:
            w_in = a_w_in[jm]
            w_main = jnp.pad(w_in[:, :A_MAIN], ((0, 0), (0, A_MAIN_PAD - A_MAIN))).astype(BF16)
            w_wi_t = jnp.pad(w_in[:, A_WI0:A_WI0 + IDX_HEADS].T,
                             ((0, 16 - IDX_HEADS), (0, 0))).astype(BF16)
            q, ckv, qi, ki, wit = _a_inproj(
                x, norm_mix_g[i][None], sh1, sc1, w_main, w_wi_t, a_kv_norm_g[jm][None],
                a_idx_k_g[jm][None], a_idx_k_b[jm][None])
            mask = _a_index(qi, wit, ki, topk)
            o = _a_attn(q, ckv, mask, a_w_uk[jm].astype(BF16), a_w_uv[jm].astype(BF16), bias_a)
            w_out, b_out = a_w_out[jm], zero_bias
        else:
            q, k, v = _b_inproj(x, norm_mix_g[i][None], sh1, sc1, b_w_in[jm].astype(BF16),
                                b_b_in[jm][None])
            o = _b_attn(q, k, v, b_sinks[jm], bias_b)
            w_out, b_out = b_w_out[jm], b_b_out[jm][None]
        x = _out_ffn(o, w_out.astype(BF16), b_out, g1, x, norm_ffn_g[i][None], sh2, sc2, g2,
                     ffn_w1[i].astype(BF16), ffn_w3[i].astype(BF16), ffn_w2[i].astype(BF16),
                     norm_final_g[None], final_norm=(i == depth - 1))
    return x
```

```python
import functools
import math

import numpy as np
import jax
import jax.numpy as jnp
from jax import lax
from jax.experimental import pallas as pl
from jax.experimental.pallas import tpu as pltpu

D_MODEL = 1024
N_HEADS = 16
HEAD_DIM = 64
KV_RANK = 256
IDX_HEADS = 8
IDX_DIM = 64
INDEX_TOPK = 256
KV_HEADS = 2
WINDOW = 128
N_BUCKETS = 32
MAX_DISTANCE = 128
D_FF = 2816
RMS_EPS = 1e-6
NEG = -1e30

HD = N_HEADS * HEAD_DIM
A_Q0, A_KV0, A_QI0, A_KI0, A_WI0 = 0, HD, HD + KV_RANK, HD + KV_RANK + IDX_HEADS * IDX_DIM, \
    HD + KV_RANK + IDX_HEADS * IDX_DIM + IDX_DIM
A_MAIN = A_WI0
A_MAIN_PAD = 1920
B_IN = (N_HEADS + 2 * KV_HEADS) * HEAD_DIM

CHUNK = 256
PV_ROWS = 64
HALF_ROWS = 16
LANES = 128
BIAS_BLK = 128
VMEM_LIMIT = 56 * 1024 * 1024

F32 = jnp.float32
BF16 = jnp.bfloat16
I32 = jnp.int32
I16 = jnp.int16

_NT = (((1,), (1,)), ((), ()))


def _cparams(*sem):
    return pltpu.CompilerParams(dimension_semantics=sem, vmem_limit_bytes=VMEM_LIMIT)


def _f32_key(v):
    b = int(np.array(v, np.float32).view(np.int32))
    return b ^ ((b >> 31) & 0x7FFFFFFF)


LOG2E = math.log2(math.e)
KEY_NEG = _f32_key(NEG)
INT_MIN = -(2 ** 31)


def _adaln_kernel(c_ref, w_ref, b_ref, o_ref):
    c = c_ref[...]
    cs = c * jax.nn.sigmoid(c)
    o_ref[0] = jnp.dot(cs, w_ref[0], preferred_element_type=F32,
                       precision=lax.Precision.HIGHEST) + b_ref[0]


def _adaln(c, w_ada, b_ada):
    depth, d, n = w_ada.shape
    bsz = c.shape[0]
    tn = 1536
    return pl.pallas_call(
        _adaln_kernel,
        grid=(depth, n // tn),
        in_specs=[pl.BlockSpec((bsz, d), lambda i, j: (0, 0)),
                  pl.BlockSpec((1, d, tn), lambda i, j: (i, 0, j)),
                  pl.BlockSpec((1, 1, tn), lambda i, j: (i, 0, j))],
        out_specs=pl.BlockSpec((1, bsz, tn), lambda i, j: (i, 0, j)),
        out_shape=jax.ShapeDtypeStruct((depth, bsz, n), F32),
        compiler_params=_cparams("parallel", "parallel"),
        name="adaln",
    )(c, w_ada, b_ada.reshape(depth, 1, n))


def _normmod(x, g, sh, sc):
    ms = jnp.mean(x * x, axis=-1, keepdims=True)
    y = (x * lax.rsqrt(ms + RMS_EPS)) * g
    return y * (1.0 + sc) + sh


def _a_inproj_kernel(x_ref, g_ref, sh_ref, sc_ref, w_ref, wwi_ref, kvg_ref, ikg_ref, ikb_ref,
                     q_ref, ckv_ref, qi_ref, ki_ref, wit_ref):
    h = _normmod(x_ref[0], g_ref[...], sh_ref[0], sc_ref[0]).astype(BF16)
    proj = jnp.dot(h, w_ref[...], preferred_element_type=F32)
    q_ref[0] = (proj[:, A_Q0:A_KV0] * (HEAD_DIM ** -0.5)).astype(BF16)
    ckv = proj[:, A_KV0:A_QI0]
    ckv = (ckv * lax.rsqrt(jnp.mean(ckv * ckv, axis=-1, keepdims=True) + RMS_EPS)) * kvg_ref[...]
    ckv_ref[0] = ckv.astype(BF16)
    qi_ref[0] = proj[:, A_QI0:A_KI0].astype(BF16)
    ki = proj[:, A_KI0:A_WI0]
    mu = jnp.mean(ki, axis=-1, keepdims=True)
    var = jnp.mean(jnp.square(ki - mu), axis=-1, keepdims=True)
    ki = ((ki - mu) * lax.rsqrt(var + RMS_EPS)) * ikg_ref[...] + ikb_ref[...]
    ki_ref[0] = ki.astype(BF16)
    wit = lax.dot_general(wwi_ref[...], h, _NT, preferred_element_type=F32)
    wit_ref[0] = wit[:IDX_HEADS] * (IDX_HEADS ** -0.5 * IDX_DIM ** -0.5)


def _a_inproj(x, g, sh, sc, w_main, w_wi_t, kv_g, ik_g, ik_b, tm=512):
    bsz, L, d = x.shape
    row = lambda b, i: (b, i, 0)
    per_b = lambda b, i: (b, 0, 0)
    const2 = lambda b, i: (0, 0)
    return pl.pallas_call(
        _a_inproj_kernel,
        grid=(bsz, L // tm),
        in_specs=[pl.BlockSpec((1, tm, d), row),
                  pl.BlockSpec((1, d), const2),
                  pl.BlockSpec((1, 1, d), per_b),
                  pl.BlockSpec((1, 1, d), per_b),
                  pl.BlockSpec((d, A_MAIN_PAD), const2),
                  pl.BlockSpec((16, d), const2),
                  pl.BlockSpec((1, KV_RANK), const2),
                  pl.BlockSpec((1, IDX_DIM), const2),
                  pl.BlockSpec((1, IDX_DIM), const2)],
        out_specs=[pl.BlockSpec((1, tm, HD), row),
                   pl.BlockSpec((1, tm, KV_RANK), row),
                   pl.BlockSpec((1, tm, IDX_HEADS * IDX_DIM), row),
                   pl.BlockSpec((1, tm, IDX_DIM), row),
                   pl.BlockSpec((1, IDX_HEADS, tm), lambda b, i: (b, 0, i))],
        out_shape=[jax.ShapeDtypeStruct((bsz, L, HD), BF16),
                   jax.ShapeDtypeStruct((bsz, L, KV_RANK), BF16),
                   jax.ShapeDtypeStruct((bsz, L, IDX_HEADS * IDX_DIM), BF16),
                   jax.ShapeDtypeStruct((bsz, L, IDX_DIM), BF16),
                   jax.ShapeDtypeStruct((bsz, IDX_HEADS, L), F32)],
        compiler_params=_cparams("parallel", "parallel"),
        name="a_inproj",
    )(x, g, sh, sc, w_main, w_wi_t, kv_g, ik_g, ik_b)


def _a_index_kernel(qi_ref, wit_ref, ki_ref, tri_ref, mask_ref, key_scr, half_scr, *, topk,
                    seq_len):
    j = pl.program_id(1)
    nchunk = seq_len // CHUNK
    qi = qi_ref[0]
    wit = wit_ref[0]
    t_glob = j * CHUNK + lax.broadcasted_iota(I32, (CHUNK, CHUNK), 1)
    s_loc = lax.broadcasted_iota(I32, (CHUNK, CHUNK), 0)

    def chunks(fn, init):
        def pair(i, carry):
            return fn(2 * i + 1, fn(2 * i, carry))
        carry = lax.fori_loop(0, lax.shift_right_logical(j + 1, 1), pair, init)
        return lax.cond((j & 1) == 0, lambda c: fn(j, c), lambda c: c, carry)

    def score_chunk(kc, carry):
        kik = ki_ref[0, pl.ds(pl.multiple_of(kc * CHUNK, CHUNK), CHUNK), :]
        acc = jnp.zeros((CHUNK, CHUNK), F32)
        for h in range(IDX_HEADS):
            r = lax.dot_general(kik, qi[:, h * IDX_DIM:(h + 1) * IDX_DIM], _NT,
                                preferred_element_type=F32)
            acc = acc + jnp.maximum(r, 0.0) * wit[h:h + 1, :]
        sc = jnp.where(kc * CHUNK + s_loc <= t_glob, acc, NEG)
        bits = lax.bitcast_convert_type(sc, I32)
        key = bits ^ ((bits >> 31) & 0x7FFFFFFF)
        key_scr[kc] = key
        half_scr[kc] = (key >> 16).astype(I16)
        return carry

    chunks(score_chunk, 0)

    n_beyond = seq_len - (j + 1) * CHUNK
    ge = lambda a, b: a >= b
    gt = lambda a, b: a > b

    def count(pred, thr):
        def body(kc, acc):
            m = jnp.where(pred(key_scr[kc], thr), jnp.int32(1), jnp.int32(0))
            return acc + m.reshape(CHUNK // 8, 8, CHUNK).sum(axis=0)
        acc = chunks(body, jnp.zeros((8, CHUNK), I32))
        cnt = acc.sum(axis=0, keepdims=True)
        return cnt + jnp.where(pred(KEY_NEG, thr), n_beyond, 0)

    def count_half(pred, thr16):
        t = jnp.broadcast_to(thr16, (HALF_ROWS, CHUNK)).astype(I16)
        def body(kc, acc):
            m = jnp.where(pred(half_scr[kc].reshape(CHUNK // HALF_ROWS, HALF_ROWS, CHUNK), t[None]),
                          jnp.int16(1), jnp.int16(0))
            for r in range(CHUNK // HALF_ROWS):
                acc = acc + m[r]
            return acc
        acc = chunks(body, jnp.zeros((HALF_ROWS, CHUNK), I16))
        return acc.astype(I32).sum(axis=0, keepdims=True)

    def beyond(cand):
        return jnp.where(KEY_NEG >= cand, n_beyond, 0)

    zero = jnp.zeros((1, CHUNK), I32)
    thr = jnp.where(count_half(ge, zero) + beyond(zero) >= topk, zero,
                    jnp.full((1, CHUNK), INT_MIN, I32))

    def bisect_high(i, thr):
        cand = thr + lax.shift_left(jnp.int32(1), 30 - i)
        cnt = count_half(ge, cand >> 16) + beyond(cand)
        return jnp.where(cnt >= topk, cand, thr)

    thr = lax.fori_loop(0, 15, bisect_high, thr)

    thr_hi = thr >> 16
    n_above = count_half(gt, thr_hi)

    def low_half(kc, carry):
        key = key_scr[kc]
        low = (key & 0xFFFF) - 0x8000
        half_scr[kc] = jnp.where((key >> 16) == thr_hi, low, -0x8000).astype(I16)
        return carry

    chunks(low_half, 0)

    def bisect_low(i, thr):
        cand = thr + lax.shift_left(jnp.int32(1), 15 - i)
        cnt = n_above + count_half(ge, (cand & 0xFFFF) - 0x8000) + beyond(cand)
        return jnp.where(cnt >= topk, cand, thr)

    thr = lax.fori_loop(0, 16, bisect_low, thr)

    need = (topk - count(gt, thr)).astype(F32)
    tri = tri_ref[...]

    def emit(kc, seen):
        key = key_scr[kc]
        eq = jnp.where(key == thr, 1.0, 0.0)
        rank = jnp.dot(tri, eq.astype(BF16), preferred_element_type=F32) + seen
        take = jnp.where(key > thr, 1.0, jnp.where(rank < need, eq, 0.0))
        causal = kc * CHUNK + s_loc <= t_glob
        m = jnp.where(causal, jnp.where(take > 0.5, 0.0, NEG), NEG)
        mask_ref[0, kc] = m.T
        return seen + jnp.sum(eq, axis=0, keepdims=True)

    chunks(emit, jnp.zeros((1, CHUNK), F32))

    def fill(kc, carry):
        mask_ref[0, kc] = jnp.full((CHUNK, CHUNK), NEG, F32)
        return carry

    lax.fori_loop(j + 1, nchunk, fill, 0)


def _a_index(qi, wit, ki, topk):
    bsz, L, _ = qi.shape
    nchunk = L // CHUNK
    tri = jnp.tril(jnp.ones((CHUNK, CHUNK), BF16), -1)
    return pl.pallas_call(
        functools.partial(_a_index_kernel, topk=topk, seq_len=L),
        grid=(bsz, nchunk),
        in_specs=[pl.BlockSpec((1, CHUNK, IDX_HEADS * IDX_DIM), lambda b, j: (b, j, 0)),
                  pl.BlockSpec((1, IDX_HEADS, CHUNK), lambda b, j: (b, 0, j)),
                  pl.BlockSpec((1, L, IDX_DIM), lambda b, j: (b, 0, 0)),
                  pl.BlockSpec((CHUNK, CHUNK), lambda b, j: (0, 0))],
        out_specs=pl.BlockSpec((1, nchunk, CHUNK, CHUNK), lambda b, j: (b, 0, j, 0)),
        out_shape=jax.ShapeDtypeStruct((bsz, nchunk, L, CHUNK), F32),
        scratch_shapes=[pltpu.VMEM((nchunk, CHUNK, CHUNK), I32),
                        pltpu.VMEM((nchunk, CHUNK, CHUNK), I16)],
        compiler_params=_cparams("parallel", "parallel"),
        name="a_index",
    )(qi, wit, ki, tri)


def _a_attn_kernel(q_ref, ckv_ref, mask_ref, wuk_ref, wuv_ref, bias_ref, o_ref,
                   qabs_scr, lg_scr, m_scr, l_scr, acc_scr, p_scr, *, hc):
    j = pl.program_id(1)
    tq = CHUNK
    q = q_ref[0]
    for i in range(hc):
        qa = lax.dot_general(q[:, i * HEAD_DIM:(i + 1) * HEAD_DIM], wuk_ref[i], _NT,
                             preferred_element_type=F32)
        qabs_scr[i * tq:(i + 1) * tq, :] = (qa * LOG2E).astype(BF16)
    qg = qabs_scr[...]

    n_far_pairs = lax.shift_right_arithmetic(j - 1, 1)
    odd = (j & 1) == 0

    def keys(kc, width):
        return ckv_ref[0, pl.ds(kc, width)].reshape(width * CHUNK, KV_RANK)

    def logits(kc, width):
        lg = lax.dot_general(qg, keys(kc, width), _NT, preferred_element_type=F32)
        mk = jnp.concatenate([mask_ref[0, kc + w] for w in range(width)], axis=1)
        return lg + jnp.concatenate([mk] * hc, axis=0)

    def near_bias(i, with_prev):
        d0, d1 = bias_ref[i, 0], bias_ref[i, 1]
        z = jnp.zeros_like(d0)
        top, bot = [d0, z], [d1, d0]
        if with_prev:
            top, bot = [z, d1] + top, [z, z] + bot
        return jnp.concatenate([jnp.concatenate(top, axis=1), jnp.concatenate(bot, axis=1)], axis=0)

    def put_logits(kc, width, lg):
        for w in range(width):
            lg_scr[kc + w] = lg[:, w * CHUNK:(w + 1) * CHUNK]
        m = m_scr[...]
        for c in range(width * CHUNK // LANES):
            m = jnp.maximum(m, lg[:, c * LANES:(c + 1) * LANES])
        m_scr[...] = m

    m_scr[...] = jnp.full(m_scr.shape, -jnp.inf, F32)

    @pl.when(j >= 1)
    def _():
        bias = jnp.concatenate([near_bias(i, True) for i in range(hc)], axis=0)
        put_logits(j - 1, 2, logits(j - 1, 2) + bias)

    def far_pair(i, carry):
        kc = j - 3 - 2 * i
        put_logits(kc, 2, logits(kc, 2))
        return carry

    lax.fori_loop(0, n_far_pairs, far_pair, 0)

    @pl.when(odd)
    def _():
        bias = jnp.concatenate([near_bias(i, False) for i in range(hc)], axis=0)
        put_logits(0, 1, logits(0, 1) + jnp.where(j == 0, bias, 0.0))

    l_scr[...] = jnp.zeros_like(l_scr)
    acc_scr[...] = jnp.zeros_like(acc_scr)

    def pv(kc, width, reduce_max):
        for r in range(hc * tq // PV_ROWS):
            rows = slice(r * PV_ROWS, (r + 1) * PV_ROWS)
            m = m_scr[rows, :]
            if reduce_max:
                m = jnp.broadcast_to(jnp.max(m, axis=-1, keepdims=True), m.shape)
                m_scr[rows, :] = m
            l = l_scr[rows, :]
            for w in range(width):
                for c in range(CHUNK // LANES):
                    cols = slice(c * LANES, (c + 1) * LANES)
                    p = jnp.exp2(lg_scr[kc + w, rows, cols] - m)
                    l = l + p
                    p_scr[rows, w * CHUNK + c * LANES:w * CHUNK + (c + 1) * LANES] = p.astype(BF16)
            l_scr[rows, :] = l
        acc_scr[...] += jnp.dot(p_scr[:, :width * CHUNK], keys(kc, width),
                                preferred_element_type=F32)

    @pl.when(j >= 1)
    def _():
        pv(j - 1, 2, True)

    def far_pv(i, carry):
        pv(j - 3 - 2 * i, 2, False)
        return carry

    lax.fori_loop(0, n_far_pairs, far_pv, 0)

    @pl.when(odd)
    def _():
        pv(0, 1, True)

    o = acc_scr[...] / jnp.sum(l_scr[...], axis=-1, keepdims=True)
    outs = [jnp.dot(o[i * tq:(i + 1) * tq].astype(BF16), wuv_ref[i], preferred_element_type=F32)
            for i in range(hc)]
    o_ref[0] = jnp.concatenate(outs, axis=1).astype(BF16)


def _a_attn(q, ckv, mask, w_uk, w_uv, bias_nd, hc=8):
    bsz, L, _ = q.shape
    nchunk = L // CHUNK
    hg = N_HEADS // hc
    m = hc * CHUNK
    ckv4 = ckv.reshape(bsz, nchunk, CHUNK, KV_RANK)
    return pl.pallas_call(
        functools.partial(_a_attn_kernel, hc=hc),
        grid=(bsz, nchunk, hg),
        in_specs=[pl.BlockSpec((1, CHUNK, hc * HEAD_DIM), lambda b, j, g: (b, j, g)),
                  pl.BlockSpec((1, nchunk, CHUNK, KV_RANK), lambda b, j, g: (b, 0, 0, 0)),
                  pl.BlockSpec((1, nchunk, CHUNK, CHUNK), lambda b, j, g: (b, 0, j, 0)),
                  pl.BlockSpec((hc, KV_RANK, HEAD_DIM), lambda b, j, g: (g, 0, 0)),
                  pl.BlockSpec((hc, KV_RANK, HEAD_DIM), lambda b, j, g: (g, 0, 0)),
                  pl.BlockSpec((hc, 2, BIAS_BLK, BIAS_BLK), lambda b, j, g: (g, 0, 0, 0))],
        out_specs=pl.BlockSpec((1, CHUNK, hc * HEAD_DIM), lambda b, j, g: (b, j, g)),
        out_shape=jax.ShapeDtypeStruct((bsz, L, HD), BF16),
        scratch_shapes=[pltpu.VMEM((m, KV_RANK), BF16),
                        pltpu.VMEM((nchunk, m, CHUNK), F32),
                        pltpu.VMEM((m, LANES), F32),
                        pltpu.VMEM((m, LANES), F32),
                        pltpu.VMEM((m, KV_RANK), F32),
                        pltpu.VMEM((m, 2 * CHUNK), BF16)],
        compiler_params=_cparams("parallel", "parallel", "arbitrary"),
        name="a_attn",
    )(q, ckv4, mask, w_uk, w_uv, bias_nd)


def _bias_kernel(rb_ref, ids_ref, o_ref, *, n_tiles, far_bucket, shift_far):
    h = pl.program_id(0)
    far = rb_ref[far_bucket, h] if shift_far else 0.0
    for t in range(n_tiles):
        ids = ids_ref[t]
        out = jnp.zeros(ids.shape, F32)
        for b in range(N_BUCKETS):
            out = jnp.where(ids == b, rb_ref[b, h] - far, out)
        o_ref[0, t] = jnp.where(ids < 0, NEG, out * LOG2E)


def _t5_bucket(dist):
    max_exact = N_BUCKETS // 2
    d = jnp.maximum(dist, 0)
    large = max_exact + (jnp.log(jnp.maximum(d, 1).astype(F32) / max_exact)
                         / math.log(MAX_DISTANCE / max_exact)
                         * (N_BUCKETS - max_exact)).astype(I32)
    large = jnp.minimum(large, N_BUCKETS - 1)
    return jnp.where(d < max_exact, d, large)


def _bias_tiles(rel_bias, ids, shift_far):
    n_tiles, r, c = ids.shape
    return pl.pallas_call(
        functools.partial(_bias_kernel, n_tiles=n_tiles, far_bucket=N_BUCKETS - 1,
                          shift_far=shift_far),
        grid=(N_HEADS,),
        in_specs=[pl.BlockSpec(memory_space=pltpu.SMEM),
                  pl.BlockSpec((n_tiles, r, c), lambda h: (0, 0, 0))],
        out_specs=pl.BlockSpec((1, n_tiles, r, c), lambda h: (h, 0, 0, 0)),
        out_shape=jax.ShapeDtypeStruct((N_HEADS, n_tiles, r, c), F32),
        compiler_params=_cparams("parallel"),
        name="bias_tiles",
    )(rel_bias, ids)


def _b_inproj_kernel(x_ref, g_ref, sh_ref, sc_ref, w_ref, b_ref, q_ref, k_ref, v_ref):
    h = _normmod(x_ref[0], g_ref[...], sh_ref[0], sc_ref[0]).astype(BF16)
    proj = jnp.dot(h, w_ref[...], preferred_element_type=F32) + b_ref[...]
    q_ref[0] = (proj[:, :HD] * (HEAD_DIM ** -0.5 * LOG2E)).astype(BF16)
    k_ref[0] = proj[:, HD:HD + KV_HEADS * HEAD_DIM].astype(BF16)
    v_ref[0] = proj[:, HD + KV_HEADS * HEAD_DIM:].astype(BF16)


def _b_inproj(x, g, sh, sc, w, b, tm=512):
    bsz, L, d = x.shape
    kvw = KV_HEADS * HEAD_DIM
    row = lambda b_, i: (b_, i, 0)
    per_b = lambda b_, i: (b_, 0, 0)
    const2 = lambda b_, i: (0, 0)
    return pl.pallas_call(
        _b_inproj_kernel,
        grid=(bsz, L // tm),
        in_specs=[pl.BlockSpec((1, tm, d), row),
                  pl.BlockSpec((1, d), const2),
                  pl.BlockSpec((1, 1, d), per_b),
                  pl.BlockSpec((1, 1, d), per_b),
                  pl.BlockSpec((d, B_IN), const2),
                  pl.BlockSpec((1, B_IN), const2)],
        out_specs=[pl.BlockSpec((1, tm, HD), row),
                   pl.BlockSpec((1, tm, kvw), row),
                   pl.BlockSpec((1, tm, kvw), row)],
        out_shape=[jax.ShapeDtypeStruct((bsz, L, HD), BF16),
                   jax.ShapeDtypeStruct((bsz, L, kvw), BF16),
                   jax.ShapeDtypeStruct((bsz, L, kvw), BF16)],
        compiler_params=_cparams("parallel", "parallel"),
        name="b_inproj",
    )(x, g, sh, sc, w, b)


B_BLOCKS = 4


def _b_attn_kernel(sink_ref, q_ref, kp_ref, kc_ref, vp_ref, vc_ref, bias_ref, o_ref):
    n = pl.program_id(1)
    w = WINDOW
    pairs = N_HEADS // KV_HEADS // 2
    kall = jnp.concatenate([kp_ref[0], kc_ref[0]], axis=0).astype(F32)
    vall = jnp.concatenate([vp_ref[0], vc_ref[0]], axis=0).astype(F32)
    low = lax.broadcasted_iota(I32, kall.shape, 1) < HEAD_DIM

    def padded(x, g):
        swapped = pltpu.roll(x, HEAD_DIM, axis=1)
        on_low, on_high = (x, swapped) if g == 0 else (swapped, x)
        return (jnp.where(low, on_low, 0.0).astype(BF16), jnp.where(low, 0.0, on_high).astype(BF16))

    kpad = [padded(kall, g) for g in range(KV_HEADS)]
    vpad = [padded(vall, g) for g in range(KV_HEADS)]
    for blk in range(B_BLOCKS):
        variant = jnp.where(n > 0, 0, 1) if blk == 0 else 0
        keys = slice(blk * w, (blk + 2) * w)
        q = q_ref[0, blk * w:(blk + 1) * w, :]
        outs = []
        for g in range(KV_HEADS):
            blocks = [q[:, (pairs * g + p) * 2 * HEAD_DIM:(pairs * g + p + 1) * 2 * HEAD_DIM]
                      for p in range(pairs)]
            lg = lax.dot_general(jnp.concatenate(blocks, axis=0),
                                 jnp.concatenate([kpad[g][0][keys], kpad[g][1][keys]], axis=0),
                                 _NT, preferred_element_type=F32)
            lg = lg + bias_ref[variant, g]
            probs = [[], []]
            for p in range(pairs):
                for par in range(2):
                    h = 2 * pairs * g + 2 * p + par
                    t = lg[p * w:(p + 1) * w, par * 2 * w:(par + 1) * 2 * w]
                    sink = sink_ref[0, h] * LOG2E
                    m = jnp.maximum(jnp.max(t, axis=-1, keepdims=True), sink)
                    e = jnp.exp2(t - m)
                    denom = jnp.sum(e, axis=-1, keepdims=True) + jnp.exp2(sink - m)
                    probs[par].append((e * (1.0 / denom)).astype(BF16))
            og = (jnp.dot(jnp.concatenate(probs[0], axis=0), vpad[g][0][keys],
                          preferred_element_type=F32)
                  + jnp.dot(jnp.concatenate(probs[1], axis=0), vpad[g][1][keys],
                            preferred_element_type=F32))
            outs += [og[p * w:(p + 1) * w] for p in range(pairs)]
        o_ref[0, blk * w:(blk + 1) * w, :] = jnp.concatenate(outs, axis=1).astype(BF16)


def _b_attn(q, k, v, sinks, bias_b):
    bsz, L, _ = q.shape
    w = WINDOW
    wide = B_BLOCKS * w
    kvw = KV_HEADS * HEAD_DIM
    cur = lambda b, n: (b, n, 0)
    prev = lambda b, n: (b, jnp.maximum(n * B_BLOCKS - 1, 0), 0)
    return pl.pallas_call(
        _b_attn_kernel,
        grid=(bsz, L // wide),
        in_specs=[pl.BlockSpec(memory_space=pltpu.SMEM),
                  pl.BlockSpec((1, wide, HD), cur),
                  pl.BlockSpec((1, w, kvw), prev),
                  pl.BlockSpec((1, wide, kvw), cur),
                  pl.BlockSpec((1, w, kvw), prev),
                  pl.BlockSpec((1, wide, kvw), cur),
                  pl.BlockSpec(bias_b.shape, lambda b, n: (0, 0, 0, 0))],
        out_specs=pl.BlockSpec((1, wide, HD), cur),
        out_shape=jax.ShapeDtypeStruct((bsz, L, HD), BF16),
        compiler_params=_cparams("parallel", "parallel"),
        name="b_attn",
    )(sinks.reshape(1, N_HEADS), q, k, k, v, v, bias_b)


FF_CHUNK = 256


def _out_ffn_kernel(o_ref, wo_ref, bo_ref, g1_ref, x_ref, g_ref, sh_ref, sc_ref, gate_ref,
                    w1_ref, w3_ref, w2_ref, fg_ref, y_ref, *, final_norm):
    mix = jnp.dot(o_ref[0], wo_ref[...], preferred_element_type=F32) + bo_ref[...]
    x = x_ref[0] + g1_ref[0] * mix
    h = _normmod(x, g_ref[...], sh_ref[0], sc_ref[0]).astype(BF16)
    acc = jnp.zeros(x.shape, F32)
    for c in range(D_FF // FF_CHUNK):
        cs = slice(c * FF_CHUNK, (c + 1) * FF_CHUNK)
        a1 = jnp.dot(h, w1_ref[:, cs], preferred_element_type=F32)
        a3 = jnp.dot(h, w3_ref[:, cs], preferred_element_type=F32)
        act = (a1 * jax.nn.sigmoid(a1)) * a3
        acc = acc + jnp.dot(act.astype(BF16), w2_ref[cs, :], preferred_element_type=F32)
    y = x + gate_ref[0] * acc
    if final_norm:
        y = (y * lax.rsqrt(jnp.mean(y * y, axis=-1, keepdims=True) + RMS_EPS)) * fg_ref[...]
    y_ref[0] = y


def _out_ffn(o, w_out, b_out, g1, x, g, sh, sc, gate, w1, w3, w2, final_g, final_norm, tm=512):
    bsz, L, d = x.shape
    row = lambda b_, i: (b_, i, 0)
    per_b = lambda b_, i: (b_, 0, 0)
    const2 = lambda b_, i: (0, 0)
    return pl.pallas_call(
        functools.partial(_out_ffn_kernel, final_norm=final_norm),
        grid=(bsz, L // tm),
        in_specs=[pl.BlockSpec((1, tm, HD), row),
                  pl.BlockSpec((HD, d), const2),
                  pl.BlockSpec((1, d), const2),
                  pl.BlockSpec((1, 1, d), per_b),
                  pl.BlockSpec((1, tm, d), row),
                  pl.BlockSpec((1, d), const2),
                  pl.BlockSpec((1, 1, d), per_b),
                  pl.BlockSpec((1, 1, d), per_b),
                  pl.BlockSpec((1, 1, d), per_b),
                  pl.BlockSpec((d, D_FF), const2),
                  pl.BlockSpec((d, D_FF), const2),
                  pl.BlockSpec((D_FF, d), const2),
                  pl.BlockSpec((1, d), const2)],
        out_specs=pl.BlockSpec((1, tm, d), row),
        out_shape=jax.ShapeDtypeStruct((bsz, L, d), F32),
        compiler_params=_cparams("parallel", "parallel"),
        name="out_ffn",
    )(o, w_out, b_out, g1, x, g, sh, sc, gate, w1, w3, w2, final_g)


def kernel(x, c, rel_bias, w_ada, b_ada, norm_mix_g, norm_ffn_g, a_w_in, a_kv_norm_g, a_w_uk,
           a_w_uv, a_idx_k_g, a_idx_k_b, a_w_out, b_w_in, b_b_in, b_sinks, b_w_out, b_b_out,
           ffn_w1, ffn_w3, ffn_w2, norm_final_g):
    bsz, L, d = x.shape
    depth = w_ada.shape[0]
    topk = min(INDEX_TOPK, L // 4)

    mod = _adaln(c, w_ada, b_ada)

    r = jnp.arange(BIAS_BLK)
    ids_a = jnp.stack([_t5_bucket(r[:, None] - r[None, :]),
                       _t5_bucket(r[:, None] - r[None, :] + BIAS_BLK)])
    bias_a = _bias_tiles(rel_bias, ids_a, shift_far=True)
    dist_b = jnp.arange(WINDOW)[:, None] + WINDOW - jnp.arange(2 * WINDOW)[None, :]
    in_window = (dist_b >= 0) & (dist_b < WINDOW)
    ids_b = jnp.where(in_window, _t5_bucket(dist_b), -1)
    ids_b0 = jnp.where(jnp.arange(2 * WINDOW)[None, :] >= WINDOW, ids_b, -1)
    bias_b = _bias_tiles(rel_bias, jnp.stack([ids_b, ids_b0]), shift_far=False)
    pairs = N_HEADS // KV_HEADS // 2
    bias_b = bias_b.reshape(KV_HEADS, pairs, 2, 2, WINDOW, 2 * WINDOW).transpose(3, 0, 1, 4, 2, 5)
    bias_b = bias_b.reshape(2, KV_HEADS, pairs * WINDOW, 4 * WINDOW)

    zero_bias = jnp.zeros((1, d), F32)
    for i in range(depth):
        sh1, sc1, g1, sh2, sc2, g2 = [m.reshape(bsz, 1, d) for m in jnp.split(mod[i], 6, axis=-1)]
        jm = i // 2
        if i % 2 == 0:
            w_in = a_w_in[jm]
            w_main = jnp.pad(w_in[:, :A_MAIN], ((0, 0), (0, A_MAIN_PAD - A_MAIN))).astype(BF16)
            w_wi_t = jnp.pad(w_in[:, A_WI0:A_WI0 + IDX_HEADS].T,
                             ((0, 16 - IDX_HEADS), (0, 0))).astype(BF16)
            q, ckv, qi, ki, wit = _a_inproj(
                x, norm_mix_g[i][None], sh1, sc1, w_main, w_wi_t, a_kv_norm_g[jm][None],
                a_idx_k_g[jm][None], a_idx_k_b[jm][None])
            mask = _a_index(qi, wit, ki, topk)
            o = _a_attn(q, ckv, mask, a_w_uk[jm].astype(BF16), a_w_uv[jm].astype(BF16), bias_a)
            w_out, b_out = a_w_out[jm], zero_bias
        else:
            q, k, v = _b_inproj(x, norm_mix_g[i][None], sh1, sc1, b_w_in[jm].astype(BF16),
                                b_b_in[jm][None])
            o = _b_attn(q, k, v, b_sinks[jm], bias_b)
            w_out, b_out = b_w_out[jm], b_b_out[jm][None]
        x = _out_ffn(o, w_out.astype(BF16), b_out, g1, x, norm_ffn_g[i][None], sh2, sc2, g2,
                     ffn_w1[i].astype(BF16), ffn_w3[i].astype(BF16), ffn_w2[i].astype(BF16),
                     norm_final_g[None], final_norm=(i == depth - 1))
    return x
```

```python
import functools
import math

import numpy as np
import jax
import jax.numpy as jnp
from jax import lax
from jax.experimental import pallas as pl
from jax.experimental.pallas import tpu as pltpu

D_MODEL = 1024
N_HEADS = 16
HEAD_DIM = 64
KV_RANK = 256
IDX_HEADS = 8
IDX_DIM = 64
INDEX_TOPK = 256
KV_HEADS = 2
WINDOW = 128
N_BUCKETS = 32
MAX_DISTANCE = 128
D_FF = 2816
RMS_EPS = 1e-6
NEG = -1e30

HD = N_HEADS * HEAD_DIM
A_Q0, A_KV0, A_QI0, A_KI0, A_WI0 = 0, HD, HD + KV_RANK, HD + KV_RANK + IDX_HEADS * IDX_DIM, \
    HD + KV_RANK + IDX_HEADS * IDX_DIM + IDX_DIM
A_MAIN = A_WI0
A_MAIN_PAD = 1920
B_IN = (N_HEADS + 2 * KV_HEADS) * HEAD_DIM

CHUNK = 256
PV_ROWS = 64
ACC_ROWS = 256
HALF_ROWS = 16
LANES = 128
BIAS_BLK = 128
VMEM_LIMIT = 56 * 1024 * 1024

F32 = jnp.float32
BF16 = jnp.bfloat16
I32 = jnp.int32
I16 = jnp.int16

_NT = (((1,), (1,)), ((), ()))


def _cparams(*sem):
    return pltpu.CompilerParams(dimension_semantics=sem, vmem_limit_bytes=VMEM_LIMIT)


def _layer_spec(stack, layer):
    zeros = (0,) * (stack.ndim - 1)
    return pl.BlockSpec((None,) + stack.shape[1:], lambda *_: (layer,) + zeros)


def _f32_key(v):
    b = int(np.array(v, np.float32).view(np.int32))
    return b ^ ((b >> 31) & 0x7FFFFFFF)


LOG2E = math.log2(math.e)
KEY_NEG = _f32_key(NEG)
INT_MIN = -(2 ** 31)


def _adaln_kernel(c_ref, w_ref, b_ref, o_ref):
    c = c_ref[...]
    cs = c * jax.nn.sigmoid(c)
    o_ref[0] = jnp.dot(cs, w_ref[0], preferred_element_type=F32,
                       precision=lax.Precision.HIGHEST) + b_ref[0]


def _adaln(c, w_ada, b_ada):
    depth, d, n = w_ada.shape
    bsz = c.shape[0]
    tn = 1536
    return pl.pallas_call(
        _adaln_kernel,
        grid=(depth, n // tn),
        in_specs=[pl.BlockSpec((bsz, d), lambda i, j: (0, 0)),
                  pl.BlockSpec((1, d, tn), lambda i, j: (i, 0, j)),
                  pl.BlockSpec((1, 1, tn), lambda i, j: (i, 0, j))],
        out_specs=pl.BlockSpec((1, bsz, tn), lambda i, j: (i, 0, j)),
        out_shape=jax.ShapeDtypeStruct((depth, bsz, n), F32),
        compiler_params=_cparams("parallel", "parallel"),
        name="adaln",
    )(c, w_ada, b_ada.reshape(depth, 1, n))


def _normmod(x, g, sh, sc):
    ms = jnp.mean(x * x, axis=-1, keepdims=True)
    y = (x * lax.rsqrt(ms + RMS_EPS)) * g
    return y * (1.0 + sc) + sh


def _a_inproj_kernel(x_ref, g_ref, sh_ref, sc_ref, w_ref, wwi_ref, kvg_ref, ikg_ref, ikb_ref,
                     q_ref, ckv_ref, qi_ref, ki_ref, wit_ref):
    h = _normmod(x_ref[0], g_ref[...], sh_ref[0], sc_ref[0]).astype(BF16)
    proj = jnp.dot(h, w_ref[...], preferred_element_type=F32)
    q_ref[0] = (proj[:, A_Q0:A_KV0] * (HEAD_DIM ** -0.5)).astype(BF16)
    ckv = proj[:, A_KV0:A_QI0]
    ckv = (ckv * lax.rsqrt(jnp.mean(ckv * ckv, axis=-1, keepdims=True) + RMS_EPS)) * kvg_ref[...]
    ckv_ref[0] = ckv.astype(BF16)
    qi_ref[0] = proj[:, A_QI0:A_KI0].astype(BF16)
    ki = proj[:, A_KI0:A_WI0]
    mu = jnp.mean(ki, axis=-1, keepdims=True)
    var = jnp.mean(jnp.square(ki - mu), axis=-1, keepdims=True)
    ki = ((ki - mu) * lax.rsqrt(var + RMS_EPS)) * ikg_ref[...] + ikb_ref[...]
    ki_ref[0] = ki.astype(BF16)
    wit = lax.dot_general(wwi_ref[...], h, _NT, preferred_element_type=F32)
    wit_ref[0] = wit[:IDX_HEADS] * (IDX_HEADS ** -0.5 * IDX_DIM ** -0.5)


def _a_inproj(x, g, sh, sc, w_main, w_wi_t, layer, kv_g, ik_g, ik_b, tm=512):
    bsz, L, d = x.shape
    row = lambda b, i: (b, i, 0)
    per_b = lambda b, i: (b, 0, 0)
    const2 = lambda b, i: (0, 0)
    return pl.pallas_call(
        _a_inproj_kernel,
        grid=(bsz, L // tm),
        in_specs=[pl.BlockSpec((1, tm, d), row),
                  pl.BlockSpec((1, d), const2),
                  pl.BlockSpec((1, 1, d), per_b),
                  pl.BlockSpec((1, 1, d), per_b),
                  _layer_spec(w_main, layer),
                  _layer_spec(w_wi_t, layer),
                  pl.BlockSpec((1, KV_RANK), const2),
                  pl.BlockSpec((1, IDX_DIM), const2),
                  pl.BlockSpec((1, IDX_DIM), const2)],
        out_specs=[pl.BlockSpec((1, tm, HD), row),
                   pl.BlockSpec((1, tm, KV_RANK), row),
                   pl.BlockSpec((1, tm, IDX_HEADS * IDX_DIM), row),
                   pl.BlockSpec((1, tm, IDX_DIM), row),
                   pl.BlockSpec((1, IDX_HEADS, tm), lambda b, i: (b, 0, i))],
        out_shape=[jax.ShapeDtypeStruct((bsz, L, HD), BF16),
                   jax.ShapeDtypeStruct((bsz, L, KV_RANK), BF16),
                   jax.ShapeDtypeStruct((bsz, L, IDX_HEADS * IDX_DIM), BF16),
                   jax.ShapeDtypeStruct((bsz, L, IDX_DIM), BF16),
                   jax.ShapeDtypeStruct((bsz, IDX_HEADS, L), F32)],
        compiler_params=_cparams("parallel", "parallel"),
        name="a_inproj",
    )(x, g, sh, sc, w_main, w_wi_t, kv_g, ik_g, ik_b)


def _a_index_kernel(qi_ref, wit_ref, ki_ref, tri_ref, mask_ref, key_scr, half_scr, *, topk,
                    seq_len):
    j = pl.program_id(1)
    nchunk = seq_len // CHUNK
    qi = qi_ref[0]
    wit = wit_ref[0]
    t_glob = j * CHUNK + lax.broadcasted_iota(I32, (CHUNK, CHUNK), 1)
    s_loc = lax.broadcasted_iota(I32, (CHUNK, CHUNK), 0)

    def chunks(fn, init):
        def pair(i, carry):
            return fn(2 * i + 1, fn(2 * i, carry))
        carry = lax.fori_loop(0, lax.shift_right_logical(j + 1, 1), pair, init)
        return lax.cond((j & 1) == 0, lambda c: fn(j, c), lambda c: c, carry)

    def score_chunk(kc, carry):
        kik = ki_ref[0, pl.ds(pl.multiple_of(kc * CHUNK, CHUNK), CHUNK), :]
        acc = jnp.zeros((CHUNK, CHUNK), F32)
        for h in range(IDX_HEADS):
            r = lax.dot_general(kik, qi[:, h * IDX_DIM:(h + 1) * IDX_DIM], _NT,
                                preferred_element_type=F32)
            acc = acc + jnp.maximum(r, 0.0) * wit[h:h + 1, :]
        sc = jnp.where(kc * CHUNK + s_loc <= t_glob, acc, NEG)
        bits = lax.bitcast_convert_type(sc, I32)
        key = bits ^ ((bits >> 31) & 0x7FFFFFFF)
        key_scr[kc] = key
        half_scr[kc] = (key >> 16).astype(I16)
        return carry

    chunks(score_chunk, 0)

    n_beyond = seq_len - (j + 1) * CHUNK
    ge = lambda a, b: a >= b
    gt = lambda a, b: a > b

    def count(pred, thr):
        def body(kc, acc):
            m = jnp.where(pred(key_scr[kc], thr), jnp.int32(1), jnp.int32(0))
            return acc + m.reshape(CHUNK // 8, 8, CHUNK).sum(axis=0)
        acc = chunks(body, jnp.zeros((8, CHUNK), I32))
        cnt = acc.sum(axis=0, keepdims=True)
        return cnt + jnp.where(pred(KEY_NEG, thr), n_beyond, 0)

    def count_half(pred, thr16):
        t = jnp.broadcast_to(thr16, (HALF_ROWS, CHUNK)).astype(I16)
        def body(kc, acc):
            m = jnp.where(pred(half_scr[kc].reshape(CHUNK // HALF_ROWS, HALF_ROWS, CHUNK), t[None]),
                          jnp.int16(1), jnp.int16(0))
            for r in range(CHUNK // HALF_ROWS):
                acc = acc + m[r]
            return acc
        acc = chunks(body, jnp.zeros((HALF_ROWS, CHUNK), I16))
        return acc.astype(I32).sum(axis=0, keepdims=True)

    def beyond(cand):
        return jnp.where(KEY_NEG >= cand, n_beyond, 0)

    zero = jnp.zeros((1, CHUNK), I32)
    thr = jnp.where(count_half(ge, zero) + beyond(zero) >= topk, zero,
                    jnp.full((1, CHUNK), INT_MIN, I32))

    def bisect_high(i, thr):
        cand = thr + lax.shift_left(jnp.int32(1), 30 - i)
        cnt = count_half(ge, cand >> 16) + beyond(cand)
        return jnp.where(cnt >= topk, cand, thr)

    thr = lax.fori_loop(0, 15, bisect_high, thr)

    thr_hi = thr >> 16
    n_above = count_half(gt, thr_hi)

    def low_half(kc, carry):
        key = key_scr[kc]
        low = (key & 0xFFFF) - 0x8000
        half_scr[kc] = jnp.where((key >> 16) == thr_hi, low, -0x8000).astype(I16)
        return carry

    chunks(low_half, 0)

    def bisect_low(i, thr):
        cand = thr + lax.shift_left(jnp.int32(1), 15 - i)
        cnt = n_above + count_half(ge, (cand & 0xFFFF) - 0x8000) + beyond(cand)
        return jnp.where(cnt >= topk, cand, thr)

    thr = lax.fori_loop(0, 16, bisect_low, thr)

    need = (topk - count(gt, thr)).astype(F32)
    tri = tri_ref[...]

    def emit(kc, seen):
        key = key_scr[kc]
        eq = jnp.where(key == thr, 1.0, 0.0)
        rank = jnp.dot(tri, eq.astype(BF16), preferred_element_type=F32) + seen
        take = jnp.where(key > thr, 1.0, jnp.where(rank < need, eq, 0.0))
        causal = kc * CHUNK + s_loc <= t_glob
        m = jnp.where(causal, jnp.where(take > 0.5, 0.0, NEG), NEG)
        mask_ref[0, kc] = m.T
        return seen + jnp.sum(eq, axis=0, keepdims=True)

    chunks(emit, jnp.zeros((1, CHUNK), F32))

    def fill(kc, carry):
        mask_ref[0, kc] = jnp.full((CHUNK, CHUNK), NEG, F32)
        return carry

    lax.fori_loop(j + 1, nchunk, fill, 0)


def _a_index(qi, wit, ki, topk):
    bsz, L, _ = qi.shape
    nchunk = L // CHUNK
    tri = jnp.tril(jnp.ones((CHUNK, CHUNK), BF16), -1)
    return pl.pallas_call(
        functools.partial(_a_index_kernel, topk=topk, seq_len=L),
        grid=(bsz, nchunk),
        in_specs=[pl.BlockSpec((1, CHUNK, IDX_HEADS * IDX_DIM), lambda b, j: (b, j, 0)),
                  pl.BlockSpec((1, IDX_HEADS, CHUNK), lambda b, j: (b, 0, j)),
                  pl.BlockSpec((1, L, IDX_DIM), lambda b, j: (b, 0, 0)),
                  pl.BlockSpec((CHUNK, CHUNK), lambda b, j: (0, 0))],
        out_specs=pl.BlockSpec((1, nchunk, CHUNK, CHUNK), lambda b, j: (b, 0, j, 0)),
        out_shape=jax.ShapeDtypeStruct((bsz, nchunk, L, CHUNK), F32),
        scratch_shapes=[pltpu.VMEM((nchunk, CHUNK, CHUNK), I32),
                        pltpu.VMEM((nchunk, CHUNK, CHUNK), I16)],
        compiler_params=_cparams("parallel", "parallel"),
        name="a_index",
    )(qi, wit, ki, tri)


def _a_attn_kernel(q_ref, ckv_ref, mask_ref, wuk_ref, wuv_ref, bias_ref, o_ref,
                   qabs_scr, lg_scr, m_scr, l_scr, acc_scr, p_scr, *, hc):
    j = pl.program_id(1)
    tq = CHUNK
    q = q_ref[0]
    for i in range(hc):
        qa = lax.dot_general(q[:, i * HEAD_DIM:(i + 1) * HEAD_DIM], wuk_ref[i], _NT,
                             preferred_element_type=F32)
        qabs_scr[i * tq:(i + 1) * tq, :] = (qa * LOG2E).astype(BF16)
    qg = qabs_scr[...]

    n_far_pairs = lax.shift_right_arithmetic(j - 1, 1)
    odd = (j & 1) == 0

    def keys(kc, width):
        return ckv_ref[0, pl.ds(kc, width)].reshape(width * CHUNK, KV_RANK)

    def logits(kc, width):
        lg = lax.dot_general(qg, keys(kc, width), _NT, preferred_element_type=F32)
        mk = jnp.concatenate([mask_ref[0, kc + w] for w in range(width)], axis=1)
        return lg + jnp.concatenate([mk] * hc, axis=0)

    def near_bias(i, with_prev):
        d0, d1 = bias_ref[i, 0], bias_ref[i, 1]
        z = jnp.zeros_like(d0)
        top, bot = [d0, z], [d1, d0]
        if with_prev:
            top, bot = [z, d1] + top, [z, z] + bot
        return jnp.concatenate([jnp.concatenate(top, axis=1), jnp.concatenate(bot, axis=1)], axis=0)

    def put_logits(kc, width, lg):
        for w in range(width):
            lg_scr[kc + w] = lg[:, w * CHUNK:(w + 1) * CHUNK]
        m = m_scr[...]
        for c in range(width * CHUNK // LANES):
            m = jnp.maximum(m, lg[:, c * LANES:(c + 1) * LANES])
        m_scr[...] = m

    m_scr[...] = jnp.full(m_scr.shape, -jnp.inf, F32)

    @pl.when(j >= 1)
    def _():
        bias = jnp.concatenate([near_bias(i, True) for i in range(hc)], axis=0)
        put_logits(j - 1, 2, logits(j - 1, 2) + bias)

    def far_pair(i, carry):
        kc = j - 3 - 2 * i
        put_logits(kc, 2, logits(kc, 2))
        return carry

    lax.fori_loop(0, n_far_pairs, far_pair, 0)

    @pl.when(odd)
    def _():
        bias = jnp.concatenate([near_bias(i, False) for i in range(hc)], axis=0)
        put_logits(0, 1, logits(0, 1) + jnp.where(j == 0, bias, 0.0))

    l_scr[...] = jnp.zeros_like(l_scr)
    acc_scr[...] = jnp.zeros_like(acc_scr)

    def pv(kc, width, reduce_max):
        ck = keys(kc, width)
        for piece in range(hc * tq // ACC_ROWS):
            for r in range(piece * ACC_ROWS // PV_ROWS, (piece + 1) * ACC_ROWS // PV_ROWS):
                rows = slice(r * PV_ROWS, (r + 1) * PV_ROWS)
                m = m_scr[rows, :]
                if reduce_max:
                    m = jnp.broadcast_to(jnp.max(m, axis=-1, keepdims=True), m.shape)
                    m_scr[rows, :] = m
                l = l_scr[rows, :]
                for w in range(width):
                    for c in range(CHUNK // LANES):
                        col = w * CHUNK + c * LANES
                        p = jnp.exp2(lg_scr[kc + w, rows, c * LANES:(c + 1) * LANES] - m)
                        l = l + p
                        p_scr[rows, col:col + LANES] = p.astype(BF16)
                l_scr[rows, :] = l
            rows = slice(piece * ACC_ROWS, (piece + 1) * ACC_ROWS)
            acc_scr[rows, :] += jnp.dot(p_scr[rows, :width * CHUNK], ck,
                                        preferred_element_type=F32)

    @pl.when(j >= 1)
    def _():
        pv(j - 1, 2, True)

    def far_pv(i, carry):
        pv(j - 3 - 2 * i, 2, False)
        return carry

    lax.fori_loop(0, n_far_pairs, far_pv, 0)

    @pl.when(odd)
    def _():
        pv(0, 1, True)

    o = acc_scr[...] / jnp.sum(l_scr[...], axis=-1, keepdims=True)
    outs = [jnp.dot(o[i * tq:(i + 1) * tq].astype(BF16), wuv_ref[i], preferred_element_type=F32)
            for i in range(hc)]
    o_ref[0] = jnp.concatenate(outs, axis=1).astype(BF16)


def _a_attn(q, ckv, mask, w_uk, w_uv, layer, bias_nd, hc=8):
    bsz, L, _ = q.shape
    nchunk = L // CHUNK
    hg = N_HEADS // hc
    m = hc * CHUNK
    ckv4 = ckv.reshape(bsz, nchunk, CHUNK, KV_RANK)
    return pl.pallas_call(
        functools.partial(_a_attn_kernel, hc=hc),
        grid=(bsz, nchunk, hg),
        in_specs=[pl.BlockSpec((1, CHUNK, hc * HEAD_DIM), lambda b, j, g: (b, j, g)),
                  pl.BlockSpec((1, nchunk, CHUNK, KV_RANK), lambda b, j, g: (b, 0, 0, 0)),
                  pl.BlockSpec((1, nchunk, CHUNK, CHUNK), lambda b, j, g: (b, 0, j, 0)),
                  pl.BlockSpec((None, hc, KV_RANK, HEAD_DIM), lambda b, j, g: (layer, g, 0, 0)),
                  pl.BlockSpec((None, hc, KV_RANK, HEAD_DIM), lambda b, j, g: (layer, g, 0, 0)),
                  pl.BlockSpec((hc, 2, BIAS_BLK, BIAS_BLK), lambda b, j, g: (g, 0, 0, 0))],
        out_specs=pl.BlockSpec((1, CHUNK, hc * HEAD_DIM), lambda b, j, g: (b, j, g)),
        out_shape=jax.ShapeDtypeStruct((bsz, L, HD), BF16),
        scratch_shapes=[pltpu.VMEM((m, KV_RANK), BF16),
                        pltpu.VMEM((nchunk, m, CHUNK), F32),
                        pltpu.VMEM((m, LANES), F32),
                        pltpu.VMEM((m, LANES), F32),
                        pltpu.VMEM((m, KV_RANK), F32),
                        pltpu.VMEM((m, 2 * CHUNK), BF16)],
        compiler_params=_cparams("parallel", "parallel", "arbitrary"),
        name="a_attn",
    )(q, ckv4, mask, w_uk, w_uv, bias_nd)


def _bias_kernel(rb_ref, ids_ref, o_ref, *, n_tiles, far_bucket, shift_far):
    h = pl.program_id(0)
    far = rb_ref[far_bucket, h] if shift_far else 0.0
    for t in range(n_tiles):
        ids = ids_ref[t]
        out = jnp.zeros(ids.shape, F32)
        for b in range(N_BUCKETS):
            out = jnp.where(ids == b, rb_ref[b, h] - far, out)
        o_ref[0, t] = jnp.where(ids < 0, NEG, out * LOG2E)


def _t5_bucket(dist):
    max_exact = N_BUCKETS // 2
    d = jnp.maximum(dist, 0)
    large = max_exact + (jnp.log(jnp.maximum(d, 1).astype(F32) / max_exact)
                         / math.log(MAX_DISTANCE / max_exact)
                         * (N_BUCKETS - max_exact)).astype(I32)
    large = jnp.minimum(large, N_BUCKETS - 1)
    return jnp.where(d < max_exact, d, large)


def _bias_tiles(rel_bias, ids, shift_far):
    n_tiles, r, c = ids.shape
    return pl.pallas_call(
        functools.partial(_bias_kernel, n_tiles=n_tiles, far_bucket=N_BUCKETS - 1,
                          shift_far=shift_far),
        grid=(N_HEADS,),
        in_specs=[pl.BlockSpec(memory_space=pltpu.SMEM),
                  pl.BlockSpec((n_tiles, r, c), lambda h: (0, 0, 0))],
        out_specs=pl.BlockSpec((1, n_tiles, r, c), lambda h: (h, 0, 0, 0)),
        out_shape=jax.ShapeDtypeStruct((N_HEADS, n_tiles, r, c), F32),
        compiler_params=_cparams("parallel"),
        name="bias_tiles",
    )(rel_bias, ids)


def _b_inproj_kernel(x_ref, g_ref, sh_ref, sc_ref, w_ref, b_ref, q_ref, k_ref, v_ref):
    h = _normmod(x_ref[0], g_ref[...], sh_ref[0], sc_ref[0]).astype(BF16)
    proj = jnp.dot(h, w_ref[...], preferred_element_type=F32) + b_ref[...]
    q_ref[0] = (proj[:, :HD] * (HEAD_DIM ** -0.5 * LOG2E)).astype(BF16)
    k_ref[0] = proj[:, HD:HD + KV_HEADS * HEAD_DIM].astype(BF16)
    v_ref[0] = proj[:, HD + KV_HEADS * HEAD_DIM:].astype(BF16)


def _b_inproj(x, g, sh, sc, w, layer, b, tm=512):
    bsz, L, d = x.shape
    kvw = KV_HEADS * HEAD_DIM
    row = lambda b_, i: (b_, i, 0)
    per_b = lambda b_, i: (b_, 0, 0)
    const2 = lambda b_, i: (0, 0)
    return pl.pallas_call(
        _b_inproj_kernel,
        grid=(bsz, L // tm),
        in_specs=[pl.BlockSpec((1, tm, d), row),
                  pl.BlockSpec((1, d), const2),
                  pl.BlockSpec((1, 1, d), per_b),
                  pl.BlockSpec((1, 1, d), per_b),
                  _layer_spec(w, layer),
                  pl.BlockSpec((1, B_IN), const2)],
        out_specs=[pl.BlockSpec((1, tm, HD), row),
                   pl.BlockSpec((1, tm, kvw), row),
                   pl.BlockSpec((1, tm, kvw), row)],
        out_shape=[jax.ShapeDtypeStruct((bsz, L, HD), BF16),
                   jax.ShapeDtypeStruct((bsz, L, kvw), BF16),
                   jax.ShapeDtypeStruct((bsz, L, kvw), BF16)],
        compiler_params=_cparams("parallel", "parallel"),
        name="b_inproj",
    )(x, g, sh, sc, w, b)


B_BLOCKS = 8


def _b_attn_kernel(sink_ref, q_ref, kp_ref, kc_ref, vp_ref, vc_ref, bias_ref, o_ref):
    n = pl.program_id(1)
    w = WINDOW
    pairs = N_HEADS // KV_HEADS // 2
    kall = jnp.concatenate([kp_ref[0], kc_ref[0]], axis=0).astype(F32)
    vall = jnp.concatenate([vp_ref[0], vc_ref[0]], axis=0).astype(F32)
    low = lax.broadcasted_iota(I32, kall.shape, 1) < HEAD_DIM

    def padded(x, g):
        swapped = pltpu.roll(x, HEAD_DIM, axis=1)
        on_low, on_high = (x, swapped) if g == 0 else (swapped, x)
        return (jnp.where(low, on_low, 0.0).astype(BF16), jnp.where(low, 0.0, on_high).astype(BF16))

    kpad = [padded(kall, g) for g in range(KV_HEADS)]
    vpad = [padded(vall, g) for g in range(KV_HEADS)]
    for blk in range(B_BLOCKS):
        variant = jnp.where(n > 0, 0, 1) if blk == 0 else 0
        keys = slice(blk * w, (blk + 2) * w)
        q = q_ref[0, blk * w:(blk + 1) * w, :]
        outs = []
        for g in range(KV_HEADS):
            blocks = [q[:, (pairs * g + p) * 2 * HEAD_DIM:(pairs * g + p + 1) * 2 * HEAD_DIM]
                      for p in range(pairs)]
            lg = lax.dot_general(jnp.concatenate(blocks, axis=0),
                                 jnp.concatenate([kpad[g][0][keys], kpad[g][1][keys]], axis=0),
                                 _NT, preferred_element_type=F32)
            lg = lg + bias_ref[variant, g]
            probs = [[], []]
            for p in range(pairs):
                for par in range(2):
                    h = 2 * pairs * g + 2 * p + par
                    t = lg[p * w:(p + 1) * w, par * 2 * w:(par + 1) * 2 * w]
                    sink = sink_ref[0, h] * LOG2E
                    m = jnp.maximum(jnp.max(t, axis=-1, keepdims=True), sink)
                    e = jnp.exp2(t - m)
                    denom = jnp.sum(e, axis=-1, keepdims=True) + jnp.exp2(sink - m)
                    probs[par].append((e * (1.0 / denom)).astype(BF16))
            og = (jnp.dot(jnp.concatenate(probs[0], axis=0), vpad[g][0][keys],
                          preferred_element_type=F32)
                  + jnp.dot(jnp.concatenate(probs[1], axis=0), vpad[g][1][keys],
                            preferred_element_type=F32))
            outs += [og[p * w:(p + 1) * w] for p in range(pairs)]
        o_ref[0, blk * w:(blk + 1) * w, :] = jnp.concatenate(outs, axis=1).astype(BF16)


def _b_attn(q, k, v, sinks, bias_b):
    bsz, L, _ = q.shape
    w = WINDOW
    wide = B_BLOCKS * w
    kvw = KV_HEADS * HEAD_DIM
    cur = lambda b, n: (b, n, 0)
    prev = lambda b, n: (b, jnp.maximum(n * B_BLOCKS - 1, 0), 0)
    return pl.pallas_call(
        _b_attn_kernel,
        grid=(bsz, L // wide),
        in_specs=[pl.BlockSpec(memory_space=pltpu.SMEM),
                  pl.BlockSpec((1, wide, HD), cur),
                  pl.BlockSpec((1, w, kvw), prev),
                  pl.BlockSpec((1, wide, kvw), cur),
                  pl.BlockSpec((1, w, kvw), prev),
                  pl.BlockSpec((1, wide, kvw), cur),
                  pl.BlockSpec(bias_b.shape, lambda b, n: (0, 0, 0, 0))],
        out_specs=pl.BlockSpec((1, wide, HD), cur),
        out_shape=jax.ShapeDtypeStruct((bsz, L, HD), BF16),
        compiler_params=_cparams("parallel", "parallel"),
        name="b_attn",
    )(sinks.reshape(1, N_HEADS), q, k, k, v, v, bias_b)


FF_CHUNK = 256


def _out_ffn_kernel(o_ref, wo_ref, bo_ref, g1_ref, x_ref, g_ref, sh_ref, sc_ref, gate_ref,
                    w1_ref, w3_ref, w2_ref, fg_ref, y_ref, *, final_norm):
    mix = jnp.dot(o_ref[0], wo_ref[...], preferred_element_type=F32) + bo_ref[...]
    x = x_ref[0] + g1_ref[0] * mix
    h = _normmod(x, g_ref[...], sh_ref[0], sc_ref[0]).astype(BF16)
    acc = jnp.zeros(x.shape, F32)
    for c in range(D_FF // FF_CHUNK):
        cs = slice(c * FF_CHUNK, (c + 1) * FF_CHUNK)
        a1 = jnp.dot(h, w1_ref[:, cs], preferred_element_type=F32)
        a3 = jnp.dot(h, w3_ref[:, cs], preferred_element_type=F32)
        act = (a1 * jax.nn.sigmoid(a1)) * a3
        acc = acc + jnp.dot(act.astype(BF16), w2_ref[cs, :], preferred_element_type=F32)
    y = x + gate_ref[0] * acc
    if final_norm:
        y = (y * lax.rsqrt(jnp.mean(y * y, axis=-1, keepdims=True) + RMS_EPS)) * fg_ref[...]
    y_ref[0] = y


def _out_ffn(o, w_out, mix_layer, b_out, g1, x, g, sh, sc, gate, w1, w3, w2, layer, final_g,
             final_norm, tm=512):
    bsz, L, d = x.shape
    row = lambda b_, i: (b_, i, 0)
    per_b = lambda b_, i: (b_, 0, 0)
    const2 = lambda b_, i: (0, 0)
    return pl.pallas_call(
        functools.partial(_out_ffn_kernel, final_norm=final_norm),
        grid=(bsz, L // tm),
        in_specs=[pl.BlockSpec((1, tm, HD), row),
                  _layer_spec(w_out, mix_layer),
                  pl.BlockSpec((1, d), const2),
                  pl.BlockSpec((1, 1, d), per_b),
                  pl.BlockSpec((1, tm, d), row),
                  pl.BlockSpec((1, d), const2),
                  pl.BlockSpec((1, 1, d), per_b),
                  pl.BlockSpec((1, 1, d), per_b),
                  pl.BlockSpec((1, 1, d), per_b),
                  _layer_spec(w1, layer),
                  _layer_spec(w3, layer),
                  _layer_spec(w2, layer),
                  pl.BlockSpec((1, d), const2)],
        out_specs=pl.BlockSpec((1, tm, d), row),
        out_shape=jax.ShapeDtypeStruct((bsz, L, d), F32),
        compiler_params=_cparams("parallel", "parallel"),
        name="out_ffn",
    )(o, w_out, b_out, g1, x, g, sh, sc, gate, w1, w3, w2, final_g)


def kernel(x, c, rel_bias, w_ada, b_ada, norm_mix_g, norm_ffn_g, a_w_in, a_kv_norm_g, a_w_uk,
           a_w_uv, a_idx_k_g, a_idx_k_b, a_w_out, b_w_in, b_b_in, b_sinks, b_w_out, b_b_out,
           ffn_w1, ffn_w3, ffn_w2, norm_final_g):
    bsz, L, d = x.shape
    depth = w_ada.shape[0]
    topk = min(INDEX_TOPK, L // 4)

    mod = _adaln(c, w_ada, b_ada)

    r = jnp.arange(BIAS_BLK)
    ids_a = jnp.stack([_t5_bucket(r[:, None] - r[None, :]),
                       _t5_bucket(r[:, None] - r[None, :] + BIAS_BLK)])
    bias_a = _bias_tiles(rel_bias, ids_a, shift_far=True)
    dist_b = jnp.arange(WINDOW)[:, None] + WINDOW - jnp.arange(2 * WINDOW)[None, :]
    in_window = (dist_b >= 0) & (dist_b < WINDOW)
    ids_b = jnp.where(in_window, _t5_bucket(dist_b), -1)
    ids_b0 = jnp.where(jnp.arange(2 * WINDOW)[None, :] >= WINDOW, ids_b, -1)
    bias_b = _bias_tiles(rel_bias, jnp.stack([ids_b, ids_b0]), shift_far=False)
    pairs = N_HEADS // KV_HEADS // 2
    bias_b = bias_b.reshape(KV_HEADS, pairs, 2, 2, WINDOW, 2 * WINDOW).transpose(3, 0, 1, 4, 2, 5)
    bias_b = bias_b.reshape(2, KV_HEADS, pairs * WINDOW, 4 * WINDOW)

    a_w_main = jnp.pad(a_w_in[:, :, :A_MAIN], ((0, 0), (0, 0), (0, A_MAIN_PAD - A_MAIN))).astype(BF16)
    a_w_wi_t = jnp.pad(a_w_in[:, :, A_WI0:A_WI0 + IDX_HEADS].transpose(0, 2, 1),
                       ((0, 0), (0, 16 - IDX_HEADS), (0, 0))).astype(BF16)
    a_w_uk, a_w_uv, a_w_out, b_w_in, b_w_out, ffn_w1, ffn_w3, ffn_w2 = [
        w.astype(BF16) for w in (a_w_uk, a_w_uv, a_w_out, b_w_in, b_w_out, ffn_w1, ffn_w3, ffn_w2)]

    zero_bias = jnp.zeros((1, d), F32)
    for i in range(depth):
        sh1, sc1, g1, sh2, sc2, g2 = [m.reshape(bsz, 1, d) for m in jnp.split(mod[i], 6, axis=-1)]
        jm = i // 2
        if i % 2 == 0:
            q, ckv, qi, ki, wit = _a_inproj(
                x, norm_mix_g[i][None], sh1, sc1, a_w_main, a_w_wi_t, jm, a_kv_norm_g[jm][None],
                a_idx_k_g[jm][None], a_idx_k_b[jm][None])
            mask = _a_index(qi, wit, ki, topk)
            o = _a_attn(q, ckv, mask, a_w_uk, a_w_uv, jm, bias_a)
            w_out, b_out = a_w_out, zero_bias
        else:
            q, k, v = _b_inproj(x, norm_mix_g[i][None], sh1, sc1, b_w_in, jm, b_b_in[jm][None])
            o = _b_attn(q, k, v, b_sinks[jm], bias_b)
            w_out, b_out = b_w_out, b_b_out[jm][None]
        x = _out_ffn(o, w_out, jm, b_out, g1, x, norm_ffn_g[i][None], sh2, sc2, g2,
                     ffn_w1, ffn_w3, ffn_w2, i, norm_final_g[None], final_norm=(i == depth - 1))
    return x
```

```python
import functools
import math

import numpy as np
import jax
import jax.numpy as jnp
from jax import lax
from jax.experimental import pallas as pl
from jax.experimental.pallas import tpu as pltpu

D_MODEL = 1024
N_HEADS = 16
HEAD_DIM = 64
KV_RANK = 256
IDX_HEADS = 8
IDX_DIM = 64
INDEX_TOPK = 256
KV_HEADS = 2
WINDOW = 128
N_BUCKETS = 32
MAX_DISTANCE = 128
D_FF = 2816
RMS_EPS = 1e-6
NEG = -1e30

HD = N_HEADS * HEAD_DIM
A_Q0, A_KV0, A_QI0, A_KI0, A_WI0 = 0, HD, HD + KV_RANK, HD + KV_RANK + IDX_HEADS * IDX_DIM, \
    HD + KV_RANK + IDX_HEADS * IDX_DIM + IDX_DIM
A_MAIN = A_WI0
A_MAIN_PAD = 1920
B_IN = (N_HEADS + 2 * KV_HEADS) * HEAD_DIM

CHUNK = 256
PV_ROWS = 64
ACC_ROWS = 256
HALF_ROWS = 16
LANES = 128
BIAS_BLK = 128
VMEM_LIMIT = 56 * 1024 * 1024

F32 = jnp.float32
BF16 = jnp.bfloat16
I32 = jnp.int32
I16 = jnp.int16

_NT = (((1,), (1,)), ((), ()))


def _cparams(*sem):
    return pltpu.CompilerParams(dimension_semantics=sem, vmem_limit_bytes=VMEM_LIMIT)


def _striped_sum(parts, stripes=4):
    accs = list(parts[:stripes])
    for i, p in enumerate(parts[stripes:]):
        accs[i % stripes] = accs[i % stripes] + p
    while len(accs) > 1:
        accs = [a + b for a, b in zip(accs[::2], accs[1::2])] + accs[len(accs) & ~1:]
    return accs[0]


def _layer_spec(stack, layer):
    zeros = (0,) * (stack.ndim - 1)
    return pl.BlockSpec((None,) + stack.shape[1:], lambda *_: (layer,) + zeros)


def _f32_key(v):
    b = int(np.array(v, np.float32).view(np.int32))
    return b ^ ((b >> 31) & 0x7FFFFFFF)


LOG2E = math.log2(math.e)
KEY_NEG = _f32_key(NEG)
INT_MIN = -(2 ** 31)


def _adaln_kernel(c_ref, w_ref, b_ref, o_ref):
    c = c_ref[...]
    cs = c * jax.nn.sigmoid(c)
    o_ref[0] = jnp.dot(cs, w_ref[0], preferred_element_type=F32,
                       precision=lax.Precision.HIGHEST) + b_ref[0]


def _adaln(c, w_ada, b_ada):
    depth, d, n = w_ada.shape
    bsz = c.shape[0]
    tn = 1536
    return pl.pallas_call(
        _adaln_kernel,
        grid=(depth, n // tn),
        in_specs=[pl.BlockSpec((bsz, d), lambda i, j: (0, 0)),
                  pl.BlockSpec((1, d, tn), lambda i, j: (i, 0, j)),
                  pl.BlockSpec((1, 1, tn), lambda i, j: (i, 0, j))],
        out_specs=pl.BlockSpec((1, bsz, tn), lambda i, j: (i, 0, j)),
        out_shape=jax.ShapeDtypeStruct((depth, bsz, n), F32),
        compiler_params=_cparams("parallel", "parallel"),
        name="adaln",
    )(c, w_ada, b_ada.reshape(depth, 1, n))


def _normmod(x, g, sh, sc):
    ms = jnp.mean(x * x, axis=-1, keepdims=True)
    y = (x * lax.rsqrt(ms + RMS_EPS)) * g
    return y * (1.0 + sc) + sh


def _a_inproj_kernel(x_ref, g_ref, sh_ref, sc_ref, w_ref, wwi_ref, kvg_ref, ikg_ref, ikb_ref,
                     q_ref, ckv_ref, qi_ref, ki_ref, wit_ref):
    h = _normmod(x_ref[0], g_ref[...], sh_ref[0], sc_ref[0]).astype(BF16)
    proj = jnp.dot(h, w_ref[...], preferred_element_type=F32)
    q_ref[0] = (proj[:, A_Q0:A_KV0] * (HEAD_DIM ** -0.5)).astype(BF16)
    ckv = proj[:, A_KV0:A_QI0]
    ckv = (ckv * lax.rsqrt(jnp.mean(ckv * ckv, axis=-1, keepdims=True) + RMS_EPS)) * kvg_ref[...]
    ckv_ref[0] = ckv.astype(BF16)
    qi_ref[0] = proj[:, A_QI0:A_KI0].astype(BF16)
    ki = proj[:, A_KI0:A_WI0]
    mu = jnp.mean(ki, axis=-1, keepdims=True)
    var = jnp.mean(jnp.square(ki - mu), axis=-1, keepdims=True)
    ki = ((ki - mu) * lax.rsqrt(var + RMS_EPS)) * ikg_ref[...] + ikb_ref[...]
    ki_ref[0] = ki.astype(BF16)
    wit = lax.dot_general(wwi_ref[...], h, _NT, preferred_element_type=F32)
    wit_ref[0] = wit[:IDX_HEADS] * (IDX_HEADS ** -0.5 * IDX_DIM ** -0.5)


def _a_inproj(x, g, sh, sc, w_main, w_wi_t, layer, kv_g, ik_g, ik_b, tm=512):
    bsz, L, d = x.shape
    row = lambda b, i: (b, i, 0)
    per_b = lambda b, i: (b, 0, 0)
    const2 = lambda b, i: (0, 0)
    return pl.pallas_call(
        _a_inproj_kernel,
        grid=(bsz, L // tm),
        in_specs=[pl.BlockSpec((1, tm, d), row),
                  pl.BlockSpec((1, d), const2),
                  pl.BlockSpec((1, 1, d), per_b),
                  pl.BlockSpec((1, 1, d), per_b),
                  _layer_spec(w_main, layer),
                  _layer_spec(w_wi_t, layer),
                  pl.BlockSpec((1, KV_RANK), const2),
                  pl.BlockSpec((1, IDX_DIM), const2),
                  pl.BlockSpec((1, IDX_DIM), const2)],
        out_specs=[pl.BlockSpec((1, tm, HD), row),
                   pl.BlockSpec((1, tm, KV_RANK), row),
                   pl.BlockSpec((1, tm, IDX_HEADS * IDX_DIM), row),
                   pl.BlockSpec((1, tm, IDX_DIM), row),
                   pl.BlockSpec((1, IDX_HEADS, tm), lambda b, i: (b, 0, i))],
        out_shape=[jax.ShapeDtypeStruct((bsz, L, HD), BF16),
                   jax.ShapeDtypeStruct((bsz, L, KV_RANK), BF16),
                   jax.ShapeDtypeStruct((bsz, L, IDX_HEADS * IDX_DIM), BF16),
                   jax.ShapeDtypeStruct((bsz, L, IDX_DIM), BF16),
                   jax.ShapeDtypeStruct((bsz, IDX_HEADS, L), F32)],
        compiler_params=_cparams("parallel", "parallel"),
        name="a_inproj",
    )(x, g, sh, sc, w_main, w_wi_t, kv_g, ik_g, ik_b)


def _a_index_kernel(qi_ref, wit_ref, ki_ref, tri_ref, mask_ref, key_scr, half_scr, *, topk,
                    seq_len):
    j = pl.program_id(1)
    nchunk = seq_len // CHUNK
    qi = qi_ref[0]
    wit = wit_ref[0]
    t_glob = j * CHUNK + lax.broadcasted_iota(I32, (CHUNK, CHUNK), 1)
    s_loc = lax.broadcasted_iota(I32, (CHUNK, CHUNK), 0)

    def chunks(fn, init):
        def pair(i, carry):
            return fn(2 * i + 1, fn(2 * i, carry))
        carry = lax.fori_loop(0, lax.shift_right_logical(j + 1, 1), pair, init)
        return lax.cond((j & 1) == 0, lambda c: fn(j, c), lambda c: c, carry)

    def score_chunk(kc, carry):
        kik = ki_ref[0, pl.ds(pl.multiple_of(kc * CHUNK, CHUNK), CHUNK), :]
        acc = jnp.zeros((CHUNK, CHUNK), F32)
        for h in range(IDX_HEADS):
            r = lax.dot_general(kik, qi[:, h * IDX_DIM:(h + 1) * IDX_DIM], _NT,
                                preferred_element_type=F32)
            acc = acc + jnp.maximum(r, 0.0) * wit[h:h + 1, :]
        sc = jnp.where(kc * CHUNK + s_loc <= t_glob, acc, NEG)
        bits = lax.bitcast_convert_type(sc, I32)
        key = bits ^ ((bits >> 31) & 0x7FFFFFFF)
        key_scr[kc] = key
        half_scr[kc] = (key >> 16).astype(I16)
        return carry

    chunks(score_chunk, 0)

    def non_causal(kc, carry):
        key_scr[kc] = jnp.full((CHUNK, CHUNK), KEY_NEG, I32)
        half_scr[kc] = jnp.full((CHUNK, CHUNK), KEY_NEG >> 16, I16)
        return carry

    lax.fori_loop(j + 1, nchunk, non_causal, 0)

    ge = lambda a, b: a >= b
    gt = lambda a, b: a > b

    def select(cover):
        n_beyond = seq_len - cover * CHUNK

        def count_half(pred, thr16):
            t = jnp.broadcast_to(thr16, (HALF_ROWS, CHUNK)).astype(I16)
            parts = []
            for kc in range(cover):
                m = jnp.where(pred(half_scr[kc].reshape(CHUNK // HALF_ROWS, HALF_ROWS, CHUNK),
                                   t[None]), jnp.int16(1), jnp.int16(0))
                parts += [m[r] for r in range(CHUNK // HALF_ROWS)]
            return _striped_sum(parts).astype(I32).sum(axis=0, keepdims=True)

        def beyond(pred, cand):
            return jnp.where(pred(KEY_NEG, cand), n_beyond, 0) if n_beyond else 0

        zero = jnp.zeros((1, CHUNK), I32)
        thr = jnp.where(count_half(ge, zero) + beyond(ge, zero) >= topk, zero,
                        jnp.full((1, CHUNK), INT_MIN, I32))

        def bisect_high(i, thr):
            cand = thr + lax.shift_left(jnp.int32(1), 30 - i)
            cnt = count_half(ge, cand >> 16) + beyond(ge, cand)
            return jnp.where(cnt >= topk, cand, thr)

        thr = lax.fori_loop(0, 15, bisect_high, thr)

        thr_hi = thr >> 16
        n_above = count_half(gt, thr_hi)
        for kc in range(cover):
            key = key_scr[kc]
            low = (key & 0xFFFF) - 0x8000
            half_scr[kc] = jnp.where((key >> 16) == thr_hi, low, -0x8000).astype(I16)

        def bisect_low(i, thr):
            cand = thr + lax.shift_left(jnp.int32(1), 15 - i)
            cnt = n_above + count_half(ge, (cand & 0xFFFF) - 0x8000) + beyond(ge, cand)
            return jnp.where(cnt >= topk, cand, thr)

        thr = lax.fori_loop(0, 16, bisect_low, thr)

        acc = jnp.zeros((8, CHUNK), I32)
        for kc in range(cover):
            m = jnp.where(key_scr[kc] > thr, jnp.int32(1), jnp.int32(0))
            acc = acc + m.reshape(CHUNK // 8, 8, CHUNK).sum(axis=0)
        n_gt = acc.sum(axis=0, keepdims=True) + beyond(gt, thr)
        return thr, (topk - n_gt).astype(F32)

    thr, need = lax.cond(j < nchunk // 2, lambda: select(nchunk // 2), lambda: select(nchunk))
    tri = tri_ref[...]

    def emit(kc, seen):
        key = key_scr[kc]
        eq = jnp.where(key == thr, 1.0, 0.0)
        rank = jnp.dot(tri, eq.astype(BF16), preferred_element_type=F32) + seen
        take = jnp.where(key > thr, 1.0, jnp.where(rank < need, eq, 0.0))
        causal = kc * CHUNK + s_loc <= t_glob
        m = jnp.where(causal, jnp.where(take > 0.5, 0.0, NEG), NEG)
        mask_ref[0, kc] = m.T
        return seen + jnp.sum(eq, axis=0, keepdims=True)

    chunks(emit, jnp.zeros((1, CHUNK), F32))

    def fill(kc, carry):
        mask_ref[0, kc] = jnp.full((CHUNK, CHUNK), NEG, F32)
        return carry

    lax.fori_loop(j + 1, nchunk, fill, 0)


def _a_index(qi, wit, ki, topk):
    bsz, L, _ = qi.shape
    nchunk = L // CHUNK
    tri = jnp.tril(jnp.ones((CHUNK, CHUNK), BF16), -1)
    return pl.pallas_call(
        functools.partial(_a_index_kernel, topk=topk, seq_len=L),
        grid=(bsz, nchunk),
        in_specs=[pl.BlockSpec((1, CHUNK, IDX_HEADS * IDX_DIM), lambda b, j: (b, j, 0)),
                  pl.BlockSpec((1, IDX_HEADS, CHUNK), lambda b, j: (b, 0, j)),
                  pl.BlockSpec((1, L, IDX_DIM), lambda b, j: (b, 0, 0)),
                  pl.BlockSpec((CHUNK, CHUNK), lambda b, j: (0, 0))],
        out_specs=pl.BlockSpec((1, nchunk, CHUNK, CHUNK), lambda b, j: (b, 0, j, 0)),
        out_shape=jax.ShapeDtypeStruct((bsz, nchunk, L, CHUNK), F32),
        scratch_shapes=[pltpu.VMEM((nchunk, CHUNK, CHUNK), I32),
                        pltpu.VMEM((nchunk, CHUNK, CHUNK), I16)],
        compiler_params=_cparams("parallel", "parallel"),
        name="a_index",
    )(qi, wit, ki, tri)


def _a_attn_kernel(q_ref, ckv_ref, mask_ref, wuk_ref, wuv_ref, bias_ref, o_ref,
                   qabs_scr, lg_scr, m_scr, l_scr, acc_scr, p_scr, *, hc):
    j = pl.program_id(1)
    tq = CHUNK
    q = q_ref[0]
    for i in range(hc):
        qa = lax.dot_general(q[:, i * HEAD_DIM:(i + 1) * HEAD_DIM], wuk_ref[i], _NT,
                             preferred_element_type=F32)
        qabs_scr[i * tq:(i + 1) * tq, :] = (qa * LOG2E).astype(BF16)
    qg = qabs_scr[...]

    n_far_pairs = lax.shift_right_arithmetic(j - 1, 1)
    odd = (j & 1) == 0

    def keys(kc, width):
        return ckv_ref[0, pl.ds(kc, width)].reshape(width * CHUNK, KV_RANK)

    def logits(kc, width):
        lg = lax.dot_general(qg, keys(kc, width), _NT, preferred_element_type=F32)
        mk = jnp.concatenate([mask_ref[0, kc + w] for w in range(width)], axis=1)
        return lg + jnp.concatenate([mk] * hc, axis=0)

    def near_bias(i, with_prev):
        d0, d1 = bias_ref[i, 0], bias_ref[i, 1]
        z = jnp.zeros_like(d0)
        top, bot = [d0, z], [d1, d0]
        if with_prev:
            top, bot = [z, d1] + top, [z, z] + bot
        return jnp.concatenate([jnp.concatenate(top, axis=1), jnp.concatenate(bot, axis=1)], axis=0)

    def put_logits(kc, width, lg):
        for w in range(width):
            lg_scr[kc + w] = lg[:, w * CHUNK:(w + 1) * CHUNK]
        m = m_scr[...]
        for c in range(width * CHUNK // LANES):
            m = jnp.maximum(m, lg[:, c * LANES:(c + 1) * LANES])
        m_scr[...] = m

    m_scr[...] = jnp.full(m_scr.shape, -jnp.inf, F32)

    @pl.when(j >= 1)
    def _():
        bias = jnp.concatenate([near_bias(i, True) for i in range(hc)], axis=0)
        put_logits(j - 1, 2, logits(j - 1, 2) + bias)

    def far_pair(i, carry):
        kc = j - 3 - 2 * i
        put_logits(kc, 2, logits(kc, 2))
        return carry

    lax.fori_loop(0, n_far_pairs, far_pair, 0)

    @pl.when(odd)
    def _():
        bias = jnp.concatenate([near_bias(i, False) for i in range(hc)], axis=0)
        put_logits(0, 1, logits(0, 1) + jnp.where(j == 0, bias, 0.0))

    l_scr[...] = jnp.zeros_like(l_scr)
    acc_scr[...] = jnp.zeros_like(acc_scr)

    def pv(kc, width, reduce_max):
        ck = keys(kc, width)
        for piece in range(hc * tq // ACC_ROWS):
            for r in range(piece * ACC_ROWS // PV_ROWS, (piece + 1) * ACC_ROWS // PV_ROWS):
                rows = slice(r * PV_ROWS, (r + 1) * PV_ROWS)
                m = m_scr[rows, :]
                if reduce_max:
                    m = jnp.broadcast_to(jnp.max(m, axis=-1, keepdims=True), m.shape)
                    m_scr[rows, :] = m
                l = l_scr[rows, :]
                for w in range(width):
                    for c in range(CHUNK // LANES):
                        col = w * CHUNK + c * LANES
                        p = jnp.exp2(lg_scr[kc + w, rows, c * LANES:(c + 1) * LANES] - m)
                        l = l + p
                        p_scr[rows, col:col + LANES] = p.astype(BF16)
                l_scr[rows, :] = l
            rows = slice(piece * ACC_ROWS, (piece + 1) * ACC_ROWS)
            acc_scr[rows, :] += jnp.dot(p_scr[rows, :width * CHUNK], ck,
                                        preferred_element_type=F32)

    @pl.when(j >= 1)
    def _():
        pv(j - 1, 2, True)

    def far_pv(i, carry):
        pv(j - 3 - 2 * i, 2, False)
        return carry

    lax.fori_loop(0, n_far_pairs, far_pv, 0)

    @pl.when(odd)
    def _():
        pv(0, 1, True)

    o = acc_scr[...] / jnp.sum(l_scr[...], axis=-1, keepdims=True)
    outs = [jnp.dot(o[i * tq:(i + 1) * tq].astype(BF16), wuv_ref[i], preferred_element_type=F32)
            for i in range(hc)]
    o_ref[0] = jnp.concatenate(outs, axis=1).astype(BF16)


def _a_attn(q, ckv, mask, w_uk, w_uv, layer, bias_nd, hc=8):
    bsz, L, _ = q.shape
    nchunk = L // CHUNK
    hg = N_HEADS // hc
    m = hc * CHUNK
    ckv4 = ckv.reshape(bsz, nchunk, CHUNK, KV_RANK)
    return pl.pallas_call(
        functools.partial(_a_attn_kernel, hc=hc),
        grid=(bsz, nchunk, hg),
        in_specs=[pl.BlockSpec((1, CHUNK, hc * HEAD_DIM), lambda b, j, g: (b, j, g)),
                  pl.BlockSpec((1, nchunk, CHUNK, KV_RANK), lambda b, j, g: (b, 0, 0, 0)),
                  pl.BlockSpec((1, nchunk, CHUNK, CHUNK), lambda b, j, g: (b, 0, j, 0)),
                  pl.BlockSpec((None, hc, KV_RANK, HEAD_DIM), lambda b, j, g: (layer, g, 0, 0)),
                  pl.BlockSpec((None, hc, KV_RANK, HEAD_DIM), lambda b, j, g: (layer, g, 0, 0)),
                  pl.BlockSpec((hc, 2, BIAS_BLK, BIAS_BLK), lambda b, j, g: (g, 0, 0, 0))],
        out_specs=pl.BlockSpec((1, CHUNK, hc * HEAD_DIM), lambda b, j, g: (b, j, g)),
        out_shape=jax.ShapeDtypeStruct((bsz, L, HD), BF16),
        scratch_shapes=[pltpu.VMEM((m, KV_RANK), BF16),
                        pltpu.VMEM((nchunk, m, CHUNK), F32),
                        pltpu.VMEM((m, LANES), F32),
                        pltpu.VMEM((m, LANES), F32),
                        pltpu.VMEM((m, KV_RANK), F32),
                        pltpu.VMEM((m, 2 * CHUNK), BF16)],
        compiler_params=_cparams("parallel", "parallel", "arbitrary"),
        name="a_attn",
    )(q, ckv4, mask, w_uk, w_uv, bias_nd)


def _bias_kernel(rb_ref, ids_ref, o_ref, *, n_tiles, far_bucket, shift_far):
    h = pl.program_id(0)
    far = rb_ref[far_bucket, h] if shift_far else 0.0
    for t in range(n_tiles):
        ids = ids_ref[t]
        out = jnp.zeros(ids.shape, F32)
        for b in range(N_BUCKETS):
            out = jnp.where(ids == b, rb_ref[b, h] - far, out)
        o_ref[0, t] = jnp.where(ids < 0, NEG, out * LOG2E)


def _t5_bucket(dist):
    max_exact = N_BUCKETS // 2
    d = jnp.maximum(dist, 0)
    large = max_exact + (jnp.log(jnp.maximum(d, 1).astype(F32) / max_exact)
                         / math.log(MAX_DISTANCE / max_exact)
                         * (N_BUCKETS - max_exact)).astype(I32)
    large = jnp.minimum(large, N_BUCKETS - 1)
    return jnp.where(d < max_exact, d, large)


def _bias_tiles(rel_bias, ids, shift_far):
    n_tiles, r, c = ids.shape
    return pl.pallas_call(
        functools.partial(_bias_kernel, n_tiles=n_tiles, far_bucket=N_BUCKETS - 1,
                          shift_far=shift_far),
        grid=(N_HEADS,),
        in_specs=[pl.BlockSpec(memory_space=pltpu.SMEM),
                  pl.BlockSpec((n_tiles, r, c), lambda h: (0, 0, 0))],
        out_specs=pl.BlockSpec((1, n_tiles, r, c), lambda h: (h, 0, 0, 0)),
        out_shape=jax.ShapeDtypeStruct((N_HEADS, n_tiles, r, c), F32),
        compiler_params=_cparams("parallel"),
        name="bias_tiles",
    )(rel_bias, ids)


def _b_inproj_kernel(x_ref, g_ref, sh_ref, sc_ref, w_ref, b_ref, q_ref, k_ref, v_ref):
    h = _normmod(x_ref[0], g_ref[...], sh_ref[0], sc_ref[0]).astype(BF16)
    proj = jnp.dot(h, w_ref[...], preferred_element_type=F32) + b_ref[...]
    q_ref[0] = (proj[:, :HD] * (HEAD_DIM ** -0.5 * LOG2E)).astype(BF16)
    k_ref[0] = proj[:, HD:HD + KV_HEADS * HEAD_DIM].astype(BF16)
    v_ref[0] = proj[:, HD + KV_HEADS * HEAD_DIM:].astype(BF16)


def _b_inproj(x, g, sh, sc, w, layer, b, tm=512):
    bsz, L, d = x.shape
    kvw = KV_HEADS * HEAD_DIM
    row = lambda b_, i: (b_, i, 0)
    per_b = lambda b_, i: (b_, 0, 0)
    const2 = lambda b_, i: (0, 0)
    return pl.pallas_call(
        _b_inproj_kernel,
        grid=(bsz, L // tm),
        in_specs=[pl.BlockSpec((1, tm, d), row),
                  pl.BlockSpec((1, d), const2),
                  pl.BlockSpec((1, 1, d), per_b),
                  pl.BlockSpec((1, 1, d), per_b),
                  _layer_spec(w, layer),
                  pl.BlockSpec((1, B_IN), const2)],
        out_specs=[pl.BlockSpec((1, tm, HD), row),
                   pl.BlockSpec((1, tm, kvw), row),
                   pl.BlockSpec((1, tm, kvw), row)],
        out_shape=[jax.ShapeDtypeStruct((bsz, L, HD), BF16),
                   jax.ShapeDtypeStruct((bsz, L, kvw), BF16),
                   jax.ShapeDtypeStruct((bsz, L, kvw), BF16)],
        compiler_params=_cparams("parallel", "parallel"),
        name="b_inproj",
    )(x, g, sh, sc, w, b)


B_BLOCKS = 8


def _b_attn_kernel(sink_ref, q_ref, kp_ref, kc_ref, vp_ref, vc_ref, bias_ref, o_ref):
    n = pl.program_id(1)
    w = WINDOW
    pairs = N_HEADS // KV_HEADS // 2
    kall = jnp.concatenate([kp_ref[0], kc_ref[0]], axis=0).astype(F32)
    vall = jnp.concatenate([vp_ref[0], vc_ref[0]], axis=0).astype(F32)
    low = lax.broadcasted_iota(I32, kall.shape, 1) < HEAD_DIM

    def padded(x, g):
        swapped = pltpu.roll(x, HEAD_DIM, axis=1)
        on_low, on_high = (x, swapped) if g == 0 else (swapped, x)
        return (jnp.where(low, on_low, 0.0).astype(BF16), jnp.where(low, 0.0, on_high).astype(BF16))

    kpad = [padded(kall, g) for g in range(KV_HEADS)]
    vpad = [padded(vall, g) for g in range(KV_HEADS)]
    for blk in range(B_BLOCKS):
        variant = jnp.where(n > 0, 0, 1) if blk == 0 else 0
        keys = slice(blk * w, (blk + 2) * w)
        q = q_ref[0, blk * w:(blk + 1) * w, :]
        outs = []
        for g in range(KV_HEADS):
            blocks = [q[:, (pairs * g + p) * 2 * HEAD_DIM:(pairs * g + p + 1) * 2 * HEAD_DIM]
                      for p in range(pairs)]
            lg = lax.dot_general(jnp.concatenate(blocks, axis=0),
                                 jnp.concatenate([kpad[g][0][keys], kpad[g][1][keys]], axis=0),
                                 _NT, preferred_element_type=F32)
            lg = lg + bias_ref[variant, g]
            probs = [[], []]
            for p in range(pairs):
                for par in range(2):
                    h = 2 * pairs * g + 2 * p + par
                    t = lg[p * w:(p + 1) * w, par * 2 * w:(par + 1) * 2 * w]
                    sink = sink_ref[0, h] * LOG2E
                    m = jnp.maximum(jnp.max(t, axis=-1, keepdims=True), sink)
                    e = jnp.exp2(t - m)
                    denom = jnp.sum(e, axis=-1, keepdims=True) + jnp.exp2(sink - m)
                    probs[par].append((e * (1.0 / denom)).astype(BF16))
            og = (jnp.dot(jnp.concatenate(probs[0], axis=0), vpad[g][0][keys],
                          preferred_element_type=F32)
                  + jnp.dot(jnp.concatenate(probs[1], axis=0), vpad[g][1][keys],
                            preferred_element_type=F32))
            outs += [og[p * w:(p + 1) * w] for p in range(pairs)]
        o_ref[0, blk * w:(blk + 1) * w, :] = jnp.concatenate(outs, axis=1).astype(BF16)


def _b_attn(q, k, v, sinks, bias_b):
    bsz, L, _ = q.shape
    w = WINDOW
    wide = B_BLOCKS * w
    assert L % wide == 0, (L, wide)
    kvw = KV_HEADS * HEAD_DIM
    cur = lambda b, n: (b, n, 0)
    prev = lambda b, n: (b, jnp.maximum(n * B_BLOCKS - 1, 0), 0)
    return pl.pallas_call(
        _b_attn_kernel,
        grid=(bsz, L // wide),
        in_specs=[pl.BlockSpec(memory_space=pltpu.SMEM),
                  pl.BlockSpec((1, wide, HD), cur),
                  pl.BlockSpec((1, w, kvw), prev),
                  pl.BlockSpec((1, wide, kvw), cur),
                  pl.BlockSpec((1, w, kvw), prev),
                  pl.BlockSpec((1, wide, kvw), cur),
                  pl.BlockSpec(bias_b.shape, lambda b, n: (0, 0, 0, 0))],
        out_specs=pl.BlockSpec((1, wide, HD), cur),
        out_shape=jax.ShapeDtypeStruct((bsz, L, HD), BF16),
        compiler_params=_cparams("parallel", "parallel"),
        name="b_attn",
    )(sinks.reshape(1, N_HEADS), q, k, k, v, v, bias_b)


FF_CHUNK = 256


def _out_ffn_kernel(o_ref, wo_ref, bo_ref, g1_ref, x_ref, g_ref, sh_ref, sc_ref, gate_ref,
                    w1_ref, w3_ref, w2_ref, fg_ref, y_ref, *, final_norm):
    mix = jnp.dot(o_ref[0], wo_ref[...], preferred_element_type=F32) + bo_ref[...]
    x = x_ref[0] + g1_ref[0] * mix
    h = _normmod(x, g_ref[...], sh_ref[0], sc_ref[0]).astype(BF16)
    acc = jnp.zeros(x.shape, F32)
    for c in range(D_FF // FF_CHUNK):
        cs = slice(c * FF_CHUNK, (c + 1) * FF_CHUNK)
        a1 = jnp.dot(h, w1_ref[:, cs], preferred_element_type=F32)
        a3 = jnp.dot(h, w3_ref[:, cs], preferred_element_type=F32)
        act = (a1 * jax.nn.sigmoid(a1)) * a3
        acc = acc + jnp.dot(act.astype(BF16), w2_ref[cs, :], preferred_element_type=F32)
    y = x + gate_ref[0] * acc
    if final_norm:
        y = (y * lax.rsqrt(jnp.mean(y * y, axis=-1, keepdims=True) + RMS_EPS)) * fg_ref[...]
    y_ref[0] = y


def _out_ffn(o, w_out, mix_layer, b_out, g1, x, g, sh, sc, gate, w1, w3, w2, layer, final_g,
             final_norm, tm=512):
    bsz, L, d = x.shape
    row = lambda b_, i: (b_, i, 0)
    per_b = lambda b_, i: (b_, 0, 0)
    const2 = lambda b_, i: (0, 0)
    return pl.pallas_call(
        functools.partial(_out_ffn_kernel, final_norm=final_norm),
        grid=(bsz, L // tm),
        in_specs=[pl.BlockSpec((1, tm, HD), row),
                  _layer_spec(w_out, mix_layer),
                  pl.BlockSpec((1, d), const2),
                  pl.BlockSpec((1, 1, d), per_b),
                  pl.BlockSpec((1, tm, d), row),
                  pl.BlockSpec((1, d), const2),
                  pl.BlockSpec((1, 1, d), per_b),
                  pl.BlockSpec((1, 1, d), per_b),
                  pl.BlockSpec((1, 1, d), per_b),
                  _layer_spec(w1, layer),
                  _layer_spec(w3, layer),
                  _layer_spec(w2, layer),
                  pl.BlockSpec((1, d), const2)],
        out_specs=pl.BlockSpec((1, tm, d), row),
        out_shape=jax.ShapeDtypeStruct((bsz, L, d), F32),
        compiler_params=_cparams("parallel", "parallel"),
        name="out_ffn",
    )(o, w_out, b_out, g1, x, g, sh, sc, gate, w1, w3, w2, final_g)


def kernel(x, c, rel_bias, w_ada, b_ada, norm_mix_g, norm_ffn_g, a_w_in, a_kv_norm_g, a_w_uk,
           a_w_uv, a_idx_k_g, a_idx_k_b, a_w_out, b_w_in, b_b_in, b_sinks, b_w_out, b_b_out,
           ffn_w1, ffn_w3, ffn_w2, norm_final_g):
    bsz, L, d = x.shape
    depth = w_ada.shape[0]
    topk = min(INDEX_TOPK, L // 4)

    mod = _adaln(c, w_ada, b_ada)

    r = jnp.arange(BIAS_BLK)
    ids_a = jnp.stack([_t5_bucket(r[:, None] - r[None, :]),
                       _t5_bucket(r[:, None] - r[None, :] + BIAS_BLK)])
    bias_a = _bias_tiles(rel_bias, ids_a, shift_far=True)
    dist_b = jnp.arange(WINDOW)[:, None] + WINDOW - jnp.arange(2 * WINDOW)[None, :]
    in_window = (dist_b >= 0) & (dist_b < WINDOW)
    ids_b = jnp.where(in_window, _t5_bucket(dist_b), -1)
    ids_b0 = jnp.where(jnp.arange(2 * WINDOW)[None, :] >= WINDOW, ids_b, -1)
    bias_b = _bias_tiles(rel_bias, jnp.stack([ids_b, ids_b0]), shift_far=False)
    pairs = N_HEADS // KV_HEADS // 2
    bias_b = bias_b.reshape(KV_HEADS, pairs, 2, 2, WINDOW, 2 * WINDOW).transpose(3, 0, 1, 4, 2, 5)
    bias_b = bias_b.reshape(2, KV_HEADS, pairs * WINDOW, 4 * WINDOW)

    a_w_main = jnp.pad(a_w_in[:, :, :A_MAIN], ((0, 0), (0, 0), (0, A_MAIN_PAD - A_MAIN))).astype(BF16)
    a_w_wi_t = jnp.pad(a_w_in[:, :, A_WI0:A_WI0 + IDX_HEADS].transpose(0, 2, 1),
                       ((0, 0), (0, 16 - IDX_HEADS), (0, 0))).astype(BF16)
    a_w_uk, a_w_uv, a_w_out, b_w_in, b_w_out, ffn_w1, ffn_w3, ffn_w2 = [
        w.astype(BF16) for w in (a_w_uk, a_w_uv, a_w_out, b_w_in, b_w_out, ffn_w1, ffn_w3, ffn_w2)]

    zero_bias = jnp.zeros((1, d), F32)
    for i in range(depth):
        sh1, sc1, g1, sh2, sc2, g2 = [m.reshape(bsz, 1, d) for m in jnp.split(mod[i], 6, axis=-1)]
        jm = i // 2
        if i % 2 == 0:
            q, ckv, qi, ki, wit = _a_inproj(
                x, norm_mix_g[i][None], sh1, sc1, a_w_main, a_w_wi_t, jm, a_kv_norm_g[jm][None],
                a_idx_k_g[jm][None], a_idx_k_b[jm][None])
            mask = _a_index(qi, wit, ki, topk)
            o = _a_attn(q, ckv, mask, a_w_uk, a_w_uv, jm, bias_a)
            w_out, b_out = a_w_out, zero_bias
        else:
            q, k, v = _b_inproj(x, norm_mix_g[i][None], sh1, sc1, b_w_in, jm, b_b_in[jm][None])
            o = _b_attn(q, k, v, b_sinks[jm], bias_b)
            w_out, b_out = b_w_out, b_b_out[jm][None]
        x = _out_ffn(o, w_out, jm, b_out, g1, x, norm_ffn_g[i][None], sh2, sc2, g2,
                     ffn_w1, ffn_w3, ffn_w2, i, norm_final_g[None], final_norm=(i == depth - 1))
    return x
```

```python
import functools
import math

import numpy as np
import jax
import jax.numpy as jnp
from jax import lax
from jax.experimental import pallas as pl
from jax.experimental.pallas import tpu as pltpu

D_MODEL = 1024
N_HEADS = 16
HEAD_DIM = 64
KV_RANK = 256
IDX_HEADS = 8
IDX_DIM = 64
INDEX_TOPK = 256
KV_HEADS = 2
WINDOW = 128
N_BUCKETS = 32
MAX_DISTANCE = 128
D_FF = 2816
RMS_EPS = 1e-6
NEG = -1e30

HD = N_HEADS * HEAD_DIM
A_Q0, A_KV0, A_QI0, A_KI0, A_WI0 = 0, HD, HD + KV_RANK, HD + KV_RANK + IDX_HEADS * IDX_DIM, \
    HD + KV_RANK + IDX_HEADS * IDX_DIM + IDX_DIM
A_MAIN = A_WI0
A_MAIN_PAD = 1920
B_IN = (N_HEADS + 2 * KV_HEADS) * HEAD_DIM

CHUNK = 256
PV_ROWS = 64
ACC_ROWS = 256
HALF_ROWS = 16
LANES = 128
BIAS_BLK = 128
VMEM_LIMIT = 56 * 1024 * 1024

F32 = jnp.float32
BF16 = jnp.bfloat16
I32 = jnp.int32
I16 = jnp.int16

_NT = (((1,), (1,)), ((), ()))


def _cparams(*sem):
    return pltpu.CompilerParams(dimension_semantics=sem, vmem_limit_bytes=VMEM_LIMIT)


def _pair_blocks(w):
    even, odd = w[..., 0::2, :, :], w[..., 1::2, :, :]
    zero = jnp.zeros_like(even)
    return jnp.concatenate([jnp.concatenate([even, zero], axis=-1),
                            jnp.concatenate([zero, odd], axis=-1)], axis=-2)


def _layer_spec(stack, layer):
    zeros = (0,) * (stack.ndim - 1)
    return pl.BlockSpec((None,) + stack.shape[1:], lambda *_: (layer,) + zeros)


def _f32_key(v):
    b = int(np.array(v, np.float32).view(np.int32))
    return b ^ ((b >> 31) & 0x7FFFFFFF)


LOG2E = math.log2(math.e)
KEY_NEG = _f32_key(NEG)
INT_MIN = -(2 ** 31)


def _adaln_kernel(c_ref, w_ref, b_ref, o_ref):
    c = c_ref[...]
    cs = c * jax.nn.sigmoid(c)
    o_ref[0] = jnp.dot(cs, w_ref[0], preferred_element_type=F32,
                       precision=lax.Precision.HIGHEST) + b_ref[0]


def _adaln(c, w_ada, b_ada):
    depth, d, n = w_ada.shape
    bsz = c.shape[0]
    tn = 1536
    return pl.pallas_call(
        _adaln_kernel,
        grid=(depth, n // tn),
        in_specs=[pl.BlockSpec((bsz, d), lambda i, j: (0, 0)),
                  pl.BlockSpec((1, d, tn), lambda i, j: (i, 0, j)),
                  pl.BlockSpec((1, 1, tn), lambda i, j: (i, 0, j))],
        out_specs=pl.BlockSpec((1, bsz, tn), lambda i, j: (i, 0, j)),
        out_shape=jax.ShapeDtypeStruct((depth, bsz, n), F32),
        compiler_params=_cparams("parallel", "parallel"),
        name="adaln",
    )(c, w_ada, b_ada.reshape(depth, 1, n))


def _normmod(x, g, sh, sc):
    ms = jnp.mean(x * x, axis=-1, keepdims=True)
    y = (x * lax.rsqrt(ms + RMS_EPS)) * g
    return y * (1.0 + sc) + sh


def _a_inproj_kernel(x_ref, g_ref, sh_ref, sc_ref, w_ref, wwi_ref, kvg_ref, ikg_ref, ikb_ref,
                     q_ref, ckv_ref, qi_ref, ki_ref, wit_ref):
    h = _normmod(x_ref[0], g_ref[...], sh_ref[0], sc_ref[0]).astype(BF16)
    proj = jnp.dot(h, w_ref[...], preferred_element_type=F32)
    q_ref[0] = (proj[:, A_Q0:A_KV0] * (HEAD_DIM ** -0.5)).astype(BF16)
    ckv = proj[:, A_KV0:A_QI0]
    ckv = (ckv * lax.rsqrt(jnp.mean(ckv * ckv, axis=-1, keepdims=True) + RMS_EPS)) * kvg_ref[...]
    ckv_ref[0] = ckv.astype(BF16)
    qi_ref[0] = proj[:, A_QI0:A_KI0].astype(BF16)
    ki = proj[:, A_KI0:A_WI0]
    mu = jnp.mean(ki, axis=-1, keepdims=True)
    var = jnp.mean(jnp.square(ki - mu), axis=-1, keepdims=True)
    ki = ((ki - mu) * lax.rsqrt(var + RMS_EPS)) * ikg_ref[...] + ikb_ref[...]
    ki_ref[0] = ki.astype(BF16)
    wit = lax.dot_general(wwi_ref[...], h, _NT, preferred_element_type=F32)
    wit_ref[0] = wit[:IDX_HEADS] * (IDX_HEADS ** -0.5 * IDX_DIM ** -0.5)


def _a_inproj(x, g, sh, sc, w_main, w_wi_t, layer, kv_g, ik_g, ik_b, tm=512):
    bsz, L, d = x.shape
    row = lambda b, i: (b, i, 0)
    per_b = lambda b, i: (b, 0, 0)
    const2 = lambda b, i: (0, 0)
    return pl.pallas_call(
        _a_inproj_kernel,
        grid=(bsz, L // tm),
        in_specs=[pl.BlockSpec((1, tm, d), row),
                  pl.BlockSpec((1, d), const2),
                  pl.BlockSpec((1, 1, d), per_b),
                  pl.BlockSpec((1, 1, d), per_b),
                  _layer_spec(w_main, layer),
                  _layer_spec(w_wi_t, layer),
                  pl.BlockSpec((1, KV_RANK), const2),
                  pl.BlockSpec((1, IDX_DIM), const2),
                  pl.BlockSpec((1, IDX_DIM), const2)],
        out_specs=[pl.BlockSpec((1, tm, HD), row),
                   pl.BlockSpec((1, tm, KV_RANK), row),
                   pl.BlockSpec((1, tm, IDX_HEADS * IDX_DIM), row),
                   pl.BlockSpec((1, tm, IDX_DIM), row),
                   pl.BlockSpec((1, IDX_HEADS, tm), lambda b, i: (b, 0, i))],
        out_shape=[jax.ShapeDtypeStruct((bsz, L, HD), BF16),
                   jax.ShapeDtypeStruct((bsz, L, KV_RANK), BF16),
                   jax.ShapeDtypeStruct((bsz, L, IDX_HEADS * IDX_DIM), BF16),
                   jax.ShapeDtypeStruct((bsz, L, IDX_DIM), BF16),
                   jax.ShapeDtypeStruct((bsz, IDX_HEADS, L), F32)],
        compiler_params=_cparams("parallel", "parallel"),
        name="a_inproj",
    )(x, g, sh, sc, w_main, w_wi_t, kv_g, ik_g, ik_b)


def _a_index_kernel(qi_ref, wit_ref, ki_ref, tri_ref, mask_ref, key_scr, half_scr, *, topk,
                    seq_len):
    j = pl.program_id(1)
    nchunk = seq_len // CHUNK
    qi = qi_ref[0]
    wit = wit_ref[0]
    t_glob = j * CHUNK + lax.broadcasted_iota(I32, (CHUNK, CHUNK), 1)
    s_loc = lax.broadcasted_iota(I32, (CHUNK, CHUNK), 0)

    def chunks(fn, init):
        def pair(i, carry):
            return fn(2 * i + 1, fn(2 * i, carry))
        carry = lax.fori_loop(0, lax.shift_right_logical(j + 1, 1), pair, init)
        return lax.cond((j & 1) == 0, lambda c: fn(j, c), lambda c: c, carry)

    def score_chunk(kc, carry):
        kik = ki_ref[0, pl.ds(pl.multiple_of(kc * CHUNK, CHUNK), CHUNK), :]
        acc = jnp.zeros((CHUNK, CHUNK), F32)
        for h in range(IDX_HEADS):
            r = lax.dot_general(kik, qi[:, h * IDX_DIM:(h + 1) * IDX_DIM], _NT,
                                preferred_element_type=F32)
            acc = acc + jnp.maximum(r, 0.0) * wit[h:h + 1, :]
        sc = jnp.where(kc * CHUNK + s_loc <= t_glob, acc, NEG)
        bits = lax.bitcast_convert_type(sc, I32)
        key = bits ^ ((bits >> 31) & 0x7FFFFFFF)
        key_scr[kc] = key
        half_scr[kc] = (key >> 16).astype(I16)
        return carry

    chunks(score_chunk, 0)

    n_beyond = seq_len - (j + 1) * CHUNK
    ge = lambda a, b: a >= b
    gt = lambda a, b: a > b

    def count(pred, thr):
        def body(kc, acc):
            m = jnp.where(pred(key_scr[kc], thr), jnp.int32(1), jnp.int32(0))
            return acc + m.reshape(CHUNK // 8, 8, CHUNK).sum(axis=0)
        acc = chunks(body, jnp.zeros((8, CHUNK), I32))
        cnt = acc.sum(axis=0, keepdims=True)
        return cnt + jnp.where(pred(KEY_NEG, thr), n_beyond, 0)

    def count_half(pred, thr16):
        t = jnp.broadcast_to(thr16, (HALF_ROWS, CHUNK)).astype(I16)
        def body(kc, acc):
            m = jnp.where(pred(half_scr[kc].reshape(CHUNK // HALF_ROWS, HALF_ROWS, CHUNK), t[None]),
                          jnp.int16(1), jnp.int16(0))
            for r in range(CHUNK // HALF_ROWS):
                acc = acc + m[r]
            return acc
        acc = chunks(body, jnp.zeros((HALF_ROWS, CHUNK), I16))
        return acc.astype(I32).sum(axis=0, keepdims=True)

    def beyond(cand):
        return jnp.where(KEY_NEG >= cand, n_beyond, 0)

    zero = jnp.zeros((1, CHUNK), I32)
    thr = jnp.where(count_half(ge, zero) + beyond(zero) >= topk, zero,
                    jnp.full((1, CHUNK), INT_MIN, I32))

    def bisect_high(i, thr):
        cand = thr + lax.shift_left(jnp.int32(1), 30 - i)
        cnt = count_half(ge, cand >> 16) + beyond(cand)
        return jnp.where(cnt >= topk, cand, thr)

    thr = lax.fori_loop(0, 15, bisect_high, thr)

    thr_hi = thr >> 16
    n_above = count_half(gt, thr_hi)

    def low_half(kc, carry):
        key = key_scr[kc]
        low = (key & 0xFFFF) - 0x8000
        half_scr[kc] = jnp.where((key >> 16) == thr_hi, low, -0x8000).astype(I16)
        return carry

    chunks(low_half, 0)

    def bisect_low(i, thr):
        cand = thr + lax.shift_left(jnp.int32(1), 15 - i)
        cnt = n_above + count_half(ge, (cand & 0xFFFF) - 0x8000) + beyond(cand)
        return jnp.where(cnt >= topk, cand, thr)

    thr = lax.fori_loop(0, 16, bisect_low, thr)

    need = (topk - count(gt, thr)).astype(F32)
    tri = tri_ref[...]

    def emit(kc, seen):
        key = key_scr[kc]
        eq = jnp.where(key == thr, 1.0, 0.0)
        rank = jnp.dot(tri, eq.astype(BF16), preferred_element_type=F32) + seen
        take = jnp.where(key > thr, 1.0, jnp.where(rank < need, eq, 0.0))
        causal = kc * CHUNK + s_loc <= t_glob
        m = jnp.where(causal, jnp.where(take > 0.5, 0.0, NEG), NEG)
        mask_ref[0, kc] = m.T
        return seen + jnp.sum(eq, axis=0, keepdims=True)

    chunks(emit, jnp.zeros((1, CHUNK), F32))

    def fill(kc, carry):
        mask_ref[0, kc] = jnp.full((CHUNK, CHUNK), NEG, F32)
        return carry

    lax.fori_loop(j + 1, nchunk, fill, 0)


def _a_index(qi, wit, ki, topk):
    bsz, L, _ = qi.shape
    nchunk = L // CHUNK
    tri = jnp.tril(jnp.ones((CHUNK, CHUNK), BF16), -1)
    return pl.pallas_call(
        functools.partial(_a_index_kernel, topk=topk, seq_len=L),
        grid=(bsz, nchunk),
        in_specs=[pl.BlockSpec((1, CHUNK, IDX_HEADS * IDX_DIM), lambda b, j: (b, j, 0)),
                  pl.BlockSpec((1, IDX_HEADS, CHUNK), lambda b, j: (b, 0, j)),
                  pl.BlockSpec((1, L, IDX_DIM), lambda b, j: (b, 0, 0)),
                  pl.BlockSpec((CHUNK, CHUNK), lambda b, j: (0, 0))],
        out_specs=pl.BlockSpec((1, nchunk, CHUNK, CHUNK), lambda b, j: (b, 0, j, 0)),
        out_shape=jax.ShapeDtypeStruct((bsz, nchunk, L, CHUNK), F32),
        scratch_shapes=[pltpu.VMEM((nchunk, CHUNK, CHUNK), I32),
                        pltpu.VMEM((nchunk, CHUNK, CHUNK), I16)],
        compiler_params=_cparams("parallel", "parallel"),
        name="a_index",
    )(qi, wit, ki, tri)


def _a_attn_kernel(q_ref, ckv_ref, mask_ref, wuk_ref, wuv_ref, bias_ref, o_ref,
                   qabs_scr, lg_scr, m_scr, l_scr, acc_scr, p_scr, *, hc, nq, n_work):
    s = pl.program_id(0)
    j = (jnp.minimum(s, n_work - 1) // (N_HEADS // hc)) % nq
    slot = s & 1
    tq = CHUNK

    def placeholder(r, carry):
        rows = pl.ds(pl.multiple_of(r * ACC_ROWS, ACC_ROWS), ACC_ROWS)
        l_scr[1, rows, :] = jnp.ones((ACC_ROWS, LANES), F32)
        acc_scr[1, rows, :] = jnp.zeros((ACC_ROWS, KV_RANK), F32)
        return carry

    lax.fori_loop(0, jnp.where(s == 0, hc * tq // ACC_ROWS, 0), placeholder, 0)

    q = q_ref[0]
    for i in range(hc // 2):
        qa = lax.dot_general(q[:, i * 2 * HEAD_DIM:(i + 1) * 2 * HEAD_DIM], wuk_ref[i], _NT,
                             preferred_element_type=F32)
        for par in range(2):
            qabs_scr[(2 * i + par) * tq:(2 * i + par + 1) * tq, :] = (
                qa[:, par * KV_RANK:(par + 1) * KV_RANK] * LOG2E).astype(BF16)
    o = (acc_scr[1 - slot] / jnp.sum(l_scr[1 - slot], axis=-1, keepdims=True)).astype(BF16)
    outs = [jnp.dot(jnp.concatenate([o[2 * i * tq:(2 * i + 1) * tq],
                                     o[(2 * i + 1) * tq:(2 * i + 2) * tq]], axis=1),
                    wuv_ref[i], preferred_element_type=F32) for i in range(hc // 2)]
    o_ref[0] = jnp.concatenate(outs, axis=1).astype(BF16)
    qg = qabs_scr[...]

    n_far_pairs = lax.shift_right_arithmetic(j - 1, 1)
    odd = (j & 1) == 0

    def keys(kc, width):
        return ckv_ref[0, pl.ds(kc, width)].reshape(width * CHUNK, KV_RANK)

    def logits(kc, width):
        lg = lax.dot_general(qg, keys(kc, width), _NT, preferred_element_type=F32)
        mk = jnp.concatenate([mask_ref[0, kc + w] for w in range(width)], axis=1)
        return lg + jnp.concatenate([mk] * hc, axis=0)

    def near_bias(i, with_prev):
        d0, d1 = bias_ref[i, 0], bias_ref[i, 1]
        z = jnp.zeros_like(d0)
        top, bot = [d0, z], [d1, d0]
        if with_prev:
            top, bot = [z, d1] + top, [z, z] + bot
        return jnp.concatenate([jnp.concatenate(top, axis=1), jnp.concatenate(bot, axis=1)], axis=0)

    def put_logits(kc, width, lg):
        for w in range(width):
            lg_scr[kc + w] = lg[:, w * CHUNK:(w + 1) * CHUNK]
        m = m_scr[...]
        for c in range(width * CHUNK // LANES):
            m = jnp.maximum(m, lg[:, c * LANES:(c + 1) * LANES])
        m_scr[...] = m

    m_scr[...] = jnp.full(m_scr.shape, -jnp.inf, F32)

    @pl.when(j >= 1)
    def _():
        bias = jnp.concatenate([near_bias(i, True) for i in range(hc)], axis=0)
        put_logits(j - 1, 2, logits(j - 1, 2) + bias)

    def far_pair(i, carry):
        kc = j - 3 - 2 * i
        put_logits(kc, 2, logits(kc, 2))
        return carry

    lax.fori_loop(0, n_far_pairs, far_pair, 0)

    @pl.when(odd)
    def _():
        bias = jnp.concatenate([near_bias(i, False) for i in range(hc)], axis=0)
        put_logits(0, 1, logits(0, 1) + jnp.where(j == 0, bias, 0.0))

    l_scr[slot] = jnp.zeros(l_scr.shape[1:], F32)
    acc_scr[slot] = jnp.zeros(acc_scr.shape[1:], F32)

    def pv(kc, width, reduce_max):
        ck = keys(kc, width)
        for piece in range(hc * tq // ACC_ROWS):
            for r in range(piece * ACC_ROWS // PV_ROWS, (piece + 1) * ACC_ROWS // PV_ROWS):
                rows = slice(r * PV_ROWS, (r + 1) * PV_ROWS)
                m = m_scr[rows, :]
                if reduce_max:
                    m = jnp.broadcast_to(jnp.max(m, axis=-1, keepdims=True), m.shape)
                    m_scr[rows, :] = m
                l = l_scr[slot, rows, :]
                for w in range(width):
                    for c in range(CHUNK // LANES):
                        col = w * CHUNK + c * LANES
                        p = jnp.exp2(lg_scr[kc + w, rows, c * LANES:(c + 1) * LANES] - m)
                        l = l + p
                        p_scr[rows, col:col + LANES] = p.astype(BF16)
                l_scr[slot, rows, :] = l
            rows = slice(piece * ACC_ROWS, (piece + 1) * ACC_ROWS)
            acc_scr[slot, rows, :] += jnp.dot(p_scr[rows, :width * CHUNK], ck,
                                              preferred_element_type=F32)

    @pl.when(j >= 1)
    def _():
        pv(j - 1, 2, True)

    def far_pv(i, carry):
        pv(j - 3 - 2 * i, 2, False)
        return carry

    lax.fori_loop(0, n_far_pairs, far_pv, 0)

    @pl.when(odd)
    def _():
        pv(0, 1, True)


def _a_attn(q, ckv, mask, w_uk, w_uv, layer, bias_nd, hc=8):
    bsz, L, _ = q.shape
    nchunk = L // CHUNK
    hg = N_HEADS // hc
    m = hc * CHUNK
    ckv4 = ckv.reshape(bsz, nchunk, CHUNK, KV_RANK)
    n_work = bsz * nchunk * hg

    def item(s):
        return s // (nchunk * hg), (s // hg) % nchunk, s % hg

    cur = lambda s: item(jnp.minimum(s, n_work - 1))
    prev = lambda s: item(jnp.maximum(s - 1, 0))
    return pl.pallas_call(
        functools.partial(_a_attn_kernel, hc=hc, nq=nchunk, n_work=n_work),
        grid=(n_work + 1,),
        in_specs=[pl.BlockSpec((1, CHUNK, hc * HEAD_DIM), lambda s: cur(s)),
                  pl.BlockSpec((1, nchunk, CHUNK, KV_RANK), lambda s: (cur(s)[0], 0, 0, 0)),
                  pl.BlockSpec((1, nchunk, CHUNK, CHUNK), lambda s: (cur(s)[0], 0, cur(s)[1], 0)),
                  pl.BlockSpec((None, hc // 2) + w_uk.shape[2:], lambda s: (layer, cur(s)[2], 0, 0)),
                  pl.BlockSpec((None, hc // 2) + w_uv.shape[2:], lambda s: (layer, prev(s)[2], 0, 0)),
                  pl.BlockSpec((hc, 2, BIAS_BLK, BIAS_BLK), lambda s: (cur(s)[2], 0, 0, 0))],
        out_specs=pl.BlockSpec((1, CHUNK, hc * HEAD_DIM), lambda s: prev(s)),
        out_shape=jax.ShapeDtypeStruct((bsz, L, HD), BF16),
        scratch_shapes=[pltpu.VMEM((m, KV_RANK), BF16),
                        pltpu.VMEM((nchunk, m, CHUNK), F32),
                        pltpu.VMEM((m, LANES), F32),
                        pltpu.VMEM((2, m, LANES), F32),
                        pltpu.VMEM((2, m, KV_RANK), F32),
                        pltpu.VMEM((m, 2 * CHUNK), BF16)],
        compiler_params=_cparams("arbitrary"),
        name="a_attn",
    )(q, ckv4, mask, w_uk, w_uv, bias_nd)


def _bias_kernel(rb_ref, ids_ref, o_ref, *, n_tiles, far_bucket, shift_far):
    h = pl.program_id(0)
    far = rb_ref[far_bucket, h] if shift_far else 0.0
    for t in range(n_tiles):
        ids = ids_ref[t]
        out = jnp.zeros(ids.shape, F32)
        for b in range(N_BUCKETS):
            out = jnp.where(ids == b, rb_ref[b, h] - far, out)
        o_ref[0, t] = jnp.where(ids < 0, NEG, out * LOG2E)


def _t5_bucket(dist):
    max_exact = N_BUCKETS // 2
    d = jnp.maximum(dist, 0)
    large = max_exact + (jnp.log(jnp.maximum(d, 1).astype(F32) / max_exact)
                         / math.log(MAX_DISTANCE / max_exact)
                         * (N_BUCKETS - max_exact)).astype(I32)
    large = jnp.minimum(large, N_BUCKETS - 1)
    return jnp.where(d < max_exact, d, large)


def _bias_tiles(rel_bias, ids, shift_far):
    n_tiles, r, c = ids.shape
    return pl.pallas_call(
        functools.partial(_bias_kernel, n_tiles=n_tiles, far_bucket=N_BUCKETS - 1,
                          shift_far=shift_far),
        grid=(N_HEADS,),
        in_specs=[pl.BlockSpec(memory_space=pltpu.SMEM),
                  pl.BlockSpec((n_tiles, r, c), lambda h: (0, 0, 0))],
        out_specs=pl.BlockSpec((1, n_tiles, r, c), lambda h: (h, 0, 0, 0)),
        out_shape=jax.ShapeDtypeStruct((N_HEADS, n_tiles, r, c), F32),
        compiler_params=_cparams("parallel"),
        name="bias_tiles",
    )(rel_bias, ids)


def _b_inproj_kernel(x_ref, g_ref, sh_ref, sc_ref, w_ref, b_ref, q_ref, k_ref, v_ref):
    h = _normmod(x_ref[0], g_ref[...], sh_ref[0], sc_ref[0]).astype(BF16)
    proj = jnp.dot(h, w_ref[...], preferred_element_type=F32) + b_ref[...]
    q_ref[0] = (proj[:, :HD] * (HEAD_DIM ** -0.5 * LOG2E)).astype(BF16)
    k_ref[0] = proj[:, HD:HD + KV_HEADS * HEAD_DIM].astype(BF16)
    v_ref[0] = proj[:, HD + KV_HEADS * HEAD_DIM:].astype(BF16)


def _b_inproj(x, g, sh, sc, w, layer, b, tm=512):
    bsz, L, d = x.shape
    kvw = KV_HEADS * HEAD_DIM
    row = lambda b_, i: (b_, i, 0)
    per_b = lambda b_, i: (b_, 0, 0)
    const2 = lambda b_, i: (0, 0)
    return pl.pallas_call(
        _b_inproj_kernel,
        grid=(bsz, L // tm),
        in_specs=[pl.BlockSpec((1, tm, d), row),
                  pl.BlockSpec((1, d), const2),
                  pl.BlockSpec((1, 1, d), per_b),
                  pl.BlockSpec((1, 1, d), per_b),
                  _layer_spec(w, layer),
                  pl.BlockSpec((1, B_IN), const2)],
        out_specs=[pl.BlockSpec((1, tm, HD), row),
                   pl.BlockSpec((1, tm, kvw), row),
                   pl.BlockSpec((1, tm, kvw), row)],
        out_shape=[jax.ShapeDtypeStruct((bsz, L, HD), BF16),
                   jax.ShapeDtypeStruct((bsz, L, kvw), BF16),
                   jax.ShapeDtypeStruct((bsz, L, kvw), BF16)],
        compiler_params=_cparams("parallel", "parallel"),
        name="b_inproj",
    )(x, g, sh, sc, w, b)


B_BLOCKS = 8


def _b_attn_kernel(sink_ref, q_ref, kp_ref, kc_ref, vp_ref, vc_ref, bias_ref, o_ref):
    n = pl.program_id(1)
    w = WINDOW
    pairs = N_HEADS // KV_HEADS // 2
    kall = jnp.concatenate([kp_ref[0], kc_ref[0]], axis=0).astype(F32)
    vall = jnp.concatenate([vp_ref[0], vc_ref[0]], axis=0).astype(F32)
    low = lax.broadcasted_iota(I32, kall.shape, 1) < HEAD_DIM

    def padded(x, g):
        swapped = pltpu.roll(x, HEAD_DIM, axis=1)
        on_low, on_high = (x, swapped) if g == 0 else (swapped, x)
        return (jnp.where(low, on_low, 0.0).astype(BF16), jnp.where(low, 0.0, on_high).astype(BF16))

    kpad = [padded(kall, g) for g in range(KV_HEADS)]
    vpad = [padded(vall, g) for g in range(KV_HEADS)]
    for blk in range(B_BLOCKS):
        variant = jnp.where(n > 0, 0, 1) if blk == 0 else 0
        keys = slice(blk * w, (blk + 2) * w)
        q = q_ref[0, blk * w:(blk + 1) * w, :]
        outs = []
        for g in range(KV_HEADS):
            blocks = [q[:, (pairs * g + p) * 2 * HEAD_DIM:(pairs * g + p + 1) * 2 * HEAD_DIM]
                      for p in range(pairs)]
            lg = lax.dot_general(jnp.concatenate(blocks, axis=0),
                                 jnp.concatenate([kpad[g][0][keys], kpad[g][1][keys]], axis=0),
                                 _NT, preferred_element_type=F32)
            lg = lg + bias_ref[variant, g]
            probs = [[], []]
            for p in range(pairs):
                for par in range(2):
                    h = 2 * pairs * g + 2 * p + par
                    t = lg[p * w:(p + 1) * w, par * 2 * w:(par + 1) * 2 * w]
                    sink = sink_ref[0, h] * LOG2E
                    m = jnp.maximum(jnp.max(t, axis=-1, keepdims=True), sink)
                    e = jnp.exp2(t - m)
                    denom = jnp.sum(e, axis=-1, keepdims=True) + jnp.exp2(sink - m)
                    probs[par].append((e * (1.0 / denom)).astype(BF16))
            og = (jnp.dot(jnp.concatenate(probs[0], axis=0), vpad[g][0][keys],
                          preferred_element_type=F32)
                  + jnp.dot(jnp.concatenate(probs[1], axis=0), vpad[g][1][keys],
                            preferred_element_type=F32))
            outs += [og[p * w:(p + 1) * w] for p in range(pairs)]
        o_ref[0, blk * w:(blk + 1) * w, :] = jnp.concatenate(outs, axis=1).astype(BF16)


def _b_attn(q, k, v, sinks, bias_b):
    bsz, L, _ = q.shape
    w = WINDOW
    wide = B_BLOCKS * w
    assert L % wide == 0, (L, wide)
    kvw = KV_HEADS * HEAD_DIM
    cur = lambda b, n: (b, n, 0)
    prev = lambda b, n: (b, jnp.maximum(n * B_BLOCKS - 1, 0), 0)
    return pl.pallas_call(
        _b_attn_kernel,
        grid=(bsz, L // wide),
        in_specs=[pl.BlockSpec(memory_space=pltpu.SMEM),
                  pl.BlockSpec((1, wide, HD), cur),
                  pl.BlockSpec((1, w, kvw), prev),
                  pl.BlockSpec((1, wide, kvw), cur),
                  pl.BlockSpec((1, w, kvw), prev),
                  pl.BlockSpec((1, wide, kvw), cur),
                  pl.BlockSpec(bias_b.shape, lambda b, n: (0, 0, 0, 0))],
        out_specs=pl.BlockSpec((1, wide, HD), cur),
        out_shape=jax.ShapeDtypeStruct((bsz, L, HD), BF16),
        compiler_params=_cparams("parallel", "parallel"),
        name="b_attn",
    )(sinks.reshape(1, N_HEADS), q, k, k, v, v, bias_b)


FF_CHUNK = 256


def _out_ffn_kernel(o_ref, wo_ref, bo_ref, g1_ref, x_ref, g_ref, sh_ref, sc_ref, gate_ref,
                    w1_ref, w3_ref, w2_ref, fg_ref, y_ref, *, final_norm):
    mix = jnp.dot(o_ref[0], wo_ref[...], preferred_element_type=F32) + bo_ref[...]
    x = x_ref[0] + g1_ref[0] * mix
    h = _normmod(x, g_ref[...], sh_ref[0], sc_ref[0]).astype(BF16)
    acc = jnp.zeros(x.shape, F32)
    for c in range(D_FF // FF_CHUNK):
        cs = slice(c * FF_CHUNK, (c + 1) * FF_CHUNK)
        a1 = jnp.dot(h, w1_ref[:, cs], preferred_element_type=F32)
        a3 = jnp.dot(h, w3_ref[:, cs], preferred_element_type=F32)
        act = (a1 * jax.nn.sigmoid(a1)) * a3
        acc = acc + jnp.dot(act.astype(BF16), w2_ref[cs, :], preferred_element_type=F32)
    y = x + gate_ref[0] * acc
    if final_norm:
        y = (y * lax.rsqrt(jnp.mean(y * y, axis=-1, keepdims=True) + RMS_EPS)) * fg_ref[...]
    y_ref[0] = y


def _out_ffn(o, w_out, mix_layer, b_out, g1, x, g, sh, sc, gate, w1, w3, w2, layer, final_g,
             final_norm, tm=512):
    bsz, L, d = x.shape
    row = lambda b_, i: (b_, i, 0)
    per_b = lambda b_, i: (b_, 0, 0)
    const2 = lambda b_, i: (0, 0)
    return pl.pallas_call(
        functools.partial(_out_ffn_kernel, final_norm=final_norm),
        grid=(bsz, L // tm),
        in_specs=[pl.BlockSpec((1, tm, HD), row),
                  _layer_spec(w_out, mix_layer),
                  pl.BlockSpec((1, d), const2),
                  pl.BlockSpec((1, 1, d), per_b),
                  pl.BlockSpec((1, tm, d), row),
                  pl.BlockSpec((1, d), const2),
                  pl.BlockSpec((1, 1, d), per_b),
                  pl.BlockSpec((1, 1, d), per_b),
                  pl.BlockSpec((1, 1, d), per_b),
                  _layer_spec(w1, layer),
                  _layer_spec(w3, layer),
                  _layer_spec(w2, layer),
                  pl.BlockSpec((1, d), const2)],
        out_specs=pl.BlockSpec((1, tm, d), row),
        out_shape=jax.ShapeDtypeStruct((bsz, L, d), F32),
        compiler_params=_cparams("parallel", "parallel"),
        name="out_ffn",
    )(o, w_out, b_out, g1, x, g, sh, sc, gate, w1, w3, w2, final_g)


def kernel(x, c, rel_bias, w_ada, b_ada, norm_mix_g, norm_ffn_g, a_w_in, a_kv_norm_g, a_w_uk,
           a_w_uv, a_idx_k_g, a_idx_k_b, a_w_out, b_w_in, b_b_in, b_sinks, b_w_out, b_b_out,
           ffn_w1, ffn_w3, ffn_w2, norm_final_g):
    bsz, L, d = x.shape
    depth = w_ada.shape[0]
    topk = min(INDEX_TOPK, L // 4)

    mod = _adaln(c, w_ada, b_ada)

    r = jnp.arange(BIAS_BLK)
    ids_a = jnp.stack([_t5_bucket(r[:, None] - r[None, :]),
                       _t5_bucket(r[:, None] - r[None, :] + BIAS_BLK)])
    bias_a = _bias_tiles(rel_bias, ids_a, shift_far=True)
    dist_b = jnp.arange(WINDOW)[:, None] + WINDOW - jnp.arange(2 * WINDOW)[None, :]
    in_window = (dist_b >= 0) & (dist_b < WINDOW)
    ids_b = jnp.where(in_window, _t5_bucket(dist_b), -1)
    ids_b0 = jnp.where(jnp.arange(2 * WINDOW)[None, :] >= WINDOW, ids_b, -1)
    bias_b = _bias_tiles(rel_bias, jnp.stack([ids_b, ids_b0]), shift_far=False)
    pairs = N_HEADS // KV_HEADS // 2
    bias_b = bias_b.reshape(KV_HEADS, pairs, 2, 2, WINDOW, 2 * WINDOW).transpose(3, 0, 1, 4, 2, 5)
    bias_b = bias_b.reshape(2, KV_HEADS, pairs * WINDOW, 4 * WINDOW)

    a_w_main = jnp.pad(a_w_in[:, :, :A_MAIN], ((0, 0), (0, 0), (0, A_MAIN_PAD - A_MAIN))).astype(BF16)
    a_w_wi_t = jnp.pad(a_w_in[:, :, A_WI0:A_WI0 + IDX_HEADS].transpose(0, 2, 1),
                       ((0, 0), (0, 16 - IDX_HEADS), (0, 0))).astype(BF16)
    a_w_uk, a_w_uv, a_w_out, b_w_in, b_w_out, ffn_w1, ffn_w3, ffn_w2 = [
        w.astype(BF16) for w in (_pair_blocks(a_w_uk), _pair_blocks(a_w_uv), a_w_out, b_w_in,
                                 b_w_out, ffn_w1, ffn_w3, ffn_w2)]

    zero_bias = jnp.zeros((1, d), F32)
    for i in range(depth):
        sh1, sc1, g1, sh2, sc2, g2 = [m.reshape(bsz, 1, d) for m in jnp.split(mod[i], 6, axis=-1)]
        jm = i // 2
        if i % 2 == 0:
            q, ckv, qi, ki, wit = _a_inproj(
                x, norm_mix_g[i][None], sh1, sc1, a_w_main, a_w_wi_t, jm, a_kv_norm_g[jm][None],
                a_idx_k_g[jm][None], a_idx_k_b[jm][None])
            mask = _a_index(qi, wit, ki, topk)
            o = _a_attn(q, ckv, mask, a_w_uk, a_w_uv, jm, bias_a)
            w_out, b_out = a_w_out, zero_bias
        else:
            q, k, v = _b_inproj(x, norm_mix_g[i][None], sh1, sc1, b_w_in, jm, b_b_in[jm][None])
            o = _b_attn(q, k, v, b_sinks[jm], bias_b)
            w_out, b_out = b_w_out, b_b_out[jm][None]
        x = _out_ffn(o, w_out, jm, b_out, g1, x, norm_ffn_g[i][None], sh2, sc2, g2,
                     ffn_w1, ffn_w3, ffn_w2, i, norm_final_g[None], final_norm=(i == depth - 1))
    return x
```

```python
import functools
import math

import numpy as np
import jax
import jax.numpy as jnp
from jax import lax
from jax.experimental import pallas as pl
from jax.experimental.pallas import tpu as pltpu

D_MODEL = 1024
N_HEADS = 16
HEAD_DIM = 64
KV_RANK = 256
IDX_HEADS = 8
IDX_DIM = 64
INDEX_TOPK = 256
KV_HEADS = 2
WINDOW = 128
N_BUCKETS = 32
MAX_DISTANCE = 128
D_FF = 2816
RMS_EPS = 1e-6
NEG = -1e30

HD = N_HEADS * HEAD_DIM
A_Q0, A_KV0, A_QI0, A_KI0, A_WI0 = 0, HD, HD + KV_RANK, HD + KV_RANK + IDX_HEADS * IDX_DIM, \
    HD + KV_RANK + IDX_HEADS * IDX_DIM + IDX_DIM
A_MAIN = A_WI0
A_MAIN_PAD = 1920
B_IN = (N_HEADS + 2 * KV_HEADS) * HEAD_DIM

CHUNK = 256
PV_ROWS = 64
ACC_ROWS = 256
HALF_ROWS = 16
LANES = 128
BIAS_BLK = 128
VMEM_LIMIT = 56 * 1024 * 1024

F32 = jnp.float32
BF16 = jnp.bfloat16
I32 = jnp.int32
I16 = jnp.int16

_NT = (((1,), (1,)), ((), ()))


def _cparams(*sem):
    return pltpu.CompilerParams(dimension_semantics=sem, vmem_limit_bytes=VMEM_LIMIT)


def _pair_blocks(w):
    even, odd = w[..., 0::2, :, :], w[..., 1::2, :, :]
    zero = jnp.zeros_like(even)
    return jnp.concatenate([jnp.concatenate([even, zero], axis=-1),
                            jnp.concatenate([zero, odd], axis=-1)], axis=-2)


def _layer_spec(stack, layer):
    zeros = (0,) * (stack.ndim - 1)
    return pl.BlockSpec((None,) + stack.shape[1:], lambda *_: (layer,) + zeros)


def _f32_key(v):
    b = int(np.array(v, np.float32).view(np.int32))
    return b ^ ((b >> 31) & 0x7FFFFFFF)


LOG2E = math.log2(math.e)
KEY_NEG = _f32_key(NEG)
INT_MIN = -(2 ** 31)


def _adaln_kernel(c_ref, w_ref, b_ref, o_ref):
    c = c_ref[...]
    cs = c * jax.nn.sigmoid(c)
    o_ref[0] = jnp.dot(cs, w_ref[0], preferred_element_type=F32,
                       precision=lax.Precision.HIGHEST) + b_ref[0]


def _adaln(c, w_ada, b_ada):
    depth, d, n = w_ada.shape
    bsz = c.shape[0]
    tn = 1536
    return pl.pallas_call(
        _adaln_kernel,
        grid=(depth, n // tn),
        in_specs=[pl.BlockSpec((bsz, d), lambda i, j: (0, 0)),
                  pl.BlockSpec((1, d, tn), lambda i, j: (i, 0, j)),
                  pl.BlockSpec((1, 1, tn), lambda i, j: (i, 0, j))],
        out_specs=pl.BlockSpec((1, bsz, tn), lambda i, j: (i, 0, j)),
        out_shape=jax.ShapeDtypeStruct((depth, bsz, n), F32),
        compiler_params=_cparams("parallel", "parallel"),
        name="adaln",
    )(c, w_ada, b_ada.reshape(depth, 1, n))


def _normmod(x, g, sh, sc):
    ms = jnp.mean(x * x, axis=-1, keepdims=True)
    y = (x * lax.rsqrt(ms + RMS_EPS)) * g
    return y * (1.0 + sc) + sh


def _a_inproj_kernel(x_ref, g_ref, sh_ref, sc_ref, w_ref, wwi_ref, kvg_ref, ikg_ref, ikb_ref,
                     q_ref, ckv_ref, ckvt_ref, qi_ref, ki_ref, wit_ref):
    h = _normmod(x_ref[0], g_ref[...], sh_ref[0], sc_ref[0]).astype(BF16)
    proj = jnp.dot(h, w_ref[...], preferred_element_type=F32)
    q_ref[0] = (proj[:, A_Q0:A_KV0] * (HEAD_DIM ** -0.5)).astype(BF16)
    ckv = proj[:, A_KV0:A_QI0]
    ckv = (ckv * lax.rsqrt(jnp.mean(ckv * ckv, axis=-1, keepdims=True) + RMS_EPS)) * kvg_ref[...]
    ckv_ref[0] = ckv.astype(BF16)
    for c in range(ckvt_ref.shape[1]):
        ckvt_ref[0, c] = ckv[c * CHUNK:(c + 1) * CHUNK].T.astype(BF16)
    qi_ref[0] = proj[:, A_QI0:A_KI0].astype(BF16)
    ki = proj[:, A_KI0:A_WI0]
    mu = jnp.mean(ki, axis=-1, keepdims=True)
    var = jnp.mean(jnp.square(ki - mu), axis=-1, keepdims=True)
    ki = ((ki - mu) * lax.rsqrt(var + RMS_EPS)) * ikg_ref[...] + ikb_ref[...]
    ki_ref[0] = ki.astype(BF16)
    wit = lax.dot_general(wwi_ref[...], h, _NT, preferred_element_type=F32)
    wit_ref[0] = wit[:IDX_HEADS] * (IDX_HEADS ** -0.5 * IDX_DIM ** -0.5)


def _a_inproj(x, g, sh, sc, w_main, w_wi_t, layer, kv_g, ik_g, ik_b, tm=512):
    bsz, L, d = x.shape
    row = lambda b, i: (b, i, 0)
    per_b = lambda b, i: (b, 0, 0)
    const2 = lambda b, i: (0, 0)
    return pl.pallas_call(
        _a_inproj_kernel,
        grid=(bsz, L // tm),
        in_specs=[pl.BlockSpec((1, tm, d), row),
                  pl.BlockSpec((1, d), const2),
                  pl.BlockSpec((1, 1, d), per_b),
                  pl.BlockSpec((1, 1, d), per_b),
                  _layer_spec(w_main, layer),
                  _layer_spec(w_wi_t, layer),
                  pl.BlockSpec((1, KV_RANK), const2),
                  pl.BlockSpec((1, IDX_DIM), const2),
                  pl.BlockSpec((1, IDX_DIM), const2)],
        out_specs=[pl.BlockSpec((1, tm, HD), row),
                   pl.BlockSpec((1, tm, KV_RANK), row),
                   pl.BlockSpec((1, tm // CHUNK, KV_RANK, CHUNK), lambda b, i: (b, i, 0, 0)),
                   pl.BlockSpec((1, tm, IDX_HEADS * IDX_DIM), row),
                   pl.BlockSpec((1, tm, IDX_DIM), row),
                   pl.BlockSpec((1, IDX_HEADS, tm), lambda b, i: (b, 0, i))],
        out_shape=[jax.ShapeDtypeStruct((bsz, L, HD), BF16),
                   jax.ShapeDtypeStruct((bsz, L, KV_RANK), BF16),
                   jax.ShapeDtypeStruct((bsz, L // CHUNK, KV_RANK, CHUNK), BF16),
                   jax.ShapeDtypeStruct((bsz, L, IDX_HEADS * IDX_DIM), BF16),
                   jax.ShapeDtypeStruct((bsz, L, IDX_DIM), BF16),
                   jax.ShapeDtypeStruct((bsz, IDX_HEADS, L), F32)],
        compiler_params=_cparams("parallel", "parallel"),
        name="a_inproj",
    )(x, g, sh, sc, w_main, w_wi_t, kv_g, ik_g, ik_b)


def _a_index_kernel(qi_ref, wit_ref, ki_ref, tri_ref, mask_ref, key_scr, half_scr, *, topk,
                    seq_len):
    j = pl.program_id(1)
    nchunk = seq_len // CHUNK
    qi = qi_ref[0]
    wit = wit_ref[0]
    t_glob = j * CHUNK + lax.broadcasted_iota(I32, (CHUNK, CHUNK), 1)
    s_loc = lax.broadcasted_iota(I32, (CHUNK, CHUNK), 0)

    def chunks(fn, init):
        def pair(i, carry):
            return fn(2 * i + 1, fn(2 * i, carry))
        carry = lax.fori_loop(0, lax.shift_right_logical(j + 1, 1), pair, init)
        return lax.cond((j & 1) == 0, lambda c: fn(j, c), lambda c: c, carry)

    def score_chunk(kc, carry):
        kik = ki_ref[0, pl.ds(pl.multiple_of(kc * CHUNK, CHUNK), CHUNK), :]
        acc = jnp.zeros((CHUNK, CHUNK), F32)
        for h in range(IDX_HEADS):
            r = lax.dot_general(kik, qi[:, h * IDX_DIM:(h + 1) * IDX_DIM], _NT,
                                preferred_element_type=F32)
            acc = acc + jnp.maximum(r, 0.0) * wit[h:h + 1, :]
        sc = jnp.where(kc * CHUNK + s_loc <= t_glob, acc, NEG)
        bits = lax.bitcast_convert_type(sc, I32)
        key = bits ^ ((bits >> 31) & 0x7FFFFFFF)
        key_scr[kc] = key
        half_scr[kc] = (key >> 16).astype(I16)
        return carry

    chunks(score_chunk, 0)

    n_beyond = seq_len - (j + 1) * CHUNK
    ge = lambda a, b: a >= b
    gt = lambda a, b: a > b

    def count(pred, thr):
        def body(kc, acc):
            m = jnp.where(pred(key_scr[kc], thr), jnp.int32(1), jnp.int32(0))
            return acc + m.reshape(CHUNK // 8, 8, CHUNK).sum(axis=0)
        acc = chunks(body, jnp.zeros((8, CHUNK), I32))
        cnt = acc.sum(axis=0, keepdims=True)
        return cnt + jnp.where(pred(KEY_NEG, thr), n_beyond, 0)

    def count_half(pred, thr16):
        t = jnp.broadcast_to(thr16, (HALF_ROWS, CHUNK)).astype(I16)
        def body(kc, acc):
            m = jnp.where(pred(half_scr[kc].reshape(CHUNK // HALF_ROWS, HALF_ROWS, CHUNK), t[None]),
                          jnp.int16(1), jnp.int16(0))
            for r in range(CHUNK // HALF_ROWS):
                acc = acc + m[r]
            return acc
        acc = chunks(body, jnp.zeros((HALF_ROWS, CHUNK), I16))
        return acc.astype(I32).sum(axis=0, keepdims=True)

    def beyond(cand):
        return jnp.where(KEY_NEG >= cand, n_beyond, 0)

    zero = jnp.zeros((1, CHUNK), I32)
    thr = jnp.where(count_half(ge, zero) + beyond(zero) >= topk, zero,
                    jnp.full((1, CHUNK), INT_MIN, I32))

    def bisect_high(i, thr):
        cand = thr + lax.shift_left(jnp.int32(1), 30 - i)
        cnt = count_half(ge, cand >> 16) + beyond(cand)
        return jnp.where(cnt >= topk, cand, thr)

    thr = lax.fori_loop(0, 15, bisect_high, thr)

    thr_hi = thr >> 16
    n_above = count_half(gt, thr_hi)

    def low_half(kc, carry):
        key = key_scr[kc]
        low = (key & 0xFFFF) - 0x8000
        half_scr[kc] = jnp.where((key >> 16) == thr_hi, low, -0x8000).astype(I16)
        return carry

    chunks(low_half, 0)

    def bisect_low(i, thr):
        cand = thr + lax.shift_left(jnp.int32(1), 15 - i)
        cnt = n_above + count_half(ge, (cand & 0xFFFF) - 0x8000) + beyond(cand)
        return jnp.where(cnt >= topk, cand, thr)

    thr = lax.fori_loop(0, 16, bisect_low, thr)

    need = (topk - count(gt, thr)).astype(F32)
    tri = tri_ref[...]

    def emit(kc, seen):
        key = key_scr[kc]
        eq = jnp.where(key == thr, 1.0, 0.0)
        rank = jnp.dot(tri, eq.astype(BF16), preferred_element_type=F32) + seen
        take = jnp.where(key > thr, 1.0, jnp.where(rank < need, eq, 0.0))
        causal = kc * CHUNK + s_loc <= t_glob
        m = jnp.where(causal, jnp.where(take > 0.5, 0.0, NEG), NEG)
        mask_ref[0, kc] = m.T
        return seen + jnp.sum(eq, axis=0, keepdims=True)

    chunks(emit, jnp.zeros((1, CHUNK), F32))

    def fill(kc, carry):
        mask_ref[0, kc] = jnp.full((CHUNK, CHUNK), NEG, F32)
        return carry

    lax.fori_loop(j + 1, nchunk, fill, 0)


def _a_index(qi, wit, ki, topk):
    bsz, L, _ = qi.shape
    nchunk = L // CHUNK
    tri = jnp.tril(jnp.ones((CHUNK, CHUNK), BF16), -1)
    return pl.pallas_call(
        functools.partial(_a_index_kernel, topk=topk, seq_len=L),
        grid=(bsz, nchunk),
        in_specs=[pl.BlockSpec((1, CHUNK, IDX_HEADS * IDX_DIM), lambda b, j: (b, j, 0)),
                  pl.BlockSpec((1, IDX_HEADS, CHUNK), lambda b, j: (b, 0, j)),
                  pl.BlockSpec((1, L, IDX_DIM), lambda b, j: (b, 0, 0)),
                  pl.BlockSpec((CHUNK, CHUNK), lambda b, j: (0, 0))],
        out_specs=pl.BlockSpec((1, nchunk, CHUNK, CHUNK), lambda b, j: (b, 0, j, 0)),
        out_shape=jax.ShapeDtypeStruct((bsz, nchunk, L, CHUNK), F32),
        scratch_shapes=[pltpu.VMEM((nchunk, CHUNK, CHUNK), I32),
                        pltpu.VMEM((nchunk, CHUNK, CHUNK), I16)],
        compiler_params=_cparams("parallel", "parallel"),
        name="a_index",
    )(qi, wit, ki, tri)


def _a_attn_kernel(q_ref, ckv_ref, ckvt_ref, mask_ref, wuk_ref, wuv_ref, bias_ref, o_ref,
                   qabs_scr, lg_scr, m_scr, l_scr, acc_scr, p_scr, *, hc, nq, n_work):
    s = pl.program_id(0)
    j = (jnp.minimum(s, n_work - 1) // (N_HEADS // hc)) % nq
    slot = s & 1
    tq = CHUNK

    def placeholder(r, carry):
        rows = pl.ds(pl.multiple_of(r * ACC_ROWS, ACC_ROWS), ACC_ROWS)
        l_scr[1, rows, :] = jnp.ones((ACC_ROWS, LANES), F32)
        acc_scr[1, rows, :] = jnp.zeros((ACC_ROWS, KV_RANK), F32)
        return carry

    lax.fori_loop(0, jnp.where(s == 0, hc * tq // ACC_ROWS, 0), placeholder, 0)

    q = q_ref[0]
    for i in range(hc // 2):
        qa = jnp.dot(q[:, i * 2 * HEAD_DIM:(i + 1) * 2 * HEAD_DIM], wuk_ref[i],
                     preferred_element_type=F32)
        for par in range(2):
            qabs_scr[(2 * i + par) * tq:(2 * i + par + 1) * tq, :] = (
                qa[:, par * KV_RANK:(par + 1) * KV_RANK] * LOG2E).astype(BF16)
    o = (acc_scr[1 - slot] / jnp.sum(l_scr[1 - slot], axis=-1, keepdims=True)).astype(BF16)
    outs = [jnp.dot(jnp.concatenate([o[2 * i * tq:(2 * i + 1) * tq],
                                     o[(2 * i + 1) * tq:(2 * i + 2) * tq]], axis=1),
                    wuv_ref[i], preferred_element_type=F32) for i in range(hc // 2)]
    o_ref[0] = jnp.concatenate(outs, axis=1).astype(BF16)
    qg = qabs_scr[...]

    n_far_pairs = lax.shift_right_arithmetic(j - 1, 1)
    odd = (j & 1) == 0

    def keys(kc, width):
        return ckv_ref[0, pl.ds(kc, width)].reshape(width * CHUNK, KV_RANK)

    def logits(kc, width):
        keys_t = jnp.concatenate([ckvt_ref[0, kc + w] for w in range(width)], axis=1)
        lg = jnp.dot(qg, keys_t, preferred_element_type=F32)
        mk = jnp.concatenate([mask_ref[0, kc + w] for w in range(width)], axis=1)
        return lg + jnp.concatenate([mk] * hc, axis=0)

    def near_bias(i, with_prev):
        d0, d1 = bias_ref[i, 0], bias_ref[i, 1]
        z = jnp.zeros_like(d0)
        top, bot = [d0, z], [d1, d0]
        if with_prev:
            top, bot = [z, d1] + top, [z, z] + bot
        return jnp.concatenate([jnp.concatenate(top, axis=1), jnp.concatenate(bot, axis=1)], axis=0)

    def put_logits(kc, width, lg):
        for w in range(width):
            lg_scr[kc + w] = lg[:, w * CHUNK:(w + 1) * CHUNK]
        m = m_scr[...]
        for c in range(width * CHUNK // LANES):
            m = jnp.maximum(m, lg[:, c * LANES:(c + 1) * LANES])
        m_scr[...] = m

    m_scr[...] = jnp.full(m_scr.shape, -jnp.inf, F32)

    @pl.when(j >= 1)
    def _():
        bias = jnp.concatenate([near_bias(i, True) for i in range(hc)], axis=0)
        put_logits(j - 1, 2, logits(j - 1, 2) + bias)

    def far_pair(i, carry):
        kc = j - 3 - 2 * i
        put_logits(kc, 2, logits(kc, 2))
        return carry

    lax.fori_loop(0, n_far_pairs, far_pair, 0)

    @pl.when(odd)
    def _():
        bias = jnp.concatenate([near_bias(i, False) for i in range(hc)], axis=0)
        put_logits(0, 1, logits(0, 1) + jnp.where(j == 0, bias, 0.0))

    l_scr[slot] = jnp.zeros(l_scr.shape[1:], F32)
    acc_scr[slot] = jnp.zeros(acc_scr.shape[1:], F32)

    def pv(kc, width, reduce_max):
        ck = keys(kc, width)
        for piece in range(hc * tq // ACC_ROWS):
            for r in range(piece * ACC_ROWS // PV_ROWS, (piece + 1) * ACC_ROWS // PV_ROWS):
                rows = slice(r * PV_ROWS, (r + 1) * PV_ROWS)
                m = m_scr[rows, :]
                if reduce_max:
                    m = jnp.broadcast_to(jnp.max(m, axis=-1, keepdims=True), m.shape)
                    m_scr[rows, :] = m
                l = l_scr[slot, rows, :]
                for w in range(width):
                    for c in range(CHUNK // LANES):
                        col = w * CHUNK + c * LANES
                        p = jnp.exp2(lg_scr[kc + w, rows, c * LANES:(c + 1) * LANES] - m)
                        l = l + p
                        p_scr[rows, col:col + LANES] = p.astype(BF16)
                l_scr[slot, rows, :] = l
            rows = slice(piece * ACC_ROWS, (piece + 1) * ACC_ROWS)
            acc_scr[slot, rows, :] += jnp.dot(p_scr[rows, :width * CHUNK], ck,
                                              preferred_element_type=F32)

    @pl.when(j >= 1)
    def _():
        pv(j - 1, 2, True)

    def far_pv(i, carry):
        pv(j - 3 - 2 * i, 2, False)
        return carry

    lax.fori_loop(0, n_far_pairs, far_pv, 0)

    @pl.when(odd)
    def _():
        pv(0, 1, True)


def _a_attn(q, ckv, ckvt, mask, w_uk, w_uv, layer, bias_nd, hc=8):
    bsz, L, _ = q.shape
    nchunk = L // CHUNK
    hg = N_HEADS // hc
    m = hc * CHUNK
    ckv4 = ckv.reshape(bsz, nchunk, CHUNK, KV_RANK)
    n_work = bsz * nchunk * hg

    def item(s):
        return s // (nchunk * hg), (s // hg) % nchunk, s % hg

    cur = lambda s: item(jnp.minimum(s, n_work - 1))
    prev = lambda s: item(jnp.maximum(s - 1, 0))
    return pl.pallas_call(
        functools.partial(_a_attn_kernel, hc=hc, nq=nchunk, n_work=n_work),
        grid=(n_work + 1,),
        in_specs=[pl.BlockSpec((1, CHUNK, hc * HEAD_DIM), lambda s: cur(s)),
                  pl.BlockSpec((1, nchunk, CHUNK, KV_RANK), lambda s: (cur(s)[0], 0, 0, 0)),
                  pl.BlockSpec((1, nchunk, KV_RANK, CHUNK), lambda s: (cur(s)[0], 0, 0, 0)),
                  pl.BlockSpec((1, nchunk, CHUNK, CHUNK), lambda s: (cur(s)[0], 0, cur(s)[1], 0)),
                  pl.BlockSpec((None, hc // 2) + w_uk.shape[2:], lambda s: (layer, cur(s)[2], 0, 0)),
                  pl.BlockSpec((None, hc // 2) + w_uv.shape[2:], lambda s: (layer, prev(s)[2], 0, 0)),
                  pl.BlockSpec((hc, 2, BIAS_BLK, BIAS_BLK), lambda s: (cur(s)[2], 0, 0, 0))],
        out_specs=pl.BlockSpec((1, CHUNK, hc * HEAD_DIM), lambda s: prev(s)),
        out_shape=jax.ShapeDtypeStruct((bsz, L, HD), BF16),
        scratch_shapes=[pltpu.VMEM((m, KV_RANK), BF16),
                        pltpu.VMEM((nchunk, m, CHUNK), F32),
                        pltpu.VMEM((m, LANES), F32),
                        pltpu.VMEM((2, m, LANES), F32),
                        pltpu.VMEM((2, m, KV_RANK), F32),
                        pltpu.VMEM((m, 2 * CHUNK), BF16)],
        compiler_params=_cparams("arbitrary"),
        name="a_attn",
    )(q, ckv4, ckvt, mask, w_uk, w_uv, bias_nd)


def _bias_kernel(rb_ref, ids_ref, o_ref, *, n_tiles, far_bucket, shift_far):
    h = pl.program_id(0)
    far = rb_ref[far_bucket, h] if shift_far else 0.0
    for t in range(n_tiles):
        ids = ids_ref[t]
        out = jnp.zeros(ids.shape, F32)
        for b in range(N_BUCKETS):
            out = jnp.where(ids == b, rb_ref[b, h] - far, out)
        o_ref[0, t] = jnp.where(ids < 0, NEG, out * LOG2E)


def _t5_bucket(dist):
    max_exact = N_BUCKETS // 2
    d = jnp.maximum(dist, 0)
    large = max_exact + (jnp.log(jnp.maximum(d, 1).astype(F32) / max_exact)
                         / math.log(MAX_DISTANCE / max_exact)
                         * (N_BUCKETS - max_exact)).astype(I32)
    large = jnp.minimum(large, N_BUCKETS - 1)
    return jnp.where(d < max_exact, d, large)


def _bias_tiles(rel_bias, ids, shift_far):
    n_tiles, r, c = ids.shape
    return pl.pallas_call(
        functools.partial(_bias_kernel, n_tiles=n_tiles, far_bucket=N_BUCKETS - 1,
                          shift_far=shift_far),
        grid=(N_HEADS,),
        in_specs=[pl.BlockSpec(memory_space=pltpu.SMEM),
                  pl.BlockSpec((n_tiles, r, c), lambda h: (0, 0, 0))],
        out_specs=pl.BlockSpec((1, n_tiles, r, c), lambda h: (h, 0, 0, 0)),
        out_shape=jax.ShapeDtypeStruct((N_HEADS, n_tiles, r, c), F32),
        compiler_params=_cparams("parallel"),
        name="bias_tiles",
    )(rel_bias, ids)


def _b_inproj_kernel(x_ref, g_ref, sh_ref, sc_ref, w_ref, b_ref, q_ref, k_ref, v_ref):
    h = _normmod(x_ref[0], g_ref[...], sh_ref[0], sc_ref[0]).astype(BF16)
    proj = jnp.dot(h, w_ref[...], preferred_element_type=F32) + b_ref[...]
    q_ref[0] = (proj[:, :HD] * (HEAD_DIM ** -0.5 * LOG2E)).astype(BF16)
    k_ref[0] = proj[:, HD:HD + KV_HEADS * HEAD_DIM].astype(BF16)
    v_ref[0] = proj[:, HD + KV_HEADS * HEAD_DIM:].astype(BF16)


def _b_inproj(x, g, sh, sc, w, layer, b, tm=512):
    bsz, L, d = x.shape
    kvw = KV_HEADS * HEAD_DIM
    row = lambda b_, i: (b_, i, 0)
    per_b = lambda b_, i: (b_, 0, 0)
    const2 = lambda b_, i: (0, 0)
    return pl.pallas_call(
        _b_inproj_kernel,
        grid=(bsz, L // tm),
        in_specs=[pl.BlockSpec((1, tm, d), row),
                  pl.BlockSpec((1, d), const2),
                  pl.BlockSpec((1, 1, d), per_b),
                  pl.BlockSpec((1, 1, d), per_b),
                  _layer_spec(w, layer),
                  pl.BlockSpec((1, B_IN), const2)],
        out_specs=[pl.BlockSpec((1, tm, HD), row),
                   pl.BlockSpec((1, tm, kvw), row),
                   pl.BlockSpec((1, tm, kvw), row)],
        out_shape=[jax.ShapeDtypeStruct((bsz, L, HD), BF16),
                   jax.ShapeDtypeStruct((bsz, L, kvw), BF16),
                   jax.ShapeDtypeStruct((bsz, L, kvw), BF16)],
        compiler_params=_cparams("parallel", "parallel"),
        name="b_inproj",
    )(x, g, sh, sc, w, b)


B_BLOCKS = 8


def _b_attn_kernel(sink_ref, q_ref, kp_ref, kc_ref, vp_ref, vc_ref, bias_ref, o_ref):
    n = pl.program_id(1)
    w = WINDOW
    pairs = N_HEADS // KV_HEADS // 2
    kall = jnp.concatenate([kp_ref[0], kc_ref[0]], axis=0).astype(F32)
    vall = jnp.concatenate([vp_ref[0], vc_ref[0]], axis=0).astype(F32)
    low = lax.broadcasted_iota(I32, kall.shape, 1) < HEAD_DIM

    def padded(x, g):
        swapped = pltpu.roll(x, HEAD_DIM, axis=1)
        on_low, on_high = (x, swapped) if g == 0 else (swapped, x)
        return (jnp.where(low, on_low, 0.0).astype(BF16), jnp.where(low, 0.0, on_high).astype(BF16))

    kpad = [padded(kall, g) for g in range(KV_HEADS)]
    vpad = [padded(vall, g) for g in range(KV_HEADS)]
    for blk in range(B_BLOCKS):
        variant = jnp.where(n > 0, 0, 1) if blk == 0 else 0
        keys = slice(blk * w, (blk + 2) * w)
        q = q_ref[0, blk * w:(blk + 1) * w, :]
        outs = []
        for g in range(KV_HEADS):
            blocks = [q[:, (pairs * g + p) * 2 * HEAD_DIM:(pairs * g + p + 1) * 2 * HEAD_DIM]
                      for p in range(pairs)]
            lg = lax.dot_general(jnp.concatenate(blocks, axis=0),
                                 jnp.concatenate([kpad[g][0][keys], kpad[g][1][keys]], axis=0),
                                 _NT, preferred_element_type=F32)
            lg = lg + bias_ref[variant, g]
            probs = [[], []]
            for p in range(pairs):
                for par in range(2):
                    h = 2 * pairs * g + 2 * p + par
                    t = lg[p * w:(p + 1) * w, par * 2 * w:(par + 1) * 2 * w]
                    sink = sink_ref[0, h] * LOG2E
                    m = jnp.maximum(jnp.max(t, axis=-1, keepdims=True), sink)
                    e = jnp.exp2(t - m)
                    denom = jnp.sum(e, axis=-1, keepdims=True) + jnp.exp2(sink - m)
                    probs[par].append((e * (1.0 / denom)).astype(BF16))
            og = (jnp.dot(jnp.concatenate(probs[0], axis=0), vpad[g][0][keys],
                          preferred_element_type=F32)
                  + jnp.dot(jnp.concatenate(probs[1], axis=0), vpad[g][1][keys],
                            preferred_element_type=F32))
            outs += [og[p * w:(p + 1) * w] for p in range(pairs)]
        o_ref[0, blk * w:(blk + 1) * w, :] = jnp.concatenate(outs, axis=1).astype(BF16)


def _b_attn(q, k, v, sinks, bias_b):
    bsz, L, _ = q.shape
    w = WINDOW
    wide = B_BLOCKS * w
    assert L % wide == 0, (L, wide)
    kvw = KV_HEADS * HEAD_DIM
    cur = lambda b, n: (b, n, 0)
    prev = lambda b, n: (b, jnp.maximum(n * B_BLOCKS - 1, 0), 0)
    return pl.pallas_call(
        _b_attn_kernel,
        grid=(bsz, L // wide),
        in_specs=[pl.BlockSpec(memory_space=pltpu.SMEM),
                  pl.BlockSpec((1, wide, HD), cur),
                  pl.BlockSpec((1, w, kvw), prev),
                  pl.BlockSpec((1, wide, kvw), cur),
                  pl.BlockSpec((1, w, kvw), prev),
                  pl.BlockSpec((1, wide, kvw), cur),
                  pl.BlockSpec(bias_b.shape, lambda b, n: (0, 0, 0, 0))],
        out_specs=pl.BlockSpec((1, wide, HD), cur),
        out_shape=jax.ShapeDtypeStruct((bsz, L, HD), BF16),
        compiler_params=_cparams("parallel", "parallel"),
        name="b_attn",
    )(sinks.reshape(1, N_HEADS), q, k, k, v, v, bias_b)


FF_CHUNK = 256


def _out_ffn_kernel(o_ref, wo_ref, bo_ref, g1_ref, x_ref, g_ref, sh_ref, sc_ref, gate_ref,
                    w1_ref, w3_ref, w2_ref, fg_ref, y_ref, *, final_norm):
    mix = jnp.dot(o_ref[0], wo_ref[...], preferred_element_type=F32) + bo_ref[...]
    x = x_ref[0] + g1_ref[0] * mix
    h = _normmod(x, g_ref[...], sh_ref[0], sc_ref[0]).astype(BF16)
    acc = jnp.zeros(x.shape, F32)
    for c in range(D_FF // FF_CHUNK):
        cs = slice(c * FF_CHUNK, (c + 1) * FF_CHUNK)
        a1 = jnp.dot(h, w1_ref[:, cs], preferred_element_type=F32)
        a3 = jnp.dot(h, w3_ref[:, cs], preferred_element_type=F32)
        act = (a1 * jax.nn.sigmoid(a1)) * a3
        acc = acc + jnp.dot(act.astype(BF16), w2_ref[cs, :], preferred_element_type=F32)
    y = x + gate_ref[0] * acc
    if final_norm:
        y = (y * lax.rsqrt(jnp.mean(y * y, axis=-1, keepdims=True) + RMS_EPS)) * fg_ref[...]
    y_ref[0] = y


def _out_ffn(o, w_out, mix_layer, b_out, g1, x, g, sh, sc, gate, w1, w3, w2, layer, final_g,
             final_norm, tm=512):
    bsz, L, d = x.shape
    row = lambda b_, i: (b_, i, 0)
    per_b = lambda b_, i: (b_, 0, 0)
    const2 = lambda b_, i: (0, 0)
    return pl.pallas_call(
        functools.partial(_out_ffn_kernel, final_norm=final_norm),
        grid=(bsz, L // tm),
        in_specs=[pl.BlockSpec((1, tm, HD), row),
                  _layer_spec(w_out, mix_layer),
                  pl.BlockSpec((1, d), const2),
                  pl.BlockSpec((1, 1, d), per_b),
                  pl.BlockSpec((1, tm, d), row),
                  pl.BlockSpec((1, d), const2),
                  pl.BlockSpec((1, 1, d), per_b),
                  pl.BlockSpec((1, 1, d), per_b),
                  pl.BlockSpec((1, 1, d), per_b),
                  _layer_spec(w1, layer),
                  _layer_spec(w3, layer),
                  _layer_spec(w2, layer),
                  pl.BlockSpec((1, d), const2)],
        out_specs=pl.BlockSpec((1, tm, d), row),
        out_shape=jax.ShapeDtypeStruct((bsz, L, d), F32),
        compiler_params=_cparams("parallel", "parallel"),
        name="out_ffn",
    )(o, w_out, b_out, g1, x, g, sh, sc, gate, w1, w3, w2, final_g)


def kernel(x, c, rel_bias, w_ada, b_ada, norm_mix_g, norm_ffn_g, a_w_in, a_kv_norm_g, a_w_uk,
           a_w_uv, a_idx_k_g, a_idx_k_b, a_w_out, b_w_in, b_b_in, b_sinks, b_w_out, b_b_out,
           ffn_w1, ffn_w3, ffn_w2, norm_final_g):
    bsz, L, d = x.shape
    depth = w_ada.shape[0]
    topk = min(INDEX_TOPK, L // 4)

    mod = _adaln(c, w_ada, b_ada)

    r = jnp.arange(BIAS_BLK)
    ids_a = jnp.stack([_t5_bucket(r[:, None] - r[None, :]),
                       _t5_bucket(r[:, None] - r[None, :] + BIAS_BLK)])
    bias_a = _bias_tiles(rel_bias, ids_a, shift_far=True)
    dist_b = jnp.arange(WINDOW)[:, None] + WINDOW - jnp.arange(2 * WINDOW)[None, :]
    in_window = (dist_b >= 0) & (dist_b < WINDOW)
    ids_b = jnp.where(in_window, _t5_bucket(dist_b), -1)
    ids_b0 = jnp.where(jnp.arange(2 * WINDOW)[None, :] >= WINDOW, ids_b, -1)
    bias_b = _bias_tiles(rel_bias, jnp.stack([ids_b, ids_b0]), shift_far=False)
    pairs = N_HEADS // KV_HEADS // 2
    bias_b = bias_b.reshape(KV_HEADS, pairs, 2, 2, WINDOW, 2 * WINDOW).transpose(3, 0, 1, 4, 2, 5)
    bias_b = bias_b.reshape(2, KV_HEADS, pairs * WINDOW, 4 * WINDOW)

    a_w_main = jnp.pad(a_w_in[:, :, :A_MAIN], ((0, 0), (0, 0), (0, A_MAIN_PAD - A_MAIN))).astype(BF16)
    a_w_wi_t = jnp.pad(a_w_in[:, :, A_WI0:A_WI0 + IDX_HEADS].transpose(0, 2, 1),
                       ((0, 0), (0, 16 - IDX_HEADS), (0, 0))).astype(BF16)
    a_w_uk, a_w_uv, a_w_out, b_w_in, b_w_out, ffn_w1, ffn_w3, ffn_w2 = [
        w.astype(BF16) for w in (jnp.swapaxes(_pair_blocks(a_w_uk), -1, -2),
                                 _pair_blocks(a_w_uv), a_w_out, b_w_in,
                                 b_w_out, ffn_w1, ffn_w3, ffn_w2)]

    zero_bias = jnp.zeros((1, d), F32)
    for i in range(depth):
        sh1, sc1, g1, sh2, sc2, g2 = [m.reshape(bsz, 1, d) for m in jnp.split(mod[i], 6, axis=-1)]
        jm = i // 2
        if i % 2 == 0:
            q, ckv, ckvt, qi, ki, wit = _a_inproj(
                x, norm_mix_g[i][None], sh1, sc1, a_w_main, a_w_wi_t, jm, a_kv_norm_g[jm][None],
                a_idx_k_g[jm][None], a_idx_k_b[jm][None])
            mask = _a_index(qi, wit, ki, topk)
            o = _a_attn(q, ckv, ckvt, mask, a_w_uk, a_w_uv, jm, bias_a)
            w_out, b_out = a_w_out, zero_bias
        else:
            q, k, v = _b_inproj(x, norm_mix_g[i][None], sh1, sc1, b_w_in, jm, b_b_in[jm][None])
            o = _b_attn(q, k, v, b_sinks[jm], bias_b)
            w_out, b_out = b_w_out, b_b_out[jm][None]
        x = _out_ffn(o, w_out, jm, b_out, g1, x, norm_ffn_g[i][None], sh2, sc2, g2,
                     ffn_w1, ffn_w3, ffn_w2, i, norm_final_g[None], final_norm=(i == depth - 1))
    return x
```

```python
import functools
import math

import numpy as np
import jax
import jax.numpy as jnp
from jax import lax
from jax.experimental import pallas as pl
from jax.experimental.pallas import tpu as pltpu

D_MODEL = 1024
N_HEADS = 16
HEAD_DIM = 64
KV_RANK = 256
IDX_HEADS = 8
IDX_DIM = 64
INDEX_TOPK = 256
KV_HEADS = 2
WINDOW = 128
N_BUCKETS = 32
MAX_DISTANCE = 128
D_FF = 2816
RMS_EPS = 1e-6
NEG = -1e30

HD = N_HEADS * HEAD_DIM
A_Q0, A_KV0, A_QI0, A_KI0, A_WI0 = 0, HD, HD + KV_RANK, HD + KV_RANK + IDX_HEADS * IDX_DIM, \
    HD + KV_RANK + IDX_HEADS * IDX_DIM + IDX_DIM
A_MAIN = A_WI0
A_MAIN_PAD = 1920
B_IN = (N_HEADS + 2 * KV_HEADS) * HEAD_DIM

CHUNK = 256
PV_ROWS = 64
ACC_ROWS = 256
HALF_ROWS = 16
LANES = 128
BIAS_BLK = 128
VMEM_LIMIT = 56 * 1024 * 1024

F32 = jnp.float32
BF16 = jnp.bfloat16
I32 = jnp.int32
I16 = jnp.int16

_NT = (((1,), (1,)), ((), ()))


def _cparams(*sem):
    return pltpu.CompilerParams(dimension_semantics=sem, vmem_limit_bytes=VMEM_LIMIT)


def _pair_blocks(w):
    even, odd = w[..., 0::2, :, :], w[..., 1::2, :, :]
    zero = jnp.zeros_like(even)
    return jnp.concatenate([jnp.concatenate([even, zero], axis=-1),
                            jnp.concatenate([zero, odd], axis=-1)], axis=-2)


def _layer_spec(stack, layer):
    zeros = (0,) * (stack.ndim - 1)
    return pl.BlockSpec((None,) + stack.shape[1:], lambda *_: (layer,) + zeros)


def _f32_key(v):
    b = int(np.array(v, np.float32).view(np.int32))
    return b ^ ((b >> 31) & 0x7FFFFFFF)


LOG2E = math.log2(math.e)
KEY_NEG = _f32_key(NEG)
INT_MIN = -(2 ** 31)


def _adaln_kernel(c_ref, w_ref, b_ref, o_ref):
    c = c_ref[...]
    cs = c * jax.nn.sigmoid(c)
    o_ref[0] = jnp.dot(cs, w_ref[0], preferred_element_type=F32,
                       precision=lax.Precision.HIGHEST) + b_ref[0]


def _adaln(c, w_ada, b_ada):
    depth, d, n = w_ada.shape
    bsz = c.shape[0]
    tn = 1536
    return pl.pallas_call(
        _adaln_kernel,
        grid=(depth, n // tn),
        in_specs=[pl.BlockSpec((bsz, d), lambda i, j: (0, 0)),
                  pl.BlockSpec((1, d, tn), lambda i, j: (i, 0, j)),
                  pl.BlockSpec((1, 1, tn), lambda i, j: (i, 0, j))],
        out_specs=pl.BlockSpec((1, bsz, tn), lambda i, j: (i, 0, j)),
        out_shape=jax.ShapeDtypeStruct((depth, bsz, n), F32),
        compiler_params=_cparams("parallel", "parallel"),
        name="adaln",
    )(c, w_ada, b_ada.reshape(depth, 1, n))


def _normmod(x, g, sh, sc):
    ms = jnp.mean(x * x, axis=-1, keepdims=True)
    y = (x * lax.rsqrt(ms + RMS_EPS)) * g
    return y * (1.0 + sc) + sh


def _a_inproj_kernel(x_ref, g_ref, sh_ref, sc_ref, w_ref, wwi_ref, kvg_ref, ikg_ref, ikb_ref,
                     q_ref, ckv_ref, ckvt_ref, qi_ref, ki_ref, wit_ref):
    h = _normmod(x_ref[0], g_ref[...], sh_ref[0], sc_ref[0]).astype(BF16)
    proj = jnp.dot(h, w_ref[...], preferred_element_type=F32)
    q_ref[0] = (proj[:, A_Q0:A_KV0] * (HEAD_DIM ** -0.5)).astype(BF16)
    ckv = proj[:, A_KV0:A_QI0]
    ckv = (ckv * lax.rsqrt(jnp.mean(ckv * ckv, axis=-1, keepdims=True) + RMS_EPS)) * kvg_ref[...]
    ckv_ref[0] = ckv.astype(BF16)
    for c in range(ckvt_ref.shape[1]):
        ckvt_ref[0, c] = ckv[c * CHUNK:(c + 1) * CHUNK].T.astype(BF16)
    qi_ref[0] = proj[:, A_QI0:A_KI0].astype(BF16)
    ki = proj[:, A_KI0:A_WI0]
    mu = jnp.mean(ki, axis=-1, keepdims=True)
    var = jnp.mean(jnp.square(ki - mu), axis=-1, keepdims=True)
    ki = ((ki - mu) * lax.rsqrt(var + RMS_EPS)) * ikg_ref[...] + ikb_ref[...]
    ki_ref[0] = ki.astype(BF16)
    wit = lax.dot_general(wwi_ref[...], h, _NT, preferred_element_type=F32)
    wit_ref[0] = wit[:IDX_HEADS] * (IDX_HEADS ** -0.5 * IDX_DIM ** -0.5)


def _a_inproj(x, g, sh, sc, w_main, w_wi_t, layer, kv_g, ik_g, ik_b, tm=512):
    bsz, L, d = x.shape
    row = lambda b, i: (b, i, 0)
    per_b = lambda b, i: (b, 0, 0)
    const2 = lambda b, i: (0, 0)
    return pl.pallas_call(
        _a_inproj_kernel,
        grid=(bsz, L // tm),
        in_specs=[pl.BlockSpec((1, tm, d), row),
                  pl.BlockSpec((1, d), const2),
                  pl.BlockSpec((1, 1, d), per_b),
                  pl.BlockSpec((1, 1, d), per_b),
                  _layer_spec(w_main, layer),
                  _layer_spec(w_wi_t, layer),
                  pl.BlockSpec((1, KV_RANK), const2),
                  pl.BlockSpec((1, IDX_DIM), const2),
                  pl.BlockSpec((1, IDX_DIM), const2)],
        out_specs=[pl.BlockSpec((1, tm, HD), row),
                   pl.BlockSpec((1, tm, KV_RANK), row),
                   pl.BlockSpec((1, tm // CHUNK, KV_RANK, CHUNK), lambda b, i: (b, i, 0, 0)),
                   pl.BlockSpec((1, tm, IDX_HEADS * IDX_DIM), row),
                   pl.BlockSpec((1, tm, IDX_DIM), row),
                   pl.BlockSpec((1, IDX_HEADS, tm), lambda b, i: (b, 0, i))],
        out_shape=[jax.ShapeDtypeStruct((bsz, L, HD), BF16),
                   jax.ShapeDtypeStruct((bsz, L, KV_RANK), BF16),
                   jax.ShapeDtypeStruct((bsz, L // CHUNK, KV_RANK, CHUNK), BF16),
                   jax.ShapeDtypeStruct((bsz, L, IDX_HEADS * IDX_DIM), BF16),
                   jax.ShapeDtypeStruct((bsz, L, IDX_DIM), BF16),
                   jax.ShapeDtypeStruct((bsz, IDX_HEADS, L), F32)],
        compiler_params=_cparams("parallel", "parallel"),
        name="a_inproj",
    )(x, g, sh, sc, w_main, w_wi_t, kv_g, ik_g, ik_b)


def _a_index_kernel(qi_ref, wit_ref, ki_ref, tri_ref, mask_ref, key_scr, half_scr, *, topk,
                    seq_len):
    j = pl.program_id(1)
    nchunk = seq_len // CHUNK
    qi = qi_ref[0]
    wit = wit_ref[0]
    t_glob = j * CHUNK + lax.broadcasted_iota(I32, (CHUNK, CHUNK), 1)
    s_loc = lax.broadcasted_iota(I32, (CHUNK, CHUNK), 0)

    def chunks(fn, init):
        def pair(i, carry):
            return fn(2 * i + 1, fn(2 * i, carry))
        carry = lax.fori_loop(0, lax.shift_right_logical(j + 1, 1), pair, init)
        return lax.cond((j & 1) == 0, lambda c: fn(j, c), lambda c: c, carry)

    def score_chunk(kc, carry):
        kik = ki_ref[0, pl.ds(pl.multiple_of(kc * CHUNK, CHUNK), CHUNK), :]
        acc = jnp.zeros((CHUNK, CHUNK), F32)
        for h in range(IDX_HEADS):
            r = lax.dot_general(kik, qi[:, h * IDX_DIM:(h + 1) * IDX_DIM], _NT,
                                preferred_element_type=F32)
            acc = acc + jnp.maximum(r, 0.0) * wit[h:h + 1, :]
        sc = jnp.where(kc * CHUNK + s_loc <= t_glob, acc, NEG)
        bits = lax.bitcast_convert_type(sc, I32)
        key = bits ^ ((bits >> 31) & 0x7FFFFFFF)
        key_scr[kc] = key
        half_scr[kc] = (key >> 16).astype(I16)
        return carry

    chunks(score_chunk, 0)

    n_beyond = seq_len - (j + 1) * CHUNK
    ge = lambda a, b: a >= b
    gt = lambda a, b: a > b

    def count(pred, thr):
        def body(kc, acc):
            m = jnp.where(pred(key_scr[kc], thr), jnp.int32(1), jnp.int32(0))
            return acc + m.reshape(CHUNK // 8, 8, CHUNK).sum(axis=0)
        acc = chunks(body, jnp.zeros((8, CHUNK), I32))
        cnt = acc.sum(axis=0, keepdims=True)
        return cnt + jnp.where(pred(KEY_NEG, thr), n_beyond, 0)

    def count_half(pred, thr16):
        t = jnp.broadcast_to(thr16, (HALF_ROWS, CHUNK)).astype(I16)
        def body(kc, acc):
            m = jnp.where(pred(half_scr[kc].reshape(CHUNK // HALF_ROWS, HALF_ROWS, CHUNK), t[None]),
                          jnp.int16(1), jnp.int16(0))
            for r in range(CHUNK // HALF_ROWS):
                acc = acc + m[r]
            return acc
        acc = chunks(body, jnp.zeros((HALF_ROWS, CHUNK), I16))
        return acc.astype(I32).sum(axis=0, keepdims=True)

    def beyond(cand):
        return jnp.where(KEY_NEG >= cand, n_beyond, 0)

    lowest = jnp.full((1, CHUNK), INT_MIN, I32)

    def bisect():
        zero = jnp.zeros((1, CHUNK), I32)
        thr = jnp.where(count_half(ge, zero) + beyond(zero) >= topk, zero, lowest)

        def bisect_high(i, thr):
            cand = thr + lax.shift_left(jnp.int32(1), 30 - i)
            cnt = count_half(ge, cand >> 16) + beyond(cand)
            return jnp.where(cnt >= topk, cand, thr)

        thr = lax.fori_loop(0, 15, bisect_high, thr)

        thr_hi = thr >> 16
        n_above = count_half(gt, thr_hi)

        def low_half(kc, carry):
            key = key_scr[kc]
            low = (key & 0xFFFF) - 0x8000
            half_scr[kc] = jnp.where((key >> 16) == thr_hi, low, -0x8000).astype(I16)
            return carry

        chunks(low_half, 0)

        def bisect_low(i, thr):
            cand = thr + lax.shift_left(jnp.int32(1), 15 - i)
            cnt = n_above + count_half(ge, (cand & 0xFFFF) - 0x8000) + beyond(cand)
            return jnp.where(cnt >= topk, cand, thr)

        return lax.fori_loop(0, 16, bisect_low, thr)

    takes_all = (j + 1) * CHUNK <= topk
    thr = lax.cond(takes_all, lambda: lowest, bisect)

    need = (topk - count(gt, thr)).astype(F32)
    tri = tri_ref[...]

    def emit_ranked(kc, seen):
        key = key_scr[kc]
        eq = jnp.where(key == thr, 1.0, 0.0)
        rank = jnp.dot(tri, eq.astype(BF16), preferred_element_type=F32) + seen
        take = jnp.where(key > thr, 1.0, jnp.where(rank < need, eq, 0.0))
        causal = kc * CHUNK + s_loc <= t_glob
        m = jnp.where(causal, jnp.where(take > 0.5, 0.0, NEG), NEG)
        mask_ref[0, kc] = m.T
        return seen + jnp.sum(eq, axis=0, keepdims=True)

    def emit_plain(kc, seen):
        causal = kc * CHUNK + s_loc <= t_glob
        m = jnp.where(causal, jnp.where(key_scr[kc] >= thr, 0.0, NEG), NEG)
        mask_ref[0, kc] = m.T
        return seen

    exact_fit = takes_all | (jnp.max(count(ge, thr)) <= topk)
    lax.cond(exact_fit, lambda: chunks(emit_plain, jnp.zeros((1, CHUNK), F32)),
             lambda: chunks(emit_ranked, jnp.zeros((1, CHUNK), F32)))

    def fill(kc, carry):
        mask_ref[0, kc] = jnp.full((CHUNK, CHUNK), NEG, F32)
        return carry

    lax.fori_loop(j + 1, nchunk, fill, 0)


def _a_index(qi, wit, ki, topk):
    bsz, L, _ = qi.shape
    nchunk = L // CHUNK
    tri = jnp.tril(jnp.ones((CHUNK, CHUNK), BF16), -1)
    return pl.pallas_call(
        functools.partial(_a_index_kernel, topk=topk, seq_len=L),
        grid=(bsz, nchunk),
        in_specs=[pl.BlockSpec((1, CHUNK, IDX_HEADS * IDX_DIM), lambda b, j: (b, j, 0)),
                  pl.BlockSpec((1, IDX_HEADS, CHUNK), lambda b, j: (b, 0, j)),
                  pl.BlockSpec((1, L, IDX_DIM), lambda b, j: (b, 0, 0)),
                  pl.BlockSpec((CHUNK, CHUNK), lambda b, j: (0, 0))],
        out_specs=pl.BlockSpec((1, nchunk, CHUNK, CHUNK), lambda b, j: (b, 0, j, 0)),
        out_shape=jax.ShapeDtypeStruct((bsz, nchunk, L, CHUNK), F32),
        scratch_shapes=[pltpu.VMEM((nchunk, CHUNK, CHUNK), I32),
                        pltpu.VMEM((nchunk, CHUNK, CHUNK), I16)],
        compiler_params=_cparams("parallel", "parallel"),
        name="a_index",
    )(qi, wit, ki, tri)


def _a_attn_kernel(q_ref, ckv_ref, ckvt_ref, mask_ref, wuk_ref, wuv_ref, bias_ref, o_ref,
                   qabs_scr, lg_scr, m_scr, l_scr, acc_scr, p_scr, *, hc, nq, n_work):
    s = pl.program_id(0)
    j = (jnp.minimum(s, n_work - 1) // (N_HEADS // hc)) % nq
    slot = s & 1
    tq = CHUNK

    def placeholder(r, carry):
        rows = pl.ds(pl.multiple_of(r * ACC_ROWS, ACC_ROWS), ACC_ROWS)
        l_scr[1, rows, :] = jnp.ones((ACC_ROWS, LANES), F32)
        acc_scr[1, rows, :] = jnp.zeros((ACC_ROWS, KV_RANK), F32)
        return carry

    lax.fori_loop(0, jnp.where(s == 0, hc * tq // ACC_ROWS, 0), placeholder, 0)

    q = q_ref[0]
    for i in range(hc // 2):
        qa = jnp.dot(q[:, i * 2 * HEAD_DIM:(i + 1) * 2 * HEAD_DIM], wuk_ref[i],
                     preferred_element_type=F32)
        for par in range(2):
            qabs_scr[(2 * i + par) * tq:(2 * i + par + 1) * tq, :] = (
                qa[:, par * KV_RANK:(par + 1) * KV_RANK] * LOG2E).astype(BF16)
    o = (acc_scr[1 - slot] / jnp.sum(l_scr[1 - slot], axis=-1, keepdims=True)).astype(BF16)
    outs = [jnp.dot(jnp.concatenate([o[2 * i * tq:(2 * i + 1) * tq],
                                     o[(2 * i + 1) * tq:(2 * i + 2) * tq]], axis=1),
                    wuv_ref[i], preferred_element_type=F32) for i in range(hc // 2)]
    o_ref[0] = jnp.concatenate(outs, axis=1).astype(BF16)
    qg = qabs_scr[...]

    n_far_pairs = lax.shift_right_arithmetic(j - 1, 1)
    odd = (j & 1) == 0

    def keys(kc, width):
        return ckv_ref[0, pl.ds(kc, width)].reshape(width * CHUNK, KV_RANK)

    def logits(kc, width):
        keys_t = jnp.concatenate([ckvt_ref[0, kc + w] for w in range(width)], axis=1)
        lg = jnp.dot(qg, keys_t, preferred_element_type=F32)
        mk = jnp.concatenate([mask_ref[0, kc + w] for w in range(width)], axis=1)
        return lg + jnp.concatenate([mk] * hc, axis=0)

    def near_bias(i, with_prev):
        d0, d1 = bias_ref[i, 0], bias_ref[i, 1]
        z = jnp.zeros_like(d0)
        top, bot = [d0, z], [d1, d0]
        if with_prev:
            top, bot = [z, d1] + top, [z, z] + bot
        return jnp.concatenate([jnp.concatenate(top, axis=1), jnp.concatenate(bot, axis=1)], axis=0)

    def put_logits(kc, width, lg):
        for w in range(width):
            lg_scr[kc + w] = lg[:, w * CHUNK:(w + 1) * CHUNK]
        m = m_scr[...]
        for c in range(width * CHUNK // LANES):
            m = jnp.maximum(m, lg[:, c * LANES:(c + 1) * LANES])
        m_scr[...] = m

    m_scr[...] = jnp.full(m_scr.shape, -jnp.inf, F32)

    @pl.when(j >= 1)
    def _():
        bias = jnp.concatenate([near_bias(i, True) for i in range(hc)], axis=0)
        put_logits(j - 1, 2, logits(j - 1, 2) + bias)

    def far_pair(i, carry):
        kc = j - 3 - 2 * i
        put_logits(kc, 2, logits(kc, 2))
        return carry

    lax.fori_loop(0, n_far_pairs, far_pair, 0)

    @pl.when(odd)
    def _():
        bias = jnp.concatenate([near_bias(i, False) for i in range(hc)], axis=0)
        put_logits(0, 1, logits(0, 1) + jnp.where(j == 0, bias, 0.0))

    l_scr[slot] = jnp.zeros(l_scr.shape[1:], F32)
    acc_scr[slot] = jnp.zeros(acc_scr.shape[1:], F32)

    def pv(kc, width, reduce_max):
        ck = keys(kc, width)
        for piece in range(hc * tq // ACC_ROWS):
            for r in range(piece * ACC_ROWS // PV_ROWS, (piece + 1) * ACC_ROWS // PV_ROWS):
                rows = slice(r * PV_ROWS, (r + 1) * PV_ROWS)
                m = m_scr[rows, :]
                if reduce_max:
                    m = jnp.broadcast_to(jnp.max(m, axis=-1, keepdims=True), m.shape)
                    m_scr[rows, :] = m
                l = l_scr[slot, rows, :]
                for w in range(width):
                    for c in range(CHUNK // LANES):
                        col = w * CHUNK + c * LANES
                        p = jnp.exp2(lg_scr[kc + w, rows, c * LANES:(c + 1) * LANES] - m)
                        l = l + p
                        p_scr[rows, col:col + LANES] = p.astype(BF16)
                l_scr[slot, rows, :] = l
            rows = slice(piece * ACC_ROWS, (piece + 1) * ACC_ROWS)
            acc_scr[slot, rows, :] += jnp.dot(p_scr[rows, :width * CHUNK], ck,
                                              preferred_element_type=F32)

    @pl.when(j >= 1)
    def _():
        pv(j - 1, 2, True)

    def far_pv(i, carry):
        pv(j - 3 - 2 * i, 2, False)
        return carry

    lax.fori_loop(0, n_far_pairs, far_pv, 0)

    @pl.when(odd)
    def _():
        pv(0, 1, True)


def _a_attn(q, ckv, ckvt, mask, w_uk, w_uv, layer, bias_nd, hc=8):
    bsz, L, _ = q.shape
    nchunk = L // CHUNK
    hg = N_HEADS // hc
    m = hc * CHUNK
    ckv4 = ckv.reshape(bsz, nchunk, CHUNK, KV_RANK)
    n_work = bsz * nchunk * hg

    def item(s):
        return s // (nchunk * hg), (s // hg) % nchunk, s % hg

    cur = lambda s: item(jnp.minimum(s, n_work - 1))
    prev = lambda s: item(jnp.maximum(s - 1, 0))
    return pl.pallas_call(
        functools.partial(_a_attn_kernel, hc=hc, nq=nchunk, n_work=n_work),
        grid=(n_work + 1,),
        in_specs=[pl.BlockSpec((1, CHUNK, hc * HEAD_DIM), lambda s: cur(s)),
                  pl.BlockSpec((1, nchunk, CHUNK, KV_RANK), lambda s: (cur(s)[0], 0, 0, 0)),
                  pl.BlockSpec((1, nchunk, KV_RANK, CHUNK), lambda s: (cur(s)[0], 0, 0, 0)),
                  pl.BlockSpec((1, nchunk, CHUNK, CHUNK), lambda s: (cur(s)[0], 0, cur(s)[1], 0)),
                  pl.BlockSpec((None, hc // 2) + w_uk.shape[2:], lambda s: (layer, cur(s)[2], 0, 0)),
                  pl.BlockSpec((None, hc // 2) + w_uv.shape[2:], lambda s: (layer, prev(s)[2], 0, 0)),
                  pl.BlockSpec((hc, 2, BIAS_BLK, BIAS_BLK), lambda s: (cur(s)[2], 0, 0, 0))],
        out_specs=pl.BlockSpec((1, CHUNK, hc * HEAD_DIM), lambda s: prev(s)),
        out_shape=jax.ShapeDtypeStruct((bsz, L, HD), BF16),
        scratch_shapes=[pltpu.VMEM((m, KV_RANK), BF16),
                        pltpu.VMEM((nchunk, m, CHUNK), F32),
                        pltpu.VMEM((m, LANES), F32),
                        pltpu.VMEM((2, m, LANES), F32),
                        pltpu.VMEM((2, m, KV_RANK), F32),
                        pltpu.VMEM((m, 2 * CHUNK), BF16)],
        compiler_params=_cparams("arbitrary"),
        name="a_attn",
    )(q, ckv4, ckvt, mask, w_uk, w_uv, bias_nd)


def _bias_kernel(rb_ref, ids_ref, o_ref, *, n_tiles, far_bucket, shift_far):
    h = pl.program_id(0)
    far = rb_ref[far_bucket, h] if shift_far else 0.0
    for t in range(n_tiles):
        ids = ids_ref[t]
        out = jnp.zeros(ids.shape, F32)
        for b in range(N_BUCKETS):
            out = jnp.where(ids == b, rb_ref[b, h] - far, out)
        o_ref[0, t] = jnp.where(ids < 0, NEG, out * LOG2E)


def _t5_bucket(dist):
    max_exact = N_BUCKETS // 2
    d = jnp.maximum(dist, 0)
    large = max_exact + (jnp.log(jnp.maximum(d, 1).astype(F32) / max_exact)
                         / math.log(MAX_DISTANCE / max_exact)
                         * (N_BUCKETS - max_exact)).astype(I32)
    large = jnp.minimum(large, N_BUCKETS - 1)
    return jnp.where(d < max_exact, d, large)


def _bias_tiles(rel_bias, ids, shift_far):
    n_tiles, r, c = ids.shape
    return pl.pallas_call(
        functools.partial(_bias_kernel, n_tiles=n_tiles, far_bucket=N_BUCKETS - 1,
                          shift_far=shift_far),
        grid=(N_HEADS,),
        in_specs=[pl.BlockSpec(memory_space=pltpu.SMEM),
                  pl.BlockSpec((n_tiles, r, c), lambda h: (0, 0, 0))],
        out_specs=pl.BlockSpec((1, n_tiles, r, c), lambda h: (h, 0, 0, 0)),
        out_shape=jax.ShapeDtypeStruct((N_HEADS, n_tiles, r, c), F32),
        compiler_params=_cparams("parallel"),
        name="bias_tiles",
    )(rel_bias, ids)


def _b_inproj_kernel(x_ref, g_ref, sh_ref, sc_ref, w_ref, b_ref, q_ref, k_ref, v_ref):
    h = _normmod(x_ref[0], g_ref[...], sh_ref[0], sc_ref[0]).astype(BF16)
    proj = jnp.dot(h, w_ref[...], preferred_element_type=F32) + b_ref[...]
    q_ref[0] = (proj[:, :HD] * (HEAD_DIM ** -0.5 * LOG2E)).astype(BF16)
    k_ref[0] = proj[:, HD:HD + KV_HEADS * HEAD_DIM].astype(BF16)
    v_ref[0] = proj[:, HD + KV_HEADS * HEAD_DIM:].astype(BF16)


def _b_inproj(x, g, sh, sc, w, layer, b, tm=512):
    bsz, L, d = x.shape
    kvw = KV_HEADS * HEAD_DIM
    row = lambda b_, i: (b_, i, 0)
    per_b = lambda b_, i: (b_, 0, 0)
    const2 = lambda b_, i: (0, 0)
    return pl.pallas_call(
        _b_inproj_kernel,
        grid=(bsz, L // tm),
        in_specs=[pl.BlockSpec((1, tm, d), row),
                  pl.BlockSpec((1, d), const2),
                  pl.BlockSpec((1, 1, d), per_b),
                  pl.BlockSpec((1, 1, d), per_b),
                  _layer_spec(w, layer),
                  pl.BlockSpec((1, B_IN), const2)],
        out_specs=[pl.BlockSpec((1, tm, HD), row),
                   pl.BlockSpec((1, tm, kvw), row),
                   pl.BlockSpec((1, tm, kvw), row)],
        out_shape=[jax.ShapeDtypeStruct((bsz, L, HD), BF16),
                   jax.ShapeDtypeStruct((bsz, L, kvw), BF16),
                   jax.ShapeDtypeStruct((bsz, L, kvw), BF16)],
        compiler_params=_cparams("parallel", "parallel"),
        name="b_inproj",
    )(x, g, sh, sc, w, b)


B_BLOCKS = 8


def _b_attn_kernel(sink_ref, q_ref, kp_ref, kc_ref, vp_ref, vc_ref, bias_ref, o_ref):
    n = pl.program_id(1)
    w = WINDOW
    pairs = N_HEADS // KV_HEADS // 2
    kall = jnp.concatenate([kp_ref[0], kc_ref[0]], axis=0).astype(F32)
    vall = jnp.concatenate([vp_ref[0], vc_ref[0]], axis=0).astype(F32)
    low = lax.broadcasted_iota(I32, kall.shape, 1) < HEAD_DIM

    def padded(x, g):
        swapped = pltpu.roll(x, HEAD_DIM, axis=1)
        on_low, on_high = (x, swapped) if g == 0 else (swapped, x)
        return (jnp.where(low, on_low, 0.0).astype(BF16), jnp.where(low, 0.0, on_high).astype(BF16))

    kpad = [padded(kall, g) for g in range(KV_HEADS)]
    vpad = [padded(vall, g) for g in range(KV_HEADS)]
    for blk in range(B_BLOCKS):
        variant = jnp.where(n > 0, 0, 1) if blk == 0 else 0
        keys = slice(blk * w, (blk + 2) * w)
        q = q_ref[0, blk * w:(blk + 1) * w, :]
        outs = []
        for g in range(KV_HEADS):
            blocks = [q[:, (pairs * g + p) * 2 * HEAD_DIM:(pairs * g + p + 1) * 2 * HEAD_DIM]
                      for p in range(pairs)]
            lg = lax.dot_general(jnp.concatenate(blocks, axis=0),
                                 jnp.concatenate([kpad[g][0][keys], kpad[g][1][keys]], axis=0),
                                 _NT, preferred_element_type=F32)
            lg = lg + bias_ref[variant, g]
            probs = [[], []]
            for p in range(pairs):
                for par in range(2):
                    h = 2 * pairs * g + 2 * p + par
                    t = lg[p * w:(p + 1) * w, par * 2 * w:(par + 1) * 2 * w]
                    sink = sink_ref[0, h] * LOG2E
                    m = jnp.maximum(jnp.max(t, axis=-1, keepdims=True), sink)
                    e = jnp.exp2(t - m)
                    denom = jnp.sum(e, axis=-1, keepdims=True) + jnp.exp2(sink - m)
                    probs[par].append((e * (1.0 / denom)).astype(BF16))
            og = (jnp.dot(jnp.concatenate(probs[0], axis=0), vpad[g][0][keys],
                          preferred_element_type=F32)
                  + jnp.dot(jnp.concatenate(probs[1], axis=0), vpad[g][1][keys],
                            preferred_element_type=F32))
            outs += [og[p * w:(p + 1) * w] for p in range(pairs)]
        o_ref[0, blk * w:(blk + 1) * w, :] = jnp.concatenate(outs, axis=1).astype(BF16)


def _b_attn(q, k, v, sinks, bias_b):
    bsz, L, _ = q.shape
    w = WINDOW
    wide = B_BLOCKS * w
    assert L % wide == 0, (L, wide)
    kvw = KV_HEADS * HEAD_DIM
    cur = lambda b, n: (b, n, 0)
    prev = lambda b, n: (b, jnp.maximum(n * B_BLOCKS - 1, 0), 0)
    return pl.pallas_call(
        _b_attn_kernel,
        grid=(bsz, L // wide),
        in_specs=[pl.BlockSpec(memory_space=pltpu.SMEM),
                  pl.BlockSpec((1, wide, HD), cur),
                  pl.BlockSpec((1, w, kvw), prev),
                  pl.BlockSpec((1, wide, kvw), cur),
                  pl.BlockSpec((1, w, kvw), prev),
                  pl.BlockSpec((1, wide, kvw), cur),
                  pl.BlockSpec(bias_b.shape, lambda b, n: (0, 0, 0, 0))],
        out_specs=pl.BlockSpec((1, wide, HD), cur),
        out_shape=jax.ShapeDtypeStruct((bsz, L, HD), BF16),
        compiler_params=_cparams("parallel", "parallel"),
        name="b_attn",
    )(sinks.reshape(1, N_HEADS), q, k, k, v, v, bias_b)


FF_CHUNK = 256


def _out_ffn_kernel(o_ref, wo_ref, bo_ref, g1_ref, x_ref, g_ref, sh_ref, sc_ref, gate_ref,
                    w1_ref, w3_ref, w2_ref, fg_ref, y_ref, *, final_norm):
    mix = jnp.dot(o_ref[0], wo_ref[...], preferred_element_type=F32) + bo_ref[...]
    x = x_ref[0] + g1_ref[0] * mix
    h = _normmod(x, g_ref[...], sh_ref[0], sc_ref[0]).astype(BF16)
    acc = jnp.zeros(x.shape, F32)
    for c in range(D_FF // FF_CHUNK):
        cs = slice(c * FF_CHUNK, (c + 1) * FF_CHUNK)
        a1 = jnp.dot(h, w1_ref[:, cs], preferred_element_type=F32)
        a3 = jnp.dot(h, w3_ref[:, cs], preferred_element_type=F32)
        act = (a1 * jax.nn.sigmoid(a1)) * a3
        acc = acc + jnp.dot(act.astype(BF16), w2_ref[cs, :], preferred_element_type=F32)
    y = x + gate_ref[0] * acc
    if final_norm:
        y = (y * lax.rsqrt(jnp.mean(y * y, axis=-1, keepdims=True) + RMS_EPS)) * fg_ref[...]
    y_ref[0] = y


def _out_ffn(o, w_out, mix_layer, b_out, g1, x, g, sh, sc, gate, w1, w3, w2, layer, final_g,
             final_norm, tm=512):
    bsz, L, d = x.shape
    row = lambda b_, i: (b_, i, 0)
    per_b = lambda b_, i: (b_, 0, 0)
    const2 = lambda b_, i: (0, 0)
    return pl.pallas_call(
        functools.partial(_out_ffn_kernel, final_norm=final_norm),
        grid=(bsz, L // tm),
        in_specs=[pl.BlockSpec((1, tm, HD), row),
                  _layer_spec(w_out, mix_layer),
                  pl.BlockSpec((1, d), const2),
                  pl.BlockSpec((1, 1, d), per_b),
                  pl.BlockSpec((1, tm, d), row),
                  pl.BlockSpec((1, d), const2),
                  pl.BlockSpec((1, 1, d), per_b),
                  pl.BlockSpec((1, 1, d), per_b),
                  pl.BlockSpec((1, 1, d), per_b),
                  _layer_spec(w1, layer),
                  _layer_spec(w3, layer),
                  _layer_spec(w2, layer),
                  pl.BlockSpec((1, d), const2)],
        out_specs=pl.BlockSpec((1, tm, d), row),
        out_shape=jax.ShapeDtypeStruct((bsz, L, d), F32),
        compiler_params=_cparams("parallel", "parallel"),
        name="out_ffn",
    )(o, w_out, b_out, g1, x, g, sh, sc, gate, w1, w3, w2, final_g)


def kernel(x, c, rel_bias, w_ada, b_ada, norm_mix_g, norm_ffn_g, a_w_in, a_kv_norm_g, a_w_uk,
           a_w_uv, a_idx_k_g, a_idx_k_b, a_w_out, b_w_in, b_b_in, b_sinks, b_w_out, b_b_out,
           ffn_w1, ffn_w3, ffn_w2, norm_final_g):
    bsz, L, d = x.shape
    depth = w_ada.shape[0]
    topk = min(INDEX_TOPK, L // 4)

    mod = _adaln(c, w_ada, b_ada)

    r = jnp.arange(BIAS_BLK)
    ids_a = jnp.stack([_t5_bucket(r[:, None] - r[None, :]),
                       _t5_bucket(r[:, None] - r[None, :] + BIAS_BLK)])
    bias_a = _bias_tiles(rel_bias, ids_a, shift_far=True)
    dist_b = jnp.arange(WINDOW)[:, None] + WINDOW - jnp.arange(2 * WINDOW)[None, :]
    in_window = (dist_b >= 0) & (dist_b < WINDOW)
    ids_b = jnp.where(in_window, _t5_bucket(dist_b), -1)
    ids_b0 = jnp.where(jnp.arange(2 * WINDOW)[None, :] >= WINDOW, ids_b, -1)
    bias_b = _bias_tiles(rel_bias, jnp.stack([ids_b, ids_b0]), shift_far=False)
    pairs = N_HEADS // KV_HEADS // 2
    bias_b = bias_b.reshape(KV_HEADS, pairs, 2, 2, WINDOW, 2 * WINDOW).transpose(3, 0, 1, 4, 2, 5)
    bias_b = bias_b.reshape(2, KV_HEADS, pairs * WINDOW, 4 * WINDOW)

    a_w_main = jnp.pad(a_w_in[:, :, :A_MAIN], ((0, 0), (0, 0), (0, A_MAIN_PAD - A_MAIN))).astype(BF16)
    a_w_wi_t = jnp.pad(a_w_in[:, :, A_WI0:A_WI0 + IDX_HEADS].transpose(0, 2, 1),
                       ((0, 0), (0, 16 - IDX_HEADS), (0, 0))).astype(BF16)
    a_w_uk, a_w_uv, a_w_out, b_w_in, b_w_out, ffn_w1, ffn_w3, ffn_w2 = [
        w.astype(BF16) for w in (jnp.swapaxes(_pair_blocks(a_w_uk), -1, -2),
                                 _pair_blocks(a_w_uv), a_w_out, b_w_in,
                                 b_w_out, ffn_w1, ffn_w3, ffn_w2)]

    zero_bias = jnp.zeros((1, d), F32)
    for i in range(depth):
        sh1, sc1, g1, sh2, sc2, g2 = [m.reshape(bsz, 1, d) for m in jnp.split(mod[i], 6, axis=-1)]
        jm = i // 2
        if i % 2 == 0:
            q, ckv, ckvt, qi, ki, wit = _a_inproj(
                x, norm_mix_g[i][None], sh1, sc1, a_w_main, a_w_wi_t, jm, a_kv_norm_g[jm][None],
                a_idx_k_g[jm][None], a_idx_k_b[jm][None])
            mask = _a_index(qi, wit, ki, topk)
            o = _a_attn(q, ckv, ckvt, mask, a_w_uk, a_w_uv, jm, bias_a)
            w_out, b_out = a_w_out, zero_bias
        else:
            q, k, v = _b_inproj(x, norm_mix_g[i][None], sh1, sc1, b_w_in, jm, b_b_in[jm][None])
            o = _b_attn(q, k, v, b_sinks[jm], bias_b)
            w_out, b_out = b_w_out, b_b_out[jm][None]
        x = _out_ffn(o, w_out, jm, b_out, g1, x, norm_ffn_g[i][None], sh2, sc2, g2,
                     ffn_w1, ffn_w3, ffn_w2, i, norm_final_g[None], final_norm=(i == depth - 1))
    return x
```

```python
import functools
import math

import numpy as np
import jax
import jax.numpy as jnp
from jax import lax
from jax.experimental import pallas as pl
from jax.experimental.pallas import tpu as pltpu

D_MODEL = 1024
N_HEADS = 16
HEAD_DIM = 64
KV_RANK = 256
IDX_HEADS = 8
IDX_DIM = 64
INDEX_TOPK = 256
KV_HEADS = 2
WINDOW = 128
N_BUCKETS = 32
MAX_DISTANCE = 128
D_FF = 2816
RMS_EPS = 1e-6
NEG = -1e30

HD = N_HEADS * HEAD_DIM
A_Q0, A_KV0, A_QI0, A_KI0, A_WI0 = 0, HD, HD + KV_RANK, HD + KV_RANK + IDX_HEADS * IDX_DIM, \
    HD + KV_RANK + IDX_HEADS * IDX_DIM + IDX_DIM
A_MAIN = A_WI0
A_MAIN_PAD = 1920
B_IN = (N_HEADS + 2 * KV_HEADS) * HEAD_DIM

CHUNK = 256
PV_ROWS = 64
ACC_ROWS = 256
HALF_ROWS = 16
LANES = 128
BIAS_BLK = 128
VMEM_LIMIT = 56 * 1024 * 1024

F32 = jnp.float32
BF16 = jnp.bfloat16
I32 = jnp.int32
I16 = jnp.int16

_NT = (((1,), (1,)), ((), ()))


def _cparams(*sem):
    return pltpu.CompilerParams(dimension_semantics=sem, vmem_limit_bytes=VMEM_LIMIT)


def _pair_blocks(w):
    even, odd = w[..., 0::2, :, :], w[..., 1::2, :, :]
    zero = jnp.zeros_like(even)
    return jnp.concatenate([jnp.concatenate([even, zero], axis=-1),
                            jnp.concatenate([zero, odd], axis=-1)], axis=-2)


def _layer_spec(stack, layer):
    zeros = (0,) * (stack.ndim - 1)
    return pl.BlockSpec((None,) + stack.shape[1:], lambda *_: (layer,) + zeros)


def _f32_key(v):
    b = int(np.array(v, np.float32).view(np.int32))
    return b ^ ((b >> 31) & 0x7FFFFFFF)


LOG2E = math.log2(math.e)
KEY_NEG = _f32_key(NEG)
INT_MIN = -(2 ** 31)


def _adaln_kernel(c_ref, w_ref, b_ref, o_ref):
    c = c_ref[...]
    cs = c * jax.nn.sigmoid(c)
    o_ref[0] = jnp.dot(cs, w_ref[0], preferred_element_type=F32,
                       precision=lax.Precision.HIGHEST) + b_ref[0]


def _adaln(c, w_ada, b_ada):
    depth, d, n = w_ada.shape
    bsz = c.shape[0]
    tn = 1536
    return pl.pallas_call(
        _adaln_kernel,
        grid=(depth, n // tn),
        in_specs=[pl.BlockSpec((bsz, d), lambda i, j: (0, 0)),
                  pl.BlockSpec((1, d, tn), lambda i, j: (i, 0, j)),
                  pl.BlockSpec((1, 1, tn), lambda i, j: (i, 0, j))],
        out_specs=pl.BlockSpec((1, bsz, tn), lambda i, j: (i, 0, j)),
        out_shape=jax.ShapeDtypeStruct((depth, bsz, n), F32),
        compiler_params=_cparams("parallel", "parallel"),
        name="adaln",
    )(c, w_ada, b_ada.reshape(depth, 1, n))


def _normmod(x, g, sh, sc):
    ms = jnp.mean(x * x, axis=-1, keepdims=True)
    y = (x * lax.rsqrt(ms + RMS_EPS)) * g
    return y * (1.0 + sc) + sh


def _a_inproj_kernel(x_ref, g_ref, sh_ref, sc_ref, w_ref, wwi_ref, kvg_ref, ikg_ref, ikb_ref,
                     q_ref, ckv_ref, ckvt_ref, qi_ref, ki_ref, wit_ref):
    h = _normmod(x_ref[0], g_ref[...], sh_ref[0], sc_ref[0]).astype(BF16)
    proj = jnp.dot(h, w_ref[...], preferred_element_type=F32)
    q_ref[0] = (proj[:, A_Q0:A_KV0] * (HEAD_DIM ** -0.5)).astype(BF16)
    ckv = proj[:, A_KV0:A_QI0]
    ckv = (ckv * lax.rsqrt(jnp.mean(ckv * ckv, axis=-1, keepdims=True) + RMS_EPS)) * kvg_ref[...]
    ckv_ref[0] = ckv.astype(BF16)
    for c in range(ckvt_ref.shape[1]):
        ckvt_ref[0, c] = ckv[c * CHUNK:(c + 1) * CHUNK].T.astype(BF16)
    qi_ref[0] = proj[:, A_QI0:A_KI0].astype(BF16)
    ki = proj[:, A_KI0:A_WI0]
    mu = jnp.mean(ki, axis=-1, keepdims=True)
    var = jnp.mean(jnp.square(ki - mu), axis=-1, keepdims=True)
    ki = ((ki - mu) * lax.rsqrt(var + RMS_EPS)) * ikg_ref[...] + ikb_ref[...]
    ki_ref[0] = ki.astype(BF16)
    wit = lax.dot_general(wwi_ref[...], h, _NT, preferred_element_type=F32)
    wit_ref[0] = wit[:IDX_HEADS] * (IDX_HEADS ** -0.5 * IDX_DIM ** -0.5)


def _a_inproj(x, g, sh, sc, w_main, w_wi_t, layer, kv_g, ik_g, ik_b, tm=512):
    bsz, L, d = x.shape
    row = lambda b, i: (b, i, 0)
    per_b = lambda b, i: (b, 0, 0)
    const2 = lambda b, i: (0, 0)
    return pl.pallas_call(
        _a_inproj_kernel,
        grid=(bsz, L // tm),
        in_specs=[pl.BlockSpec((1, tm, d), row),
                  pl.BlockSpec((1, d), const2),
                  pl.BlockSpec((1, 1, d), per_b),
                  pl.BlockSpec((1, 1, d), per_b),
                  _layer_spec(w_main, layer),
                  _layer_spec(w_wi_t, layer),
                  pl.BlockSpec((1, KV_RANK), const2),
                  pl.BlockSpec((1, IDX_DIM), const2),
                  pl.BlockSpec((1, IDX_DIM), const2)],
        out_specs=[pl.BlockSpec((1, tm, HD), row),
                   pl.BlockSpec((1, tm, KV_RANK), row),
                   pl.BlockSpec((1, tm // CHUNK, KV_RANK, CHUNK), lambda b, i: (b, i, 0, 0)),
                   pl.BlockSpec((1, tm, IDX_HEADS * IDX_DIM), row),
                   pl.BlockSpec((1, tm, IDX_DIM), row),
                   pl.BlockSpec((1, IDX_HEADS, tm), lambda b, i: (b, 0, i))],
        out_shape=[jax.ShapeDtypeStruct((bsz, L, HD), BF16),
                   jax.ShapeDtypeStruct((bsz, L, KV_RANK), BF16),
                   jax.ShapeDtypeStruct((bsz, L // CHUNK, KV_RANK, CHUNK), BF16),
                   jax.ShapeDtypeStruct((bsz, L, IDX_HEADS * IDX_DIM), BF16),
                   jax.ShapeDtypeStruct((bsz, L, IDX_DIM), BF16),
                   jax.ShapeDtypeStruct((bsz, IDX_HEADS, L), F32)],
        compiler_params=_cparams("parallel", "parallel"),
        name="a_inproj",
    )(x, g, sh, sc, w_main, w_wi_t, kv_g, ik_g, ik_b)


def _a_index_kernel(qi_ref, wit_ref, ki_ref, tri_ref, mask_ref, key_scr, half_scr, *, topk,
                    seq_len):
    j = pl.program_id(1)
    nchunk = seq_len // CHUNK
    qi = qi_ref[0]
    wit = wit_ref[0]
    t_glob = j * CHUNK + lax.broadcasted_iota(I32, (CHUNK, CHUNK), 1)
    s_loc = lax.broadcasted_iota(I32, (CHUNK, CHUNK), 0)

    def chunks(fn, init):
        def pair(i, carry):
            return fn(2 * i + 1, fn(2 * i, carry))
        carry = lax.fori_loop(0, lax.shift_right_logical(j + 1, 1), pair, init)
        return lax.cond((j & 1) == 0, lambda c: fn(j, c), lambda c: c, carry)

    def score_chunk(kc, carry):
        kik = ki_ref[0, pl.ds(pl.multiple_of(kc * CHUNK, CHUNK), CHUNK), :]
        acc = jnp.zeros((CHUNK, CHUNK), F32)
        for h in range(IDX_HEADS):
            r = lax.dot_general(kik, qi[:, h * IDX_DIM:(h + 1) * IDX_DIM], _NT,
                                preferred_element_type=F32)
            acc = acc + jnp.maximum(r, 0.0) * wit[h:h + 1, :]
        sc = jnp.where(kc * CHUNK + s_loc <= t_glob, acc, NEG)
        bits = lax.bitcast_convert_type(sc, I32)
        key = bits ^ ((bits >> 31) & 0x7FFFFFFF)
        key_scr[kc] = key
        half_scr[kc] = (key >> 16).astype(I16)
        return carry

    chunks(score_chunk, 0)

    n_beyond = seq_len - (j + 1) * CHUNK
    ge = lambda a, b: a >= b
    gt = lambda a, b: a > b

    def count(pred, thr):
        def body(kc, acc):
            m = jnp.where(pred(key_scr[kc], thr), jnp.int32(1), jnp.int32(0))
            return acc + m.reshape(CHUNK // 8, 8, CHUNK).sum(axis=0)
        acc = chunks(body, jnp.zeros((8, CHUNK), I32))
        cnt = acc.sum(axis=0, keepdims=True)
        return cnt + jnp.where(pred(KEY_NEG, thr), n_beyond, 0)

    def count_half(pred, thr16):
        t = jnp.broadcast_to(thr16, (HALF_ROWS, CHUNK)).astype(I16)
        def body(kc, acc):
            m = jnp.where(pred(half_scr[kc].reshape(CHUNK // HALF_ROWS, HALF_ROWS, CHUNK), t[None]),
                          jnp.int16(1), jnp.int16(0))
            for r in range(CHUNK // HALF_ROWS):
                acc = acc + m[r]
            return acc
        acc = chunks(body, jnp.zeros((HALF_ROWS, CHUNK), I16))
        return acc.astype(I32).sum(axis=0, keepdims=True)

    def beyond(cand):
        return jnp.where(KEY_NEG >= cand, n_beyond, 0)

    lowest = jnp.full((1, CHUNK), INT_MIN, I32)

    def bisect():
        zero = jnp.zeros((1, CHUNK), I32)
        thr = jnp.where(count_half(ge, zero) + beyond(zero) >= topk, zero, lowest)

        def bisect_high(i, thr):
            cand = thr + lax.shift_left(jnp.int32(1), 30 - i)
            cnt = count_half(ge, cand >> 16) + beyond(cand)
            return jnp.where(cnt >= topk, cand, thr)

        thr = lax.fori_loop(0, 15, bisect_high, thr)

        thr_hi = thr >> 16
        n_above = count_half(gt, thr_hi)

        def low_half(kc, carry):
            key = key_scr[kc]
            low = (key & 0xFFFF) - 0x8000
            half_scr[kc] = jnp.where((key >> 16) == thr_hi, low, -0x8000).astype(I16)
            return carry

        chunks(low_half, 0)

        def bisect_low(i, thr):
            cand = thr + lax.shift_left(jnp.int32(1), 15 - i)
            cnt = n_above + count_half(ge, (cand & 0xFFFF) - 0x8000) + beyond(cand)
            return jnp.where(cnt >= topk, cand, thr)

        return lax.fori_loop(0, 16, bisect_low, thr)

    takes_all = (j + 1) * CHUNK <= topk
    thr = lax.cond(takes_all, lambda: lowest, bisect)

    need = (topk - count(gt, thr)).astype(F32)
    tri = tri_ref[...]

    def emit_ranked(kc, seen):
        key = key_scr[kc]
        eq = jnp.where(key == thr, 1.0, 0.0)
        rank = jnp.dot(tri, eq.astype(BF16), preferred_element_type=F32) + seen
        take = jnp.where(key > thr, 1.0, jnp.where(rank < need, eq, 0.0))
        causal = kc * CHUNK + s_loc <= t_glob
        m = jnp.where(causal, jnp.where(take > 0.5, 0.0, NEG), NEG)
        mask_ref[0, kc] = m.T
        return seen + jnp.sum(eq, axis=0, keepdims=True)

    def emit_plain(kc, seen):
        causal = kc * CHUNK + s_loc <= t_glob
        m = jnp.where(causal, jnp.where(key_scr[kc] >= thr, 0.0, NEG), NEG)
        mask_ref[0, kc] = m.T
        return seen

    exact_fit = takes_all | (jnp.max(count(ge, thr)) <= topk)
    lax.cond(exact_fit, lambda: chunks(emit_plain, jnp.zeros((1, CHUNK), F32)),
             lambda: chunks(emit_ranked, jnp.zeros((1, CHUNK), F32)))

    def fill(kc, carry):
        mask_ref[0, kc] = jnp.full((CHUNK, CHUNK), NEG, F32)
        return carry

    lax.fori_loop(j + 1, nchunk, fill, 0)


def _a_index(qi, wit, ki, topk):
    bsz, L, _ = qi.shape
    nchunk = L // CHUNK
    tri = jnp.tril(jnp.ones((CHUNK, CHUNK), BF16), -1)
    return pl.pallas_call(
        functools.partial(_a_index_kernel, topk=topk, seq_len=L),
        grid=(bsz, nchunk),
        in_specs=[pl.BlockSpec((1, CHUNK, IDX_HEADS * IDX_DIM), lambda b, j: (b, j, 0)),
                  pl.BlockSpec((1, IDX_HEADS, CHUNK), lambda b, j: (b, 0, j)),
                  pl.BlockSpec((1, L, IDX_DIM), lambda b, j: (b, 0, 0)),
                  pl.BlockSpec((CHUNK, CHUNK), lambda b, j: (0, 0))],
        out_specs=pl.BlockSpec((1, nchunk, CHUNK, CHUNK), lambda b, j: (b, 0, j, 0)),
        out_shape=jax.ShapeDtypeStruct((bsz, nchunk, L, CHUNK), F32),
        scratch_shapes=[pltpu.VMEM((nchunk, CHUNK, CHUNK), I32),
                        pltpu.VMEM((nchunk, CHUNK, CHUNK), I16)],
        compiler_params=_cparams("parallel", "parallel"),
        name="a_index",
    )(qi, wit, ki, tri)


def _a_attn_kernel(q_ref, ckv_ref, ckvt_ref, mask_ref, wuk_ref, wuv_ref, bias_ref, o_ref,
                   qabs_scr, lg_scr, m_scr, l_scr, acc_scr, p_scr, *, hc, nq, n_work):
    s = pl.program_id(0)
    j = (jnp.minimum(s, n_work - 1) // (N_HEADS // hc)) % nq
    slot = s & 1
    tq = CHUNK

    def placeholder(r, carry):
        rows = pl.ds(pl.multiple_of(r * ACC_ROWS, ACC_ROWS), ACC_ROWS)
        l_scr[1, rows, :] = jnp.ones((ACC_ROWS, LANES), F32)
        acc_scr[1, rows, :] = jnp.zeros((ACC_ROWS, KV_RANK), F32)
        return carry

    lax.fori_loop(0, jnp.where(s == 0, hc * tq // ACC_ROWS, 0), placeholder, 0)

    q = q_ref[0]
    for i in range(hc // 2):
        qa = jnp.dot(q[:, i * 2 * HEAD_DIM:(i + 1) * 2 * HEAD_DIM], wuk_ref[i],
                     preferred_element_type=F32)
        for par in range(2):
            qabs_scr[(2 * i + par) * tq:(2 * i + par + 1) * tq, :] = (
                qa[:, par * KV_RANK:(par + 1) * KV_RANK] * LOG2E).astype(BF16)
    o = (acc_scr[1 - slot] / jnp.sum(l_scr[1 - slot], axis=-1, keepdims=True)).astype(BF16)
    outs = [jnp.dot(jnp.concatenate([o[2 * i * tq:(2 * i + 1) * tq],
                                     o[(2 * i + 1) * tq:(2 * i + 2) * tq]], axis=1),
                    wuv_ref[i], preferred_element_type=F32) for i in range(hc // 2)]
    o_ref[0] = jnp.concatenate(outs, axis=1).astype(BF16)
    qg = qabs_scr[...]

    n_far_pairs = lax.shift_right_arithmetic(j - 1, 1)
    odd = (j & 1) == 0

    def keys(kc, width):
        return ckv_ref[0, pl.ds(kc, width)].reshape(width * CHUNK, KV_RANK)

    def logits(kc, width):
        keys_t = jnp.concatenate([ckvt_ref[0, kc + w] for w in range(width)], axis=1)
        lg = jnp.dot(qg, keys_t, preferred_element_type=F32)
        mk = jnp.concatenate([mask_ref[0, kc + w] for w in range(width)], axis=1)
        return lg + jnp.concatenate([mk] * hc, axis=0)

    def near_bias(i, with_prev):
        d0, d1 = bias_ref[i, 0], bias_ref[i, 1]
        z = jnp.zeros_like(d0)
        top, bot = [d0, z], [d1, d0]
        if with_prev:
            top, bot = [z, d1] + top, [z, z] + bot
        return jnp.concatenate([jnp.concatenate(top, axis=1), jnp.concatenate(bot, axis=1)], axis=0)

    def put_logits(kc, width, lg):
        for w in range(width):
            lg_scr[kc + w] = lg[:, w * CHUNK:(w + 1) * CHUNK]
        m = m_scr[...]
        for c in range(width * CHUNK // LANES):
            m = jnp.maximum(m, lg[:, c * LANES:(c + 1) * LANES])
        m_scr[...] = m

    m_scr[...] = jnp.full(m_scr.shape, -jnp.inf, F32)

    @pl.when(j >= 1)
    def _():
        bias = jnp.concatenate([near_bias(i, True) for i in range(hc)], axis=0)
        put_logits(j - 1, 2, logits(j - 1, 2) + bias)

    def far_pair(i, carry):
        kc = j - 3 - 2 * i
        put_logits(kc, 2, logits(kc, 2))
        return carry

    lax.fori_loop(0, n_far_pairs, far_pair, 0)

    @pl.when(odd)
    def _():
        bias = jnp.concatenate([near_bias(i, False) for i in range(hc)], axis=0)
        put_logits(0, 1, logits(0, 1) + jnp.where(j == 0, bias, 0.0))

    l_scr[slot] = jnp.zeros(l_scr.shape[1:], F32)
    acc_scr[slot] = jnp.zeros(acc_scr.shape[1:], F32)

    def pv(kc, width, reduce_max):
        ck = keys(kc, width)
        for piece in range(hc * tq // ACC_ROWS):
            for r in range(piece * ACC_ROWS // PV_ROWS, (piece + 1) * ACC_ROWS // PV_ROWS):
                rows = slice(r * PV_ROWS, (r + 1) * PV_ROWS)
                m = m_scr[rows, :]
                if reduce_max:
                    m = jnp.broadcast_to(jnp.max(m, axis=-1, keepdims=True), m.shape)
                    m_scr[rows, :] = m
                l = l_scr[slot, rows, :]
                for w in range(width):
                    for c in range(CHUNK // LANES):
                        col = w * CHUNK + c * LANES
                        p = jnp.exp2(lg_scr[kc + w, rows, c * LANES:(c + 1) * LANES] - m)
                        l = l + p
                        p_scr[rows, col:col + LANES] = p.astype(BF16)
                l_scr[slot, rows, :] = l
            rows = slice(piece * ACC_ROWS, (piece + 1) * ACC_ROWS)
            acc_scr[slot, rows, :] += jnp.dot(p_scr[rows, :width * CHUNK], ck,
                                              preferred_element_type=F32)

    @pl.when(j >= 1)
    def _():
        pv(j - 1, 2, True)

    def far_pv(i, carry):
        pv(j - 3 - 2 * i, 2, False)
        return carry

    lax.fori_loop(0, n_far_pairs, far_pv, 0)

    @pl.when(odd)
    def _():
        pv(0, 1, True)


def _a_attn(q, ckv, ckvt, mask, w_uk, w_uv, layer, bias_nd, hc=8):
    bsz, L, _ = q.shape
    nchunk = L // CHUNK
    hg = N_HEADS // hc
    m = hc * CHUNK
    ckv4 = ckv.reshape(bsz, nchunk, CHUNK, KV_RANK)
    n_work = bsz * nchunk * hg

    def item(s):
        return s // (nchunk * hg), (s // hg) % nchunk, s % hg

    cur = lambda s: item(jnp.minimum(s, n_work - 1))
    prev = lambda s: item(jnp.maximum(s - 1, 0))
    return pl.pallas_call(
        functools.partial(_a_attn_kernel, hc=hc, nq=nchunk, n_work=n_work),
        grid=(n_work + 1,),
        in_specs=[pl.BlockSpec((1, CHUNK, hc * HEAD_DIM), lambda s: cur(s)),
                  pl.BlockSpec((1, nchunk, CHUNK, KV_RANK), lambda s: (cur(s)[0], 0, 0, 0)),
                  pl.BlockSpec((1, nchunk, KV_RANK, CHUNK), lambda s: (cur(s)[0], 0, 0, 0)),
                  pl.BlockSpec((1, nchunk, CHUNK, CHUNK), lambda s: (cur(s)[0], 0, cur(s)[1], 0)),
                  pl.BlockSpec((None, hc // 2) + w_uk.shape[2:], lambda s: (layer, cur(s)[2], 0, 0)),
                  pl.BlockSpec((None, hc // 2) + w_uv.shape[2:], lambda s: (layer, prev(s)[2], 0, 0)),
                  pl.BlockSpec((hc, 2, BIAS_BLK, BIAS_BLK), lambda s: (cur(s)[2], 0, 0, 0))],
        out_specs=pl.BlockSpec((1, CHUNK, hc * HEAD_DIM), lambda s: prev(s)),
        out_shape=jax.ShapeDtypeStruct((bsz, L, HD), BF16),
        scratch_shapes=[pltpu.VMEM((m, KV_RANK), BF16),
                        pltpu.VMEM((nchunk, m, CHUNK), F32),
                        pltpu.VMEM((m, LANES), F32),
                        pltpu.VMEM((2, m, LANES), F32),
                        pltpu.VMEM((2, m, KV_RANK), F32),
                        pltpu.VMEM((m, 2 * CHUNK), BF16)],
        compiler_params=_cparams("arbitrary"),
        name="a_attn",
    )(q, ckv4, ckvt, mask, w_uk, w_uv, bias_nd)


def _bias_kernel(rb_ref, ids_ref, o_ref, *, n_tiles, far_bucket, shift_far):
    h = pl.program_id(0)
    far = rb_ref[far_bucket, h] if shift_far else 0.0
    for t in range(n_tiles):
        ids = ids_ref[t]
        out = jnp.zeros(ids.shape, F32)
        for b in range(N_BUCKETS):
            out = jnp.where(ids == b, rb_ref[b, h] - far, out)
        o_ref[0, t] = jnp.where(ids < 0, NEG, out * LOG2E)


def _t5_bucket(dist):
    max_exact = N_BUCKETS // 2
    d = jnp.maximum(dist, 0)
    large = max_exact + (jnp.log(jnp.maximum(d, 1).astype(F32) / max_exact)
                         / math.log(MAX_DISTANCE / max_exact)
                         * (N_BUCKETS - max_exact)).astype(I32)
    large = jnp.minimum(large, N_BUCKETS - 1)
    return jnp.where(d < max_exact, d, large)


def _bias_tiles(rel_bias, ids, shift_far):
    n_tiles, r, c = ids.shape
    return pl.pallas_call(
        functools.partial(_bias_kernel, n_tiles=n_tiles, far_bucket=N_BUCKETS - 1,
                          shift_far=shift_far),
        grid=(N_HEADS,),
        in_specs=[pl.BlockSpec(memory_space=pltpu.SMEM),
                  pl.BlockSpec((n_tiles, r, c), lambda h: (0, 0, 0))],
        out_specs=pl.BlockSpec((1, n_tiles, r, c), lambda h: (h, 0, 0, 0)),
        out_shape=jax.ShapeDtypeStruct((N_HEADS, n_tiles, r, c), F32),
        compiler_params=_cparams("parallel"),
        name="bias_tiles",
    )(rel_bias, ids)


def _b_inproj_kernel(x_ref, g_ref, sh_ref, sc_ref, w_ref, b_ref, q_ref, k_ref, v_ref):
    h = _normmod(x_ref[0], g_ref[...], sh_ref[0], sc_ref[0]).astype(BF16)
    proj = jnp.dot(h, w_ref[...], preferred_element_type=F32) + b_ref[...]
    q_ref[0] = (proj[:, :HD] * (HEAD_DIM ** -0.5 * LOG2E)).astype(BF16)
    k_ref[0] = proj[:, HD:HD + KV_HEADS * HEAD_DIM].astype(BF16)
    v_ref[0] = proj[:, HD + KV_HEADS * HEAD_DIM:].astype(BF16)


def _b_inproj(x, g, sh, sc, w, layer, b, tm=512):
    bsz, L, d = x.shape
    kvw = KV_HEADS * HEAD_DIM
    row = lambda b_, i: (b_, i, 0)
    per_b = lambda b_, i: (b_, 0, 0)
    const2 = lambda b_, i: (0, 0)
    return pl.pallas_call(
        _b_inproj_kernel,
        grid=(bsz, L // tm),
        in_specs=[pl.BlockSpec((1, tm, d), row),
                  pl.BlockSpec((1, d), const2),
                  pl.BlockSpec((1, 1, d), per_b),
                  pl.BlockSpec((1, 1, d), per_b),
                  _layer_spec(w, layer),
                  pl.BlockSpec((1, B_IN), const2)],
        out_specs=[pl.BlockSpec((1, tm, HD), row),
                   pl.BlockSpec((1, tm, kvw), row),
                   pl.BlockSpec((1, tm, kvw), row)],
        out_shape=[jax.ShapeDtypeStruct((bsz, L, HD), BF16),
                   jax.ShapeDtypeStruct((bsz, L, kvw), BF16),
                   jax.ShapeDtypeStruct((bsz, L, kvw), BF16)],
        compiler_params=_cparams("parallel", "parallel"),
        name="b_inproj",
    )(x, g, sh, sc, w, b)


def _b_window_attention(sink_ref, q_block, kp, kc, vp, vc, bias_ref, first, n_blocks, store):
    w = WINDOW
    pairs = N_HEADS // KV_HEADS // 2
    kall = jnp.concatenate([kp, kc], axis=0).astype(F32)
    vall = jnp.concatenate([vp, vc], axis=0).astype(F32)
    low = lax.broadcasted_iota(I32, kall.shape, 1) < HEAD_DIM

    def padded(x, g):
        swapped = pltpu.roll(x, HEAD_DIM, axis=1)
        on_low, on_high = (x, swapped) if g == 0 else (swapped, x)
        return (jnp.where(low, on_low, 0.0).astype(BF16), jnp.where(low, 0.0, on_high).astype(BF16))

    kpad = [padded(kall, g) for g in range(KV_HEADS)]
    vpad = [padded(vall, g) for g in range(KV_HEADS)]
    for blk in range(n_blocks):
        variant = jnp.where(first, 1, 0) if blk == 0 else 0
        keys = slice(blk * w, (blk + 2) * w)
        q = q_block(blk)
        outs = []
        for g in range(KV_HEADS):
            blocks = [q[:, (pairs * g + p) * 2 * HEAD_DIM:(pairs * g + p + 1) * 2 * HEAD_DIM]
                      for p in range(pairs)]
            lg = lax.dot_general(jnp.concatenate(blocks, axis=0),
                                 jnp.concatenate([kpad[g][0][keys], kpad[g][1][keys]], axis=0),
                                 _NT, preferred_element_type=F32)
            lg = lg + bias_ref[variant, g]
            probs = [[], []]
            for p in range(pairs):
                for par in range(2):
                    h = 2 * pairs * g + 2 * p + par
                    t = lg[p * w:(p + 1) * w, par * 2 * w:(par + 1) * 2 * w]
                    sink = sink_ref[0, h] * LOG2E
                    m = jnp.maximum(jnp.max(t, axis=-1, keepdims=True), sink)
                    e = jnp.exp2(t - m)
                    denom = jnp.sum(e, axis=-1, keepdims=True) + jnp.exp2(sink - m)
                    probs[par].append((e * (1.0 / denom)).astype(BF16))
            og = (jnp.dot(jnp.concatenate(probs[0], axis=0), vpad[g][0][keys],
                          preferred_element_type=F32)
                  + jnp.dot(jnp.concatenate(probs[1], axis=0), vpad[g][1][keys],
                            preferred_element_type=F32))
            outs += [og[p * w:(p + 1) * w] for p in range(pairs)]
        store(blk, jnp.concatenate(outs, axis=1).astype(BF16))


FF_CHUNK = 256


def _out_ffn_tile(o, wo_ref, bo_ref, g1_ref, x_ref, g_ref, sh_ref, sc_ref, gate_ref,
                  w1_ref, w3_ref, w2_ref, fg_ref, y_ref, final_norm):
    mix = jnp.dot(o, wo_ref[...], preferred_element_type=F32) + bo_ref[...]
    x = x_ref[0] + g1_ref[0] * mix
    h = _normmod(x, g_ref[...], sh_ref[0], sc_ref[0]).astype(BF16)
    acc = jnp.zeros(x.shape, F32)
    for c in range(D_FF // FF_CHUNK):
        cs = slice(c * FF_CHUNK, (c + 1) * FF_CHUNK)
        a1 = jnp.dot(h, w1_ref[:, cs], preferred_element_type=F32)
        a3 = jnp.dot(h, w3_ref[:, cs], preferred_element_type=F32)
        act = (a1 * jax.nn.sigmoid(a1)) * a3
        acc = acc + jnp.dot(act.astype(BF16), w2_ref[cs, :], preferred_element_type=F32)
    y = x + gate_ref[0] * acc
    if final_norm:
        y = (y * lax.rsqrt(jnp.mean(y * y, axis=-1, keepdims=True) + RMS_EPS)) * fg_ref[...]
    y_ref[0] = y


def _out_ffn_kernel(o_ref, *refs, final_norm):
    _out_ffn_tile(o_ref[0], *refs, final_norm)


def _b_attn_ffn_kernel(sink_ref, q_ref, kp_ref, kc_ref, vp_ref, vc_ref, bias_ref, *refs,
                       final_norm, n_blocks, tiles_per_seq, n_work):
    ffn_refs, o_scr = refs[:-1], refs[-1]
    s = pl.program_id(0)
    slot = s & 1
    w = WINDOW

    def placeholder(r, carry):
        o_scr[1, pl.ds(pl.multiple_of(r * w, w), w), :] = jnp.zeros((w, HD), BF16)
        return carry

    lax.fori_loop(0, jnp.where(s == 0, n_blocks, 0), placeholder, 0)

    def store(blk, value):
        o_scr[slot, blk * w:(blk + 1) * w, :] = value

    _out_ffn_tile(o_scr[1 - slot], *ffn_refs, final_norm)
    first = lax.rem(jnp.minimum(s, n_work - 1), tiles_per_seq) == 0
    _b_window_attention(sink_ref, lambda blk: q_ref[0, blk * w:(blk + 1) * w, :], kp_ref[0],
                        kc_ref[0], vp_ref[0], vc_ref[0], bias_ref, first, n_blocks, store)


def _b_attn_ffn(q, k, v, sinks, bias_b, w_out, mix_layer, b_out, g1, x, g, sh, sc, gate, w1, w3,
                w2, layer, final_g, final_norm, tm=512):
    bsz, L, d = x.shape
    w = WINDOW
    kvw = KV_HEADS * HEAD_DIM
    tiles = L // tm
    n_work = bsz * tiles

    def tile(s):
        return s // tiles, s % tiles

    cur = lambda s: tile(jnp.minimum(s, n_work - 1))
    prev = lambda s: tile(jnp.maximum(s - 1, 0))
    cur_row = lambda s: cur(s) + (0,)
    before = lambda s: (cur(s)[0], jnp.maximum(cur(s)[1] * (tm // w) - 1, 0), 0)
    prev_row = lambda s: prev(s) + (0,)
    prev_b = lambda s: (prev(s)[0], 0, 0)
    const2 = lambda s: (0, 0)
    return pl.pallas_call(
        functools.partial(_b_attn_ffn_kernel, final_norm=final_norm, n_blocks=tm // w,
                          tiles_per_seq=tiles, n_work=n_work),
        grid=(n_work + 1,),
        in_specs=[pl.BlockSpec(memory_space=pltpu.SMEM),
                  pl.BlockSpec((1, tm, HD), cur_row),
                  pl.BlockSpec((1, w, kvw), before),
                  pl.BlockSpec((1, tm, kvw), cur_row),
                  pl.BlockSpec((1, w, kvw), before),
                  pl.BlockSpec((1, tm, kvw), cur_row),
                  pl.BlockSpec(bias_b.shape, lambda s: (0, 0, 0, 0)),
                  _layer_spec(w_out, mix_layer),
                  pl.BlockSpec((1, d), const2),
                  pl.BlockSpec((1, 1, d), prev_b),
                  pl.BlockSpec((1, tm, d), prev_row),
                  pl.BlockSpec((1, d), const2),
                  pl.BlockSpec((1, 1, d), prev_b),
                  pl.BlockSpec((1, 1, d), prev_b),
                  pl.BlockSpec((1, 1, d), prev_b),
                  _layer_spec(w1, layer),
                  _layer_spec(w3, layer),
                  _layer_spec(w2, layer),
                  pl.BlockSpec((1, d), const2)],
        out_specs=pl.BlockSpec((1, tm, d), prev_row),
        out_shape=jax.ShapeDtypeStruct((bsz, L, d), F32),
        scratch_shapes=[pltpu.VMEM((2, tm, HD), BF16)],
        compiler_params=_cparams("arbitrary"),
        name="b_attn_ffn",
    )(sinks.reshape(1, N_HEADS), q, k, k, v, v, bias_b, w_out, b_out, g1, x, g, sh, sc, gate,
      w1, w3, w2, final_g)


def _out_ffn(o, w_out, mix_layer, b_out, g1, x, g, sh, sc, gate, w1, w3, w2, layer, final_g,
             final_norm, tm=512):
    bsz, L, d = x.shape
    row = lambda b_, i: (b_, i, 0)
    per_b = lambda b_, i: (b_, 0, 0)
    const2 = lambda b_, i: (0, 0)
    return pl.pallas_call(
        functools.partial(_out_ffn_kernel, final_norm=final_norm),
        grid=(bsz, L // tm),
        in_specs=[pl.BlockSpec((1, tm, HD), row),
                  _layer_spec(w_out, mix_layer),
                  pl.BlockSpec((1, d), const2),
                  pl.BlockSpec((1, 1, d), per_b),
                  pl.BlockSpec((1, tm, d), row),
                  pl.BlockSpec((1, d), const2),
                  pl.BlockSpec((1, 1, d), per_b),
                  pl.BlockSpec((1, 1, d), per_b),
                  pl.BlockSpec((1, 1, d), per_b),
                  _layer_spec(w1, layer),
                  _layer_spec(w3, layer),
                  _layer_spec(w2, layer),
                  pl.BlockSpec((1, d), const2)],
        out_specs=pl.BlockSpec((1, tm, d), row),
        out_shape=jax.ShapeDtypeStruct((bsz, L, d), F32),
        compiler_params=_cparams("parallel", "parallel"),
        name="out_ffn",
    )(o, w_out, b_out, g1, x, g, sh, sc, gate, w1, w3, w2, final_g)


def kernel(x, c, rel_bias, w_ada, b_ada, norm_mix_g, norm_ffn_g, a_w_in, a_kv_norm_g, a_w_uk,
           a_w_uv, a_idx_k_g, a_idx_k_b, a_w_out, b_w_in, b_b_in, b_sinks, b_w_out, b_b_out,
           ffn_w1, ffn_w3, ffn_w2, norm_final_g):
    bsz, L, d = x.shape
    depth = w_ada.shape[0]
    topk = min(INDEX_TOPK, L // 4)

    mod = _adaln(c, w_ada, b_ada)

    r = jnp.arange(BIAS_BLK)
    ids_a = jnp.stack([_t5_bucket(r[:, None] - r[None, :]),
                       _t5_bucket(r[:, None] - r[None, :] + BIAS_BLK)])
    bias_a = _bias_tiles(rel_bias, ids_a, shift_far=True)
    dist_b = jnp.arange(WINDOW)[:, None] + WINDOW - jnp.arange(2 * WINDOW)[None, :]
    in_window = (dist_b >= 0) & (dist_b < WINDOW)
    ids_b = jnp.where(in_window, _t5_bucket(dist_b), -1)
    ids_b0 = jnp.where(jnp.arange(2 * WINDOW)[None, :] >= WINDOW, ids_b, -1)
    bias_b = _bias_tiles(rel_bias, jnp.stack([ids_b, ids_b0]), shift_far=False)
    pairs = N_HEADS // KV_HEADS // 2
    bias_b = bias_b.reshape(KV_HEADS, pairs, 2, 2, WINDOW, 2 * WINDOW).transpose(3, 0, 1, 4, 2, 5)
    bias_b = bias_b.reshape(2, KV_HEADS, pairs * WINDOW, 4 * WINDOW)

    a_w_main = jnp.pad(a_w_in[:, :, :A_MAIN], ((0, 0), (0, 0), (0, A_MAIN_PAD - A_MAIN))).astype(BF16)
    a_w_wi_t = jnp.pad(a_w_in[:, :, A_WI0:A_WI0 + IDX_HEADS].transpose(0, 2, 1),
                       ((0, 0), (0, 16 - IDX_HEADS), (0, 0))).astype(BF16)
    a_w_uk, a_w_uv, a_w_out, b_w_in, b_w_out, ffn_w1, ffn_w3, ffn_w2 = [
        w.astype(BF16) for w in (jnp.swapaxes(_pair_blocks(a_w_uk), -1, -2),
                                 _pair_blocks(a_w_uv), a_w_out, b_w_in,
                                 b_w_out, ffn_w1, ffn_w3, ffn_w2)]

    zero_bias = jnp.zeros((1, d), F32)
    for i in range(depth):
        sh1, sc1, g1, sh2, sc2, g2 = [m.reshape(bsz, 1, d) for m in jnp.split(mod[i], 6, axis=-1)]
        jm = i // 2
        ffn_args = (g1, x, norm_ffn_g[i][None], sh2, sc2, g2, ffn_w1, ffn_w3, ffn_w2, i,
                    norm_final_g[None], i == depth - 1)
        if i % 2 == 0:
            q, ckv, ckvt, qi, ki, wit = _a_inproj(
                x, norm_mix_g[i][None], sh1, sc1, a_w_main, a_w_wi_t, jm, a_kv_norm_g[jm][None],
                a_idx_k_g[jm][None], a_idx_k_b[jm][None])
            mask = _a_index(qi, wit, ki, topk)
            o = _a_attn(q, ckv, ckvt, mask, a_w_uk, a_w_uv, jm, bias_a)
            x = _out_ffn(o, a_w_out, jm, zero_bias, *ffn_args)
        else:
            q, k, v = _b_inproj(x, norm_mix_g[i][None], sh1, sc1, b_w_in, jm, b_b_in[jm][None])
            x = _b_attn_ffn(q, k, v, b_sinks[jm], bias_b, b_w_out, jm, b_b_out[jm][None], *ffn_args)
    return x
```

```python
import functools
import math

import numpy as np
import jax
import jax.numpy as jnp
from jax import lax
from jax.experimental import pallas as pl
from jax.experimental.pallas import tpu as pltpu

D_MODEL = 1024
N_HEADS = 16
HEAD_DIM = 64
KV_RANK = 256
IDX_HEADS = 8
IDX_DIM = 64
INDEX_TOPK = 256
KV_HEADS = 2
WINDOW = 128
N_BUCKETS = 32
MAX_DISTANCE = 128
D_FF = 2816
RMS_EPS = 1e-6
NEG = -1e30

HD = N_HEADS * HEAD_DIM
A_Q0, A_KV0, A_QI0, A_KI0, A_WI0 = 0, HD, HD + KV_RANK, HD + KV_RANK + IDX_HEADS * IDX_DIM, \
    HD + KV_RANK + IDX_HEADS * IDX_DIM + IDX_DIM
A_MAIN = A_WI0
A_MAIN_PAD = 1920
B_IN = (N_HEADS + 2 * KV_HEADS) * HEAD_DIM

CHUNK = 256
PV_ROWS = 64
ACC_ROWS = 256
HALF_ROWS = 16
LANES = 128
BIAS_BLK = 128
VMEM_LIMIT = 56 * 1024 * 1024

F32 = jnp.float32
BF16 = jnp.bfloat16
I32 = jnp.int32
I16 = jnp.int16

_NT = (((1,), (1,)), ((), ()))


def _cparams(*sem):
    return pltpu.CompilerParams(dimension_semantics=sem, vmem_limit_bytes=VMEM_LIMIT)


def _pair_blocks(w):
    even, odd = w[..., 0::2, :, :], w[..., 1::2, :, :]
    zero = jnp.zeros_like(even)
    return jnp.concatenate([jnp.concatenate([even, zero], axis=-1),
                            jnp.concatenate([zero, odd], axis=-1)], axis=-2)


def _layer_spec(stack, layer):
    zeros = (0,) * (stack.ndim - 1)
    return pl.BlockSpec((None,) + stack.shape[1:], lambda *_: (layer,) + zeros)


def _f32_key(v):
    b = int(np.array(v, np.float32).view(np.int32))
    return b ^ ((b >> 31) & 0x7FFFFFFF)


LOG2E = math.log2(math.e)
KEY_NEG = _f32_key(NEG)
INT_MIN = -(2 ** 31)


def _adaln_kernel(c_ref, w_ref, b_ref, o_ref):
    c = c_ref[...]
    cs = c * jax.nn.sigmoid(c)
    o_ref[0] = jnp.dot(cs, w_ref[0], preferred_element_type=F32,
                       precision=lax.Precision.HIGHEST) + b_ref[0]


def _adaln(c, w_ada, b_ada):
    depth, d, n = w_ada.shape
    bsz = c.shape[0]
    tn = 1536
    return pl.pallas_call(
        _adaln_kernel,
        grid=(depth, n // tn),
        in_specs=[pl.BlockSpec((bsz, d), lambda i, j: (0, 0)),
                  pl.BlockSpec((1, d, tn), lambda i, j: (i, 0, j)),
                  pl.BlockSpec((1, 1, tn), lambda i, j: (i, 0, j))],
        out_specs=pl.BlockSpec((1, bsz, tn), lambda i, j: (i, 0, j)),
        out_shape=jax.ShapeDtypeStruct((depth, bsz, n), F32),
        compiler_params=_cparams("parallel", "parallel"),
        name="adaln",
    )(c, w_ada, b_ada.reshape(depth, 1, n))


def _normmod(x, g, sh, sc):
    ms = jnp.mean(x * x, axis=-1, keepdims=True)
    y = (x * lax.rsqrt(ms + RMS_EPS)) * g
    return y * (1.0 + sc) + sh


def _a_inproj_kernel(x_ref, g_ref, sh_ref, sc_ref, w_ref, wwi_ref, kvg_ref, ikg_ref, ikb_ref,
                     q_ref, ckv_ref, ckvt_ref, qi_ref, ki_ref, wit_ref):
    h = _normmod(x_ref[0], g_ref[...], sh_ref[0], sc_ref[0]).astype(BF16)
    proj = jnp.dot(h, w_ref[...], preferred_element_type=F32)
    q_ref[0] = (proj[:, A_Q0:A_KV0] * (HEAD_DIM ** -0.5)).astype(BF16)
    ckv = proj[:, A_KV0:A_QI0]
    ckv = (ckv * lax.rsqrt(jnp.mean(ckv * ckv, axis=-1, keepdims=True) + RMS_EPS)) * kvg_ref[...]
    ckv_ref[0] = ckv.astype(BF16)
    for c in range(ckvt_ref.shape[1]):
        ckvt_ref[0, c] = ckv[c * CHUNK:(c + 1) * CHUNK].T.astype(BF16)
    qi_ref[0] = proj[:, A_QI0:A_KI0].astype(BF16)
    ki = proj[:, A_KI0:A_WI0]
    mu = jnp.mean(ki, axis=-1, keepdims=True)
    var = jnp.mean(jnp.square(ki - mu), axis=-1, keepdims=True)
    ki = ((ki - mu) * lax.rsqrt(var + RMS_EPS)) * ikg_ref[...] + ikb_ref[...]
    ki_ref[0] = ki.astype(BF16)
    wit = lax.dot_general(wwi_ref[...], h, _NT, preferred_element_type=F32)
    wit_ref[0] = wit[:IDX_HEADS] * (IDX_HEADS ** -0.5 * IDX_DIM ** -0.5)


def _a_inproj(x, g, sh, sc, w_main, w_wi_t, layer, kv_g, ik_g, ik_b, tm=512):
    bsz, L, d = x.shape
    row = lambda b, i: (b, i, 0)
    per_b = lambda b, i: (b, 0, 0)
    const2 = lambda b, i: (0, 0)
    return pl.pallas_call(
        _a_inproj_kernel,
        grid=(bsz, L // tm),
        in_specs=[pl.BlockSpec((1, tm, d), row),
                  pl.BlockSpec((1, d), const2),
                  pl.BlockSpec((1, 1, d), per_b),
                  pl.BlockSpec((1, 1, d), per_b),
                  _layer_spec(w_main, layer),
                  _layer_spec(w_wi_t, layer),
                  pl.BlockSpec((1, KV_RANK), const2),
                  pl.BlockSpec((1, IDX_DIM), const2),
                  pl.BlockSpec((1, IDX_DIM), const2)],
        out_specs=[pl.BlockSpec((1, tm, HD), row),
                   pl.BlockSpec((1, tm, KV_RANK), row),
                   pl.BlockSpec((1, tm // CHUNK, KV_RANK, CHUNK), lambda b, i: (b, i, 0, 0)),
                   pl.BlockSpec((1, tm, IDX_HEADS * IDX_DIM), row),
                   pl.BlockSpec((1, tm, IDX_DIM), row),
                   pl.BlockSpec((1, IDX_HEADS, tm), lambda b, i: (b, 0, i))],
        out_shape=[jax.ShapeDtypeStruct((bsz, L, HD), BF16),
                   jax.ShapeDtypeStruct((bsz, L, KV_RANK), BF16),
                   jax.ShapeDtypeStruct((bsz, L // CHUNK, KV_RANK, CHUNK), BF16),
                   jax.ShapeDtypeStruct((bsz, L, IDX_HEADS * IDX_DIM), BF16),
                   jax.ShapeDtypeStruct((bsz, L, IDX_DIM), BF16),
                   jax.ShapeDtypeStruct((bsz, IDX_HEADS, L), F32)],
        compiler_params=_cparams("parallel", "parallel"),
        name="a_inproj",
    )(x, g, sh, sc, w_main, w_wi_t, kv_g, ik_g, ik_b)


def _a_index_kernel(qi_ref, wit_ref, ki_ref, tri_ref, mask_ref, key_scr, half_scr, *, topk,
                    seq_len):
    j = pl.program_id(1)
    nchunk = seq_len // CHUNK
    qi = qi_ref[0]
    wit = wit_ref[0]
    t_glob = j * CHUNK + lax.broadcasted_iota(I32, (CHUNK, CHUNK), 1)
    s_loc = lax.broadcasted_iota(I32, (CHUNK, CHUNK), 0)

    def chunks(fn, init):
        def pair(i, carry):
            return fn(2 * i + 1, fn(2 * i, carry))
        carry = lax.fori_loop(0, lax.shift_right_logical(j + 1, 1), pair, init)
        return lax.cond((j & 1) == 0, lambda c: fn(j, c), lambda c: c, carry)

    def score_chunk(kc, carry):
        kik = ki_ref[0, pl.ds(pl.multiple_of(kc * CHUNK, CHUNK), CHUNK), :]
        acc = jnp.zeros((CHUNK, CHUNK), F32)
        for h in range(IDX_HEADS):
            r = lax.dot_general(kik, qi[:, h * IDX_DIM:(h + 1) * IDX_DIM], _NT,
                                preferred_element_type=F32)
            acc = acc + jnp.maximum(r, 0.0) * wit[h:h + 1, :]
        sc = jnp.where(kc * CHUNK + s_loc <= t_glob, acc, NEG)
        bits = lax.bitcast_convert_type(sc, I32)
        key = bits ^ ((bits >> 31) & 0x7FFFFFFF)
        key_scr[kc] = key
        half_scr[kc] = (key >> 16).astype(I16)
        return carry

    chunks(score_chunk, 0)

    n_beyond = seq_len - (j + 1) * CHUNK
    ge = lambda a, b: a >= b
    gt = lambda a, b: a > b

    def count(pred, thr):
        def body(kc, acc):
            m = jnp.where(pred(key_scr[kc], thr), jnp.int32(1), jnp.int32(0))
            return acc + m.reshape(CHUNK // 8, 8, CHUNK).sum(axis=0)
        acc = chunks(body, jnp.zeros((8, CHUNK), I32))
        cnt = acc.sum(axis=0, keepdims=True)
        return cnt + jnp.where(pred(KEY_NEG, thr), n_beyond, 0)

    def count_half(pred, thr16):
        t = jnp.broadcast_to(thr16, (HALF_ROWS, CHUNK)).astype(I16)
        def body(kc, acc):
            m = jnp.where(pred(half_scr[kc].reshape(CHUNK // HALF_ROWS, HALF_ROWS, CHUNK), t[None]),
                          jnp.int16(1), jnp.int16(0))
            for r in range(CHUNK // HALF_ROWS):
                acc = acc + m[r]
            return acc
        acc = chunks(body, jnp.zeros((HALF_ROWS, CHUNK), I16))
        return acc.astype(I32).sum(axis=0, keepdims=True)

    def beyond(cand):
        return jnp.where(KEY_NEG >= cand, n_beyond, 0)

    lowest = jnp.full((1, CHUNK), INT_MIN, I32)

    def bisect():
        zero = jnp.zeros((1, CHUNK), I32)
        thr = jnp.where(count_half(ge, zero) + beyond(zero) >= topk, zero, lowest)

        def bisect_high(i, thr):
            cand = thr + lax.shift_left(jnp.int32(1), 30 - i)
            cnt = count_half(ge, cand >> 16) + beyond(cand)
            return jnp.where(cnt >= topk, cand, thr)

        thr = lax.fori_loop(0, 15, bisect_high, thr)

        thr_hi = thr >> 16
        n_above = count_half(gt, thr_hi)

        def low_half(kc, carry):
            key = key_scr[kc]
            low = (key & 0xFFFF) - 0x8000
            half_scr[kc] = jnp.where((key >> 16) == thr_hi, low, -0x8000).astype(I16)
            return carry

        chunks(low_half, 0)

        def bisect_low(i, thr):
            cand = thr + lax.shift_left(jnp.int32(1), 15 - i)
            cnt = n_above + count_half(ge, (cand & 0xFFFF) - 0x8000) + beyond(cand)
            return jnp.where(cnt >= topk, cand, thr)

        return lax.fori_loop(0, 16, bisect_low, thr)

    takes_all = (j + 1) * CHUNK <= topk
    thr = lax.cond(takes_all, lambda: lowest, bisect)

    need = (topk - count(gt, thr)).astype(F32)
    tri = tri_ref[...]

    def emit_ranked(kc, seen):
        key = key_scr[kc]
        eq = jnp.where(key == thr, 1.0, 0.0)
        rank = jnp.dot(tri, eq.astype(BF16), preferred_element_type=F32) + seen
        take = jnp.where(key > thr, 1.0, jnp.where(rank < need, eq, 0.0))
        causal = kc * CHUNK + s_loc <= t_glob
        m = jnp.where(causal, jnp.where(take > 0.5, 0.0, NEG), NEG)
        mask_ref[0, kc] = m.T
        return seen + jnp.sum(eq, axis=0, keepdims=True)

    def emit_plain(kc, seen):
        causal = kc * CHUNK + s_loc <= t_glob
        m = jnp.where(causal, jnp.where(key_scr[kc] >= thr, 0.0, NEG), NEG)
        mask_ref[0, kc] = m.T
        return seen

    exact_fit = takes_all | (jnp.max(count(ge, thr)) <= topk)
    lax.cond(exact_fit, lambda: chunks(emit_plain, jnp.zeros((1, CHUNK), F32)),
             lambda: chunks(emit_ranked, jnp.zeros((1, CHUNK), F32)))

    def fill(kc, carry):
        mask_ref[0, kc] = jnp.full((CHUNK, CHUNK), NEG, F32)
        return carry

    lax.fori_loop(j + 1, nchunk, fill, 0)


def _a_index(qi, wit, ki, topk):
    bsz, L, _ = qi.shape
    nchunk = L // CHUNK
    tri = jnp.tril(jnp.ones((CHUNK, CHUNK), BF16), -1)
    return pl.pallas_call(
        functools.partial(_a_index_kernel, topk=topk, seq_len=L),
        grid=(bsz, nchunk),
        in_specs=[pl.BlockSpec((1, CHUNK, IDX_HEADS * IDX_DIM), lambda b, j: (b, j, 0)),
                  pl.BlockSpec((1, IDX_HEADS, CHUNK), lambda b, j: (b, 0, j)),
                  pl.BlockSpec((1, L, IDX_DIM), lambda b, j: (b, 0, 0)),
                  pl.BlockSpec((CHUNK, CHUNK), lambda b, j: (0, 0))],
        out_specs=pl.BlockSpec((1, nchunk, CHUNK, CHUNK), lambda b, j: (b, 0, j, 0)),
        out_shape=jax.ShapeDtypeStruct((bsz, nchunk, L, CHUNK), F32),
        scratch_shapes=[pltpu.VMEM((nchunk, CHUNK, CHUNK), I32),
                        pltpu.VMEM((nchunk, CHUNK, CHUNK), I16)],
        compiler_params=_cparams("parallel", "parallel"),
        name="a_index",
    )(qi, wit, ki, tri)


def _a_attn_kernel(q_ref, ckv_ref, ckvt_ref, mask_ref, wuk_ref, wuv_ref, bias_ref, o_ref,
                   qabs_scr, lg_scr, m_scr, l_scr, acc_scr, p_scr, *, hc, nq, n_work):
    s = pl.program_id(0)
    j = (jnp.minimum(s, n_work - 1) // (N_HEADS // hc)) % nq
    slot = s & 1
    tq = CHUNK

    def placeholder(r, carry):
        rows = pl.ds(pl.multiple_of(r * ACC_ROWS, ACC_ROWS), ACC_ROWS)
        l_scr[1, rows, :] = jnp.ones((ACC_ROWS, LANES), F32)
        acc_scr[1, rows, :] = jnp.zeros((ACC_ROWS, KV_RANK), F32)
        return carry

    lax.fori_loop(0, jnp.where(s == 0, hc * tq // ACC_ROWS, 0), placeholder, 0)

    q = q_ref[0]
    for i in range(hc // 2):
        qa = jnp.dot(q[:, i * 2 * HEAD_DIM:(i + 1) * 2 * HEAD_DIM], wuk_ref[i],
                     preferred_element_type=F32)
        for par in range(2):
            qabs_scr[(2 * i + par) * tq:(2 * i + par + 1) * tq, :] = (
                qa[:, par * KV_RANK:(par + 1) * KV_RANK] * LOG2E).astype(BF16)
    o = (acc_scr[1 - slot] / jnp.sum(l_scr[1 - slot], axis=-1, keepdims=True)).astype(BF16)
    outs = [jnp.dot(jnp.concatenate([o[2 * i * tq:(2 * i + 1) * tq],
                                     o[(2 * i + 1) * tq:(2 * i + 2) * tq]], axis=1),
                    wuv_ref[i], preferred_element_type=F32) for i in range(hc // 2)]
    o_ref[0] = jnp.concatenate(outs, axis=1).astype(BF16)
    qg = qabs_scr[...]

    n_far_pairs = lax.shift_right_arithmetic(j - 1, 1)
    odd = (j & 1) == 0

    def keys(kc, width):
        return ckv_ref[0, pl.ds(kc, width)].reshape(width * CHUNK, KV_RANK)

    def logits(kc, width):
        keys_t = jnp.concatenate([ckvt_ref[0, kc + w] for w in range(width)], axis=1)
        lg = jnp.dot(qg, keys_t, preferred_element_type=F32)
        mk = jnp.concatenate([mask_ref[0, kc + w] for w in range(width)], axis=1)
        return lg + jnp.concatenate([mk] * hc, axis=0)

    def near_bias(i, with_prev):
        d0, d1 = bias_ref[i, 0], bias_ref[i, 1]
        z = jnp.zeros_like(d0)
        top, bot = [d0, z], [d1, d0]
        if with_prev:
            top, bot = [z, d1] + top, [z, z] + bot
        return jnp.concatenate([jnp.concatenate(top, axis=1), jnp.concatenate(bot, axis=1)], axis=0)

    def put_logits(kc, width, lg):
        for w in range(width):
            lg_scr[kc + w] = lg[:, w * CHUNK:(w + 1) * CHUNK]
        m = m_scr[...]
        for c in range(width * CHUNK // LANES):
            m = jnp.maximum(m, lg[:, c * LANES:(c + 1) * LANES])
        m_scr[...] = m

    m_scr[...] = jnp.full(m_scr.shape, -jnp.inf, F32)

    @pl.when(j >= 1)
    def _():
        bias = jnp.concatenate([near_bias(i, True) for i in range(hc)], axis=0)
        put_logits(j - 1, 2, logits(j - 1, 2) + bias)

    def far_pair(i, carry):
        kc = j - 3 - 2 * i
        put_logits(kc, 2, logits(kc, 2))
        return carry

    lax.fori_loop(0, n_far_pairs, far_pair, 0)

    @pl.when(odd)
    def _():
        bias = jnp.concatenate([near_bias(i, False) for i in range(hc)], axis=0)
        put_logits(0, 1, logits(0, 1) + jnp.where(j == 0, bias, 0.0))

    l_scr[slot] = jnp.zeros(l_scr.shape[1:], F32)
    acc_scr[slot] = jnp.zeros(acc_scr.shape[1:], F32)

    def pv(kc, width, reduce_max):
        ck = keys(kc, width)
        for piece in range(hc * tq // ACC_ROWS):
            for r in range(piece * ACC_ROWS // PV_ROWS, (piece + 1) * ACC_ROWS // PV_ROWS):
                rows = slice(r * PV_ROWS, (r + 1) * PV_ROWS)
                m = m_scr[rows, :]
                if reduce_max:
                    m = jnp.broadcast_to(jnp.max(m, axis=-1, keepdims=True), m.shape)
                    m_scr[rows, :] = m
                l = l_scr[slot, rows, :]
                for w in range(width):
                    for c in range(CHUNK // LANES):
                        col = w * CHUNK + c * LANES
                        p = jnp.exp2(lg_scr[kc + w, rows, c * LANES:(c + 1) * LANES] - m)
                        l = l + p
                        p_scr[rows, col:col + LANES] = p.astype(BF16)
                l_scr[slot, rows, :] = l
            rows = slice(piece * ACC_ROWS, (piece + 1) * ACC_ROWS)
            acc_scr[slot, rows, :] += jnp.dot(p_scr[rows, :width * CHUNK], ck,
                                              preferred_element_type=F32)

    @pl.when(j >= 1)
    def _():
        pv(j - 1, 2, True)

    def far_pv(i, carry):
        pv(j - 3 - 2 * i, 2, False)
        return carry

    lax.fori_loop(0, n_far_pairs, far_pv, 0)

    @pl.when(odd)
    def _():
        pv(0, 1, True)


def _a_attn(q, ckv, ckvt, mask, w_uk, w_uv, layer, bias_nd, hc=8):
    bsz, L, _ = q.shape
    nchunk = L // CHUNK
    hg = N_HEADS // hc
    m = hc * CHUNK
    ckv4 = ckv.reshape(bsz, nchunk, CHUNK, KV_RANK)
    n_work = bsz * nchunk * hg

    def item(s):
        return s // (nchunk * hg), (s // hg) % nchunk, s % hg

    cur = lambda s: item(jnp.minimum(s, n_work - 1))
    prev = lambda s: item(jnp.maximum(s - 1, 0))
    return pl.pallas_call(
        functools.partial(_a_attn_kernel, hc=hc, nq=nchunk, n_work=n_work),
        grid=(n_work + 1,),
        in_specs=[pl.BlockSpec((1, CHUNK, hc * HEAD_DIM), lambda s: cur(s)),
                  pl.BlockSpec((1, nchunk, CHUNK, KV_RANK), lambda s: (cur(s)[0], 0, 0, 0)),
                  pl.BlockSpec((1, nchunk, KV_RANK, CHUNK), lambda s: (cur(s)[0], 0, 0, 0)),
                  pl.BlockSpec((1, nchunk, CHUNK, CHUNK), lambda s: (cur(s)[0], 0, cur(s)[1], 0)),
                  pl.BlockSpec((None, hc // 2) + w_uk.shape[2:], lambda s: (layer, cur(s)[2], 0, 0)),
                  pl.BlockSpec((None, hc // 2) + w_uv.shape[2:], lambda s: (layer, prev(s)[2], 0, 0)),
                  pl.BlockSpec((hc, 2, BIAS_BLK, BIAS_BLK), lambda s: (cur(s)[2], 0, 0, 0))],
        out_specs=pl.BlockSpec((1, CHUNK, hc * HEAD_DIM), lambda s: prev(s)),
        out_shape=jax.ShapeDtypeStruct((bsz, L, HD), BF16),
        scratch_shapes=[pltpu.VMEM((m, KV_RANK), BF16),
                        pltpu.VMEM((nchunk, m, CHUNK), F32),
                        pltpu.VMEM((m, LANES), F32),
                        pltpu.VMEM((2, m, LANES), F32),
                        pltpu.VMEM((2, m, KV_RANK), F32),
                        pltpu.VMEM((m, 2 * CHUNK), BF16)],
        compiler_params=_cparams("arbitrary"),
        name="a_attn",
    )(q, ckv4, ckvt, mask, w_uk, w_uv, bias_nd)


def _bias_kernel(rb_ref, ids_ref, o_ref, *, n_tiles, far_bucket, shift_far):
    h = pl.program_id(0)
    far = rb_ref[far_bucket, h] if shift_far else 0.0
    for t in range(n_tiles):
        ids = ids_ref[t]
        out = jnp.zeros(ids.shape, F32)
        for b in range(N_BUCKETS):
            out = jnp.where(ids == b, rb_ref[b, h] - far, out)
        o_ref[0, t] = jnp.where(ids < 0, NEG, out * LOG2E)


def _t5_bucket(dist):
    max_exact = N_BUCKETS // 2
    d = jnp.maximum(dist, 0)
    large = max_exact + (jnp.log(jnp.maximum(d, 1).astype(F32) / max_exact)
                         / math.log(MAX_DISTANCE / max_exact)
                         * (N_BUCKETS - max_exact)).astype(I32)
    large = jnp.minimum(large, N_BUCKETS - 1)
    return jnp.where(d < max_exact, d, large)


def _bias_tiles(rel_bias, ids, shift_far):
    n_tiles, r, c = ids.shape
    return pl.pallas_call(
        functools.partial(_bias_kernel, n_tiles=n_tiles, far_bucket=N_BUCKETS - 1,
                          shift_far=shift_far),
        grid=(N_HEADS,),
        in_specs=[pl.BlockSpec(memory_space=pltpu.SMEM),
                  pl.BlockSpec((n_tiles, r, c), lambda h: (0, 0, 0))],
        out_specs=pl.BlockSpec((1, n_tiles, r, c), lambda h: (h, 0, 0, 0)),
        out_shape=jax.ShapeDtypeStruct((N_HEADS, n_tiles, r, c), F32),
        compiler_params=_cparams("parallel"),
        name="bias_tiles",
    )(rel_bias, ids)


def _b_inproj_kernel(x_ref, g_ref, sh_ref, sc_ref, w_ref, b_ref, q_ref, k_ref, v_ref):
    h = _normmod(x_ref[0], g_ref[...], sh_ref[0], sc_ref[0]).astype(BF16)
    proj = jnp.dot(h, w_ref[...], preferred_element_type=F32) + b_ref[...]
    q_ref[0] = (proj[:, :HD] * (HEAD_DIM ** -0.5 * LOG2E)).astype(BF16)
    k_ref[0] = proj[:, HD:HD + KV_HEADS * HEAD_DIM].astype(BF16)
    v_ref[0] = proj[:, HD + KV_HEADS * HEAD_DIM:].astype(BF16)


def _b_inproj(x, g, sh, sc, w, layer, b, tm=512):
    bsz, L, d = x.shape
    kvw = KV_HEADS * HEAD_DIM
    row = lambda b_, i: (b_, i, 0)
    per_b = lambda b_, i: (b_, 0, 0)
    const2 = lambda b_, i: (0, 0)
    return pl.pallas_call(
        _b_inproj_kernel,
        grid=(bsz, L // tm),
        in_specs=[pl.BlockSpec((1, tm, d), row),
                  pl.BlockSpec((1, d), const2),
                  pl.BlockSpec((1, 1, d), per_b),
                  pl.BlockSpec((1, 1, d), per_b),
                  _layer_spec(w, layer),
                  pl.BlockSpec((1, B_IN), const2)],
        out_specs=[pl.BlockSpec((1, tm, HD), row),
                   pl.BlockSpec((1, tm, kvw), row),
                   pl.BlockSpec((1, tm, kvw), row)],
        out_shape=[jax.ShapeDtypeStruct((bsz, L, HD), BF16),
                   jax.ShapeDtypeStruct((bsz, L, kvw), BF16),
                   jax.ShapeDtypeStruct((bsz, L, kvw), BF16)],
        compiler_params=_cparams("parallel", "parallel"),
        name="b_inproj",
    )(x, g, sh, sc, w, b)


def _b_window_attention(sink_ref, q_block, kp, kc, vp, vc, bias_ref, first):
    w = WINDOW
    pairs = N_HEADS // KV_HEADS // 2
    kall = jnp.concatenate([kp, kc], axis=0).astype(F32)
    vall = jnp.concatenate([vp, vc], axis=0).astype(F32)
    low = lax.broadcasted_iota(I32, kall.shape, 1) < HEAD_DIM

    def padded(x, g):
        swapped = pltpu.roll(x, HEAD_DIM, axis=1)
        on_low, on_high = (x, swapped) if g == 0 else (swapped, x)
        return (jnp.where(low, on_low, 0.0).astype(BF16), jnp.where(low, 0.0, on_high).astype(BF16))

    kpad = [padded(kall, g) for g in range(KV_HEADS)]
    vpad = [padded(vall, g) for g in range(KV_HEADS)]

    def attend(blk, g):
        variant = jnp.where(first, 1, 0) if blk == 0 else 0
        keys = slice(blk * w, (blk + 2) * w)
        q = q_block(blk)
        blocks = [q[:, (pairs * g + p) * 2 * HEAD_DIM:(pairs * g + p + 1) * 2 * HEAD_DIM]
                  for p in range(pairs)]
        lg = lax.dot_general(jnp.concatenate(blocks, axis=0),
                             jnp.concatenate([kpad[g][0][keys], kpad[g][1][keys]], axis=0),
                             _NT, preferred_element_type=F32)
        lg = lg + bias_ref[variant, g]
        probs = [[], []]
        for p in range(pairs):
            for par in range(2):
                h = 2 * pairs * g + 2 * p + par
                t = lg[p * w:(p + 1) * w, par * 2 * w:(par + 1) * 2 * w]
                sink = sink_ref[0, h] * LOG2E
                m = jnp.maximum(jnp.max(t, axis=-1, keepdims=True), sink)
                e = jnp.exp2(t - m)
                denom = jnp.sum(e, axis=-1, keepdims=True) + jnp.exp2(sink - m)
                probs[par].append((e * (1.0 / denom)).astype(BF16))
        og = (jnp.dot(jnp.concatenate(probs[0], axis=0), vpad[g][0][keys],
                      preferred_element_type=F32)
              + jnp.dot(jnp.concatenate(probs[1], axis=0), vpad[g][1][keys],
                        preferred_element_type=F32))
        return jnp.concatenate([og[p * w:(p + 1) * w] for p in range(pairs)], axis=1).astype(BF16)

    return attend


FF_CHUNK = 256


def _out_ffn_tile(o, wo_ref, bo_ref, g1_ref, x_ref, g_ref, sh_ref, sc_ref, gate_ref,
                  w1_ref, w3_ref, w2_ref, fg_ref, y_ref, final_norm):
    mix = jnp.dot(o, wo_ref[...], preferred_element_type=F32) + bo_ref[...]
    x = x_ref[0] + g1_ref[0] * mix
    h = _normmod(x, g_ref[...], sh_ref[0], sc_ref[0]).astype(BF16)
    acc = jnp.zeros(x.shape, F32)
    for c in range(D_FF // FF_CHUNK):
        cs = slice(c * FF_CHUNK, (c + 1) * FF_CHUNK)
        a1 = jnp.dot(h, w1_ref[:, cs], preferred_element_type=F32)
        a3 = jnp.dot(h, w3_ref[:, cs], preferred_element_type=F32)
        act = (a1 * jax.nn.sigmoid(a1)) * a3
        acc = acc + jnp.dot(act.astype(BF16), w2_ref[cs, :], preferred_element_type=F32)
    y = x + gate_ref[0] * acc
    if final_norm:
        y = (y * lax.rsqrt(jnp.mean(y * y, axis=-1, keepdims=True) + RMS_EPS)) * fg_ref[...]
    y_ref[0] = y


def _out_ffn_kernel(o_ref, *refs, final_norm):
    _out_ffn_tile(o_ref[0], *refs, final_norm)


def _b_attn_ffn_kernel(sink_ref, q_ref, kp_ref, kc_ref, vp_ref, vc_ref, bias_ref, *refs,
                       final_norm, n_blocks, tiles_per_seq, n_work):
    ffn_refs, o_scr = refs[:-1], refs[-1]
    s = pl.program_id(0)
    slot = s & 1
    w = WINDOW

    def placeholder(r, carry):
        o_scr[1, pl.ds(pl.multiple_of(r * w, w), w), :] = jnp.zeros((w, HD), BF16)
        return carry

    lax.fori_loop(0, jnp.where(s == 0, n_blocks, 0), placeholder, 0)

    first = lax.rem(jnp.minimum(s, n_work - 1), tiles_per_seq) == 0
    attend = _b_window_attention(sink_ref, lambda blk: q_ref[0, blk * w:(blk + 1) * w, :],
                                 kp_ref[0], kc_ref[0], vp_ref[0], vc_ref[0], bias_ref, first)

    _out_ffn_tile(o_scr[1 - slot], *ffn_refs, final_norm)
    gw = HD // KV_HEADS
    for blk in range(n_blocks):
        for g in range(KV_HEADS):
            o_scr[slot, blk * w:(blk + 1) * w, g * gw:(g + 1) * gw] = attend(blk, g)


def _b_attn_ffn(q, k, v, sinks, bias_b, w_out, mix_layer, b_out, g1, x, g, sh, sc, gate, w1, w3,
                w2, layer, final_g, final_norm, tm=512):
    bsz, L, d = x.shape
    w = WINDOW
    kvw = KV_HEADS * HEAD_DIM
    tiles = L // tm
    n_work = bsz * tiles

    def tile(s):
        return s // tiles, s % tiles

    cur = lambda s: tile(jnp.minimum(s, n_work - 1))
    prev = lambda s: tile(jnp.maximum(s - 1, 0))
    cur_row = lambda s: cur(s) + (0,)
    before = lambda s: (cur(s)[0], jnp.maximum(cur(s)[1] * (tm // w) - 1, 0), 0)
    prev_row = lambda s: prev(s) + (0,)
    prev_b = lambda s: (prev(s)[0], 0, 0)
    const2 = lambda s: (0, 0)
    return pl.pallas_call(
        functools.partial(_b_attn_ffn_kernel, final_norm=final_norm, n_blocks=tm // w,
                          tiles_per_seq=tiles, n_work=n_work),
        grid=(n_work + 1,),
        in_specs=[pl.BlockSpec(memory_space=pltpu.SMEM),
                  pl.BlockSpec((1, tm, HD), cur_row),
                  pl.BlockSpec((1, w, kvw), before),
                  pl.BlockSpec((1, tm, kvw), cur_row),
                  pl.BlockSpec((1, w, kvw), before),
                  pl.BlockSpec((1, tm, kvw), cur_row),
                  pl.BlockSpec(bias_b.shape, lambda s: (0, 0, 0, 0)),
                  _layer_spec(w_out, mix_layer),
                  pl.BlockSpec((1, d), const2),
                  pl.BlockSpec((1, 1, d), prev_b),
                  pl.BlockSpec((1, tm, d), prev_row),
                  pl.BlockSpec((1, d), const2),
                  pl.BlockSpec((1, 1, d), prev_b),
                  pl.BlockSpec((1, 1, d), prev_b),
                  pl.BlockSpec((1, 1, d), prev_b),
                  _layer_spec(w1, layer),
                  _layer_spec(w3, layer),
                  _layer_spec(w2, layer),
                  pl.BlockSpec((1, d), const2)],
        out_specs=pl.BlockSpec((1, tm, d), prev_row),
        out_shape=jax.ShapeDtypeStruct((bsz, L, d), F32),
        scratch_shapes=[pltpu.VMEM((2, tm, HD), BF16)],
        compiler_params=_cparams("arbitrary"),
        name="b_attn_ffn",
    )(sinks.reshape(1, N_HEADS), q, k, k, v, v, bias_b, w_out, b_out, g1, x, g, sh, sc, gate,
      w1, w3, w2, final_g)


def _out_ffn(o, w_out, mix_layer, b_out, g1, x, g, sh, sc, gate, w1, w3, w2, layer, final_g,
             final_norm, tm=512):
    bsz, L, d = x.shape
    row = lambda b_, i: (b_, i, 0)
    per_b = lambda b_, i: (b_, 0, 0)
    const2 = lambda b_, i: (0, 0)
    return pl.pallas_call(
        functools.partial(_out_ffn_kernel, final_norm=final_norm),
        grid=(bsz, L // tm),
        in_specs=[pl.BlockSpec((1, tm, HD), row),
                  _layer_spec(w_out, mix_layer),
                  pl.BlockSpec((1, d), const2),
                  pl.BlockSpec((1, 1, d), per_b),
                  pl.BlockSpec((1, tm, d), row),
                  pl.BlockSpec((1, d), const2),
                  pl.BlockSpec((1, 1, d), per_b),
                  pl.BlockSpec((1, 1, d), per_b),
                  pl.BlockSpec((1, 1, d), per_b),
                  _layer_spec(w1, layer),
                  _layer_spec(w3, layer),
                  _layer_spec(w2, layer),
                  pl.BlockSpec((1, d), const2)],
        out_specs=pl.BlockSpec((1, tm, d), row),
        out_shape=jax.ShapeDtypeStruct((bsz, L, d), F32),
        compiler_params=_cparams("parallel", "parallel"),
        name="out_ffn",
    )(o, w_out, b_out, g1, x, g, sh, sc, gate, w1, w3, w2, final_g)


def kernel(x, c, rel_bias, w_ada, b_ada, norm_mix_g, norm_ffn_g, a_w_in, a_kv_norm_g, a_w_uk,
           a_w_uv, a_idx_k_g, a_idx_k_b, a_w_out, b_w_in, b_b_in, b_sinks, b_w_out, b_b_out,
           ffn_w1, ffn_w3, ffn_w2, norm_final_g):
    bsz, L, d = x.shape
    depth = w_ada.shape[0]
    topk = min(INDEX_TOPK, L // 4)

    mod = _adaln(c, w_ada, b_ada)

    r = jnp.arange(BIAS_BLK)
    ids_a = jnp.stack([_t5_bucket(r[:, None] - r[None, :]),
                       _t5_bucket(r[:, None] - r[None, :] + BIAS_BLK)])
    bias_a = _bias_tiles(rel_bias, ids_a, shift_far=True)
    dist_b = jnp.arange(WINDOW)[:, None] + WINDOW - jnp.arange(2 * WINDOW)[None, :]
    in_window = (dist_b >= 0) & (dist_b < WINDOW)
    ids_b = jnp.where(in_window, _t5_bucket(dist_b), -1)
    ids_b0 = jnp.where(jnp.arange(2 * WINDOW)[None, :] >= WINDOW, ids_b, -1)
    bias_b = _bias_tiles(rel_bias, jnp.stack([ids_b, ids_b0]), shift_far=False)
    pairs = N_HEADS // KV_HEADS // 2
    bias_b = bias_b.reshape(KV_HEADS, pairs, 2, 2, WINDOW, 2 * WINDOW).transpose(3, 0, 1, 4, 2, 5)
    bias_b = bias_b.reshape(2, KV_HEADS, pairs * WINDOW, 4 * WINDOW)

    a_w_main = jnp.pad(a_w_in[:, :, :A_MAIN], ((0, 0), (0, 0), (0, A_MAIN_PAD - A_MAIN))).astype(BF16)
    a_w_wi_t = jnp.pad(a_w_in[:, :, A_WI0:A_WI0 + IDX_HEADS].transpose(0, 2, 1),
                       ((0, 0), (0, 16 - IDX_HEADS), (0, 0))).astype(BF16)
    a_w_uk, a_w_uv, a_w_out, b_w_in, b_w_out, ffn_w1, ffn_w3, ffn_w2 = [
        w.astype(BF16) for w in (jnp.swapaxes(_pair_blocks(a_w_uk), -1, -2),
                                 _pair_blocks(a_w_uv), a_w_out, b_w_in,
                                 b_w_out, ffn_w1, ffn_w3, ffn_w2)]

    zero_bias = jnp.zeros((1, d), F32)
    for i in range(depth):
        sh1, sc1, g1, sh2, sc2, g2 = [m.reshape(bsz, 1, d) for m in jnp.split(mod[i], 6, axis=-1)]
        jm = i // 2
        ffn_args = (g1, x, norm_ffn_g[i][None], sh2, sc2, g2, ffn_w1, ffn_w3, ffn_w2, i,
                    norm_final_g[None], i == depth - 1)
        if i % 2 == 0:
            q, ckv, ckvt, qi, ki, wit = _a_inproj(
                x, norm_mix_g[i][None], sh1, sc1, a_w_main, a_w_wi_t, jm, a_kv_norm_g[jm][None],
                a_idx_k_g[jm][None], a_idx_k_b[jm][None])
            mask = _a_index(qi, wit, ki, topk)
            o = _a_attn(q, ckv, ckvt, mask, a_w_uk, a_w_uv, jm, bias_a)
            x = _out_ffn(o, a_w_out, jm, zero_bias, *ffn_args)
        else:
            q, k, v = _b_inproj(x, norm_mix_g[i][None], sh1, sc1, b_w_in, jm, b_b_in[jm][None])
            x = _b_attn_ffn(q, k, v, b_sinks[jm], bias_b, b_w_out, jm, b_b_out[jm][None], *ffn_args)
    return x
```

```python
import functools
import math

import numpy as np
import jax
import jax.numpy as jnp
from jax import lax
from jax.experimental import pallas as pl
from jax.experimental.pallas import tpu as pltpu

N_HEADS = 16
HEAD_DIM = 64
KV_RANK = 256
IDX_HEADS = 8
IDX_DIM = 64
INDEX_TOPK = 256
KV_HEADS = 2
WINDOW = 128
N_BUCKETS = 32
MAX_DISTANCE = 128
D_FF = 2816
RMS_EPS = 1e-6
NEG = -1e30

HD = N_HEADS * HEAD_DIM
A_Q0, A_KV0, A_QI0, A_KI0, A_WI0 = 0, HD, HD + KV_RANK, HD + KV_RANK + IDX_HEADS * IDX_DIM, \
    HD + KV_RANK + IDX_HEADS * IDX_DIM + IDX_DIM
LANES = 128
HALF_ROWS = 16
A_MAIN = A_WI0
A_MAIN_PAD = -(-A_MAIN // LANES) * LANES
B_IN = (N_HEADS + 2 * KV_HEADS) * HEAD_DIM

CHUNK = 256
PV_ROWS = 64
ACC_ROWS = 256
BIAS_BLK = 128
VMEM_LIMIT = 56 * 1024 * 1024

F32 = jnp.float32
BF16 = jnp.bfloat16
I32 = jnp.int32
I16 = jnp.int16

_NT = (((1,), (1,)), ((), ()))


def _cparams(*sem):
    return pltpu.CompilerParams(dimension_semantics=sem, vmem_limit_bytes=VMEM_LIMIT)


def _pair_blocks(w):
    even, odd = w[..., 0::2, :, :], w[..., 1::2, :, :]
    zero = jnp.zeros_like(even)
    return jnp.concatenate([jnp.concatenate([even, zero], axis=-1),
                            jnp.concatenate([zero, odd], axis=-1)], axis=-2)


def _layer_spec(stack, layer):
    zeros = (0,) * (stack.ndim - 1)
    return pl.BlockSpec((None,) + stack.shape[1:], lambda *_: (layer,) + zeros)


def _f32_key(v):
    b = int(np.array(v, np.float32).view(np.int32))
    return b ^ ((b >> 31) & 0x7FFFFFFF)


LOG2E = math.log2(math.e)
KEY_NEG = _f32_key(NEG)
INT_MIN = -(2 ** 31)


def _adaln_kernel(c_ref, w_ref, b_ref, o_ref):
    c = c_ref[...]
    cs = c * jax.nn.sigmoid(c)
    o_ref[0] = jnp.dot(cs, w_ref[0], preferred_element_type=F32,
                       precision=lax.Precision.HIGHEST) + b_ref[0]


def _adaln(c, w_ada, b_ada):
    depth, d, n = w_ada.shape
    bsz = c.shape[0]
    tn = 1536
    return pl.pallas_call(
        _adaln_kernel,
        grid=(depth, n // tn),
        in_specs=[pl.BlockSpec((bsz, d), lambda i, j: (0, 0)),
                  pl.BlockSpec((1, d, tn), lambda i, j: (i, 0, j)),
                  pl.BlockSpec((1, 1, tn), lambda i, j: (i, 0, j))],
        out_specs=pl.BlockSpec((1, bsz, tn), lambda i, j: (i, 0, j)),
        out_shape=jax.ShapeDtypeStruct((depth, bsz, n), F32),
        compiler_params=_cparams("parallel", "parallel"),
        name="adaln",
    )(c, w_ada, b_ada.reshape(depth, 1, n))


def _normmod(x, g, sh, sc):
    ms = jnp.mean(x * x, axis=-1, keepdims=True)
    y = (x * lax.rsqrt(ms + RMS_EPS)) * g
    return y * (1.0 + sc) + sh


def _a_inproj_kernel(x_ref, g_ref, sh_ref, sc_ref, w_ref, wwi_ref, kvg_ref, ikg_ref, ikb_ref,
                     q_ref, ckv_ref, ckvt_ref, qi_ref, ki_ref, wit_ref):
    h = _normmod(x_ref[0], g_ref[...], sh_ref[0], sc_ref[0]).astype(BF16)
    proj = jnp.dot(h, w_ref[...], preferred_element_type=F32)
    q_ref[0] = (proj[:, A_Q0:A_KV0] * (HEAD_DIM ** -0.5)).astype(BF16)
    ckv = proj[:, A_KV0:A_QI0]
    ckv = (ckv * lax.rsqrt(jnp.mean(ckv * ckv, axis=-1, keepdims=True) + RMS_EPS)) * kvg_ref[...]
    ckv_ref[0] = ckv.astype(BF16)
    for c in range(ckvt_ref.shape[1]):
        ckvt_ref[0, c] = ckv[c * CHUNK:(c + 1) * CHUNK].T.astype(BF16)
    qi_ref[0] = proj[:, A_QI0:A_KI0].astype(BF16)
    ki = proj[:, A_KI0:A_WI0]
    mu = jnp.mean(ki, axis=-1, keepdims=True)
    var = jnp.mean(jnp.square(ki - mu), axis=-1, keepdims=True)
    ki = ((ki - mu) * lax.rsqrt(var + RMS_EPS)) * ikg_ref[...] + ikb_ref[...]
    ki_ref[0] = ki.astype(BF16)
    wit = lax.dot_general(wwi_ref[...], h, _NT, preferred_element_type=F32)
    wit_ref[0] = wit[:IDX_HEADS] * (IDX_HEADS ** -0.5 * IDX_DIM ** -0.5)


def _a_inproj(x, g, sh, sc, w_main, w_wi_t, layer, kv_g, ik_g, ik_b, tm=512):
    bsz, L, d = x.shape
    row = lambda b, i: (b, i, 0)
    per_b = lambda b, i: (b, 0, 0)
    const2 = lambda b, i: (0, 0)
    return pl.pallas_call(
        _a_inproj_kernel,
        grid=(bsz, L // tm),
        in_specs=[pl.BlockSpec((1, tm, d), row),
                  pl.BlockSpec((1, d), const2),
                  pl.BlockSpec((1, 1, d), per_b),
                  pl.BlockSpec((1, 1, d), per_b),
                  _layer_spec(w_main, layer),
                  _layer_spec(w_wi_t, layer),
                  pl.BlockSpec((1, KV_RANK), const2),
                  pl.BlockSpec((1, IDX_DIM), const2),
                  pl.BlockSpec((1, IDX_DIM), const2)],
        out_specs=[pl.BlockSpec((1, tm, HD), row),
                   pl.BlockSpec((1, tm, KV_RANK), row),
                   pl.BlockSpec((1, tm // CHUNK, KV_RANK, CHUNK), lambda b, i: (b, i, 0, 0)),
                   pl.BlockSpec((1, tm, IDX_HEADS * IDX_DIM), row),
                   pl.BlockSpec((1, tm, IDX_DIM), row),
                   pl.BlockSpec((1, IDX_HEADS, tm), lambda b, i: (b, 0, i))],
        out_shape=[jax.ShapeDtypeStruct((bsz, L, HD), BF16),
                   jax.ShapeDtypeStruct((bsz, L, KV_RANK), BF16),
                   jax.ShapeDtypeStruct((bsz, L // CHUNK, KV_RANK, CHUNK), BF16),
                   jax.ShapeDtypeStruct((bsz, L, IDX_HEADS * IDX_DIM), BF16),
                   jax.ShapeDtypeStruct((bsz, L, IDX_DIM), BF16),
                   jax.ShapeDtypeStruct((bsz, IDX_HEADS, L), F32)],
        compiler_params=_cparams("parallel", "parallel"),
        name="a_inproj",
    )(x, g, sh, sc, w_main, w_wi_t, kv_g, ik_g, ik_b)


def _a_index_kernel(qi_ref, wit_ref, ki_ref, tri_ref, mask_ref, key_scr, half_scr, *, topk,
                    seq_len):
    j = pl.program_id(1)
    nchunk = seq_len // CHUNK
    qi = qi_ref[0]
    wit = wit_ref[0]
    t_glob = j * CHUNK + lax.broadcasted_iota(I32, (CHUNK, CHUNK), 1)
    s_loc = lax.broadcasted_iota(I32, (CHUNK, CHUNK), 0)

    def chunks(fn, init):
        def pair(i, carry):
            return fn(2 * i + 1, fn(2 * i, carry))
        carry = lax.fori_loop(0, lax.shift_right_logical(j + 1, 1), pair, init)
        return lax.cond((j & 1) == 0, lambda c: fn(j, c), lambda c: c, carry)

    def score_chunk(kc, carry):
        kik = ki_ref[0, pl.ds(pl.multiple_of(kc * CHUNK, CHUNK), CHUNK), :]
        acc = jnp.zeros((CHUNK, CHUNK), F32)
        for h in range(IDX_HEADS):
            r = lax.dot_general(kik, qi[:, h * IDX_DIM:(h + 1) * IDX_DIM], _NT,
                                preferred_element_type=F32)
            acc = acc + jnp.maximum(r, 0.0) * wit[h:h + 1, :]
        sc = jnp.where(kc * CHUNK + s_loc <= t_glob, acc, NEG)
        bits = lax.bitcast_convert_type(sc, I32)
        key = bits ^ ((bits >> 31) & 0x7FFFFFFF)
        key_scr[kc] = key
        half_scr[kc] = (key >> 16).astype(I16)
        return carry

    chunks(score_chunk, 0)

    n_beyond = seq_len - (j + 1) * CHUNK
    ge = lambda a, b: a >= b
    gt = lambda a, b: a > b

    def count(pred, thr):
        def body(kc, acc):
            m = jnp.where(pred(key_scr[kc], thr), jnp.int32(1), jnp.int32(0))
            return acc + m.reshape(CHUNK // 8, 8, CHUNK).sum(axis=0)
        acc = chunks(body, jnp.zeros((8, CHUNK), I32))
        cnt = acc.sum(axis=0, keepdims=True)
        return cnt + jnp.where(pred(KEY_NEG, thr), n_beyond, 0)

    def count_half(pred, thr16):
        t = jnp.broadcast_to(thr16, (HALF_ROWS, CHUNK)).astype(I16)
        def body(kc, acc):
            m = jnp.where(pred(half_scr[kc].reshape(CHUNK // HALF_ROWS, HALF_ROWS, CHUNK), t[None]),
                          jnp.int16(1), jnp.int16(0))
            for r in range(CHUNK // HALF_ROWS):
                acc = acc + m[r]
            return acc
        acc = chunks(body, jnp.zeros((HALF_ROWS, CHUNK), I16))
        return acc.astype(I32).sum(axis=0, keepdims=True)

    def beyond(cand):
        return jnp.where(KEY_NEG >= cand, n_beyond, 0)

    lowest = jnp.full((1, CHUNK), INT_MIN, I32), jnp.full((1, CHUNK), seq_len, I32)

    def step(state, cand, cnt):
        keep = cnt >= topk
        return jnp.where(keep, cand, state[0]), jnp.where(keep, cnt, state[1])

    def bisect():
        zero = jnp.zeros((1, CHUNK), I32)
        state = step(lowest, zero, count_half(ge, zero) + beyond(zero))

        def bisect_high(i, state):
            cand = state[0] + lax.shift_left(jnp.int32(1), 30 - i)
            return step(state, cand, count_half(ge, cand >> 16) + beyond(cand))

        state = lax.fori_loop(0, 15, bisect_high, state)
        thr = state[0]

        thr_hi = thr >> 16
        n_above = count_half(gt, thr_hi)

        def low_half(kc, carry):
            key = key_scr[kc]
            low = (key & 0xFFFF) - 0x8000
            half_scr[kc] = jnp.where((key >> 16) == thr_hi, low, -0x8000).astype(I16)
            return carry

        chunks(low_half, 0)

        def bisect_low(i, state):
            cand = state[0] + lax.shift_left(jnp.int32(1), 15 - i)
            cnt = n_above + count_half(ge, (cand & 0xFFFF) - 0x8000) + beyond(cand)
            return step(state, cand, cnt)

        return lax.fori_loop(0, 16, bisect_low, state)

    takes_all = (j + 1) * CHUNK <= topk
    thr, n_at_thr = lax.cond(takes_all, lambda: lowest, bisect)

    def emit_ranked():
        need = (topk - count(gt, thr)).astype(F32)
        tri = tri_ref[...]

        def emit(kc, seen):
            key = key_scr[kc]
            eq = jnp.where(key == thr, 1.0, 0.0)
            rank = jnp.dot(tri, eq.astype(BF16), preferred_element_type=F32) + seen
            take = jnp.where(key > thr, 1.0, jnp.where(rank < need, eq, 0.0))
            causal = kc * CHUNK + s_loc <= t_glob
            m = jnp.where(causal, jnp.where(take > 0.5, 0.0, NEG), NEG)
            mask_ref[0, kc] = m.T
            return seen + jnp.sum(eq, axis=0, keepdims=True)

        chunks(emit, jnp.zeros((1, CHUNK), F32))

    def emit_plain():
        def emit(kc, carry):
            causal = kc * CHUNK + s_loc <= t_glob
            m = jnp.where(causal, jnp.where(key_scr[kc] >= thr, 0.0, NEG), NEG)
            mask_ref[0, kc] = m.T
            return carry

        chunks(emit, 0)

    exact_fit = takes_all | (jnp.max(n_at_thr) <= topk)
    lax.cond(exact_fit, emit_plain, emit_ranked)

    def fill(kc, carry):
        mask_ref[0, kc] = jnp.full((CHUNK, CHUNK), NEG, F32)
        return carry

    lax.fori_loop(j + 1, nchunk, fill, 0)


def _a_index(qi, wit, ki, topk):
    bsz, L, _ = qi.shape
    nchunk = L // CHUNK
    tri = jnp.tril(jnp.ones((CHUNK, CHUNK), BF16), -1)
    return pl.pallas_call(
        functools.partial(_a_index_kernel, topk=topk, seq_len=L),
        grid=(bsz, nchunk),
        in_specs=[pl.BlockSpec((1, CHUNK, IDX_HEADS * IDX_DIM), lambda b, j: (b, j, 0)),
                  pl.BlockSpec((1, IDX_HEADS, CHUNK), lambda b, j: (b, 0, j)),
                  pl.BlockSpec((1, L, IDX_DIM), lambda b, j: (b, 0, 0)),
                  pl.BlockSpec((CHUNK, CHUNK), lambda b, j: (0, 0))],
        out_specs=pl.BlockSpec((1, nchunk, CHUNK, CHUNK), lambda b, j: (b, 0, j, 0)),
        out_shape=jax.ShapeDtypeStruct((bsz, nchunk, L, CHUNK), F32),
        scratch_shapes=[pltpu.VMEM((nchunk, CHUNK, CHUNK), I32),
                        pltpu.VMEM((nchunk, CHUNK, CHUNK), I16)],
        compiler_params=_cparams("parallel", "parallel"),
        name="a_index",
    )(qi, wit, ki, tri)


def _a_attn_kernel(q_ref, ckv_ref, ckvt_ref, mask_ref, wuk_ref, wuv_ref, bias_ref, o_ref,
                   qabs_scr, lg_scr, m_scr, l_scr, acc_scr, p_scr, *, hc, nq, n_work):
    s = pl.program_id(0)
    j = (jnp.minimum(s, n_work - 1) // (N_HEADS // hc)) % nq
    slot = s & 1
    tq = CHUNK

    def placeholder(r, carry):
        rows = pl.ds(pl.multiple_of(r * ACC_ROWS, ACC_ROWS), ACC_ROWS)
        for k in range(2):
            l_scr[k, rows, :] = jnp.ones((ACC_ROWS, LANES), F32)
            acc_scr[k, rows, :] = jnp.zeros((ACC_ROWS, KV_RANK), F32)
        return carry

    lax.fori_loop(0, jnp.where(s == 0, hc * tq // ACC_ROWS, 0), placeholder, 0)

    q = q_ref[0]
    for i in range(hc // 2):
        qa = jnp.dot(q[:, i * 2 * HEAD_DIM:(i + 1) * 2 * HEAD_DIM], wuk_ref[i],
                     preferred_element_type=F32)
        for par in range(2):
            qabs_scr[(2 * i + par) * tq:(2 * i + par + 1) * tq, :] = (
                qa[:, par * KV_RANK:(par + 1) * KV_RANK] * LOG2E).astype(BF16)
    o = (acc_scr[1 - slot] / jnp.sum(l_scr[1 - slot], axis=-1, keepdims=True)).astype(BF16)
    outs = [jnp.dot(jnp.concatenate([o[2 * i * tq:(2 * i + 1) * tq],
                                     o[(2 * i + 1) * tq:(2 * i + 2) * tq]], axis=1),
                    wuv_ref[i], preferred_element_type=F32) for i in range(hc // 2)]
    o_ref[0] = jnp.concatenate(outs, axis=1).astype(BF16)
    qg = qabs_scr[...]

    n_far_pairs = lax.shift_right_arithmetic(j - 1, 1)
    odd = (j & 1) == 0

    def keys(kc, width):
        return ckv_ref[0, pl.ds(kc, width)].reshape(width * CHUNK, KV_RANK)

    def logits(kc, width):
        keys_t = jnp.concatenate([ckvt_ref[0, kc + w] for w in range(width)], axis=1)
        lg = jnp.dot(qg, keys_t, preferred_element_type=F32)
        mk = jnp.concatenate([mask_ref[0, kc + w] for w in range(width)], axis=1)
        return lg + jnp.concatenate([mk] * hc, axis=0)

    def near_bias(i, with_prev):
        d0, d1 = bias_ref[i, 0], bias_ref[i, 1]
        z = jnp.zeros_like(d0)
        top, bot = [d0, z], [d1, d0]
        if with_prev:
            top, bot = [z, d1] + top, [z, z] + bot
        return jnp.concatenate([jnp.concatenate(top, axis=1), jnp.concatenate(bot, axis=1)], axis=0)

    def put_logits(kc, width, lg):
        for w in range(width):
            lg_scr[kc + w] = lg[:, w * CHUNK:(w + 1) * CHUNK]
        m = m_scr[...]
        for c in range(width * CHUNK // LANES):
            m = jnp.maximum(m, lg[:, c * LANES:(c + 1) * LANES])
        m_scr[...] = m

    m_scr[...] = jnp.full(m_scr.shape, -jnp.inf, F32)

    @pl.when(j >= 1)
    def _():
        bias = jnp.concatenate([near_bias(i, True) for i in range(hc)], axis=0)
        put_logits(j - 1, 2, logits(j - 1, 2) + bias)

    def far_pair(i, carry):
        kc = j - 3 - 2 * i
        put_logits(kc, 2, logits(kc, 2))
        return carry

    lax.fori_loop(0, n_far_pairs, far_pair, 0)

    @pl.when(odd)
    def _():
        bias = jnp.concatenate([near_bias(i, False) for i in range(hc)], axis=0)
        put_logits(0, 1, logits(0, 1) + jnp.where(j == 0, bias, 0.0))

    def pv(kc, width, reduce_max, fresh):
        def unless_fresh(old):
            return 0.0 if fresh is True else old if fresh is False else jnp.where(fresh, 0.0, old)

        ck = keys(kc, width)
        for piece in range(hc * tq // ACC_ROWS):
            for r in range(piece * ACC_ROWS // PV_ROWS, (piece + 1) * ACC_ROWS // PV_ROWS):
                rows = slice(r * PV_ROWS, (r + 1) * PV_ROWS)
                m = m_scr[rows, :]
                if reduce_max:
                    m = jnp.broadcast_to(jnp.max(m, axis=-1, keepdims=True), m.shape)
                    m_scr[rows, :] = m
                l = jnp.zeros((PV_ROWS, LANES), F32) if fresh is True else unless_fresh(
                    l_scr[slot, rows, :])
                for w in range(width):
                    for c in range(CHUNK // LANES):
                        col = w * CHUNK + c * LANES
                        p = jnp.exp2(lg_scr[kc + w, rows, c * LANES:(c + 1) * LANES] - m)
                        l = l + p
                        p_scr[rows, col:col + LANES] = p.astype(BF16)
                l_scr[slot, rows, :] = l
            rows = slice(piece * ACC_ROWS, (piece + 1) * ACC_ROWS)
            part = jnp.dot(p_scr[rows, :width * CHUNK], ck, preferred_element_type=F32)
            acc_scr[slot, rows, :] = part if fresh is True else unless_fresh(
                acc_scr[slot, rows, :]) + part

    @pl.when(j >= 1)
    def _():
        pv(j - 1, 2, True, True)

    def far_pv(i, carry):
        pv(j - 3 - 2 * i, 2, False, False)
        return carry

    lax.fori_loop(0, n_far_pairs, far_pv, 0)

    @pl.when(odd)
    def _():
        pv(0, 1, True, j == 0)


def _a_attn(q, ckv, ckvt, mask, w_uk, w_uv, layer, bias_nd, hc=8):
    bsz, L, _ = q.shape
    nchunk = L // CHUNK
    hg = N_HEADS // hc
    m = hc * CHUNK
    ckv4 = ckv.reshape(bsz, nchunk, CHUNK, KV_RANK)
    n_work = bsz * nchunk * hg

    def item(s):
        return s // (nchunk * hg), (s // hg) % nchunk, s % hg

    cur = lambda s: item(jnp.minimum(s, n_work - 1))
    prev = lambda s: item(jnp.maximum(s - 1, 0))
    return pl.pallas_call(
        functools.partial(_a_attn_kernel, hc=hc, nq=nchunk, n_work=n_work),
        grid=(n_work + 1,),
        in_specs=[pl.BlockSpec((1, CHUNK, hc * HEAD_DIM), lambda s: cur(s)),
                  pl.BlockSpec((1, nchunk, CHUNK, KV_RANK), lambda s: (cur(s)[0], 0, 0, 0)),
                  pl.BlockSpec((1, nchunk, KV_RANK, CHUNK), lambda s: (cur(s)[0], 0, 0, 0)),
                  pl.BlockSpec((1, nchunk, CHUNK, CHUNK), lambda s: (cur(s)[0], 0, cur(s)[1], 0)),
                  pl.BlockSpec((None, hc // 2) + w_uk.shape[2:], lambda s: (layer, cur(s)[2], 0, 0)),
                  pl.BlockSpec((None, hc // 2) + w_uv.shape[2:], lambda s: (layer, prev(s)[2], 0, 0)),
                  pl.BlockSpec((hc, 2, BIAS_BLK, BIAS_BLK), lambda s: (cur(s)[2], 0, 0, 0))],
        out_specs=pl.BlockSpec((1, CHUNK, hc * HEAD_DIM), lambda s: prev(s)),
        out_shape=jax.ShapeDtypeStruct((bsz, L, HD), BF16),
        scratch_shapes=[pltpu.VMEM((m, KV_RANK), BF16),
                        pltpu.VMEM((nchunk, m, CHUNK), F32),
                        pltpu.VMEM((m, LANES), F32),
                        pltpu.VMEM((2, m, LANES), F32),
                        pltpu.VMEM((2, m, KV_RANK), F32),
                        pltpu.VMEM((m, 2 * CHUNK), BF16)],
        compiler_params=_cparams("arbitrary"),
        name="a_attn",
    )(q, ckv4, ckvt, mask, w_uk, w_uv, bias_nd)


def _bias_kernel(rb_ref, ids_ref, o_ref, *, n_tiles, far_bucket, shift_far):
    h = pl.program_id(0)
    far = rb_ref[far_bucket, h] if shift_far else 0.0
    for t in range(n_tiles):
        ids = ids_ref[t]
        out = jnp.zeros(ids.shape, F32)
        for b in range(N_BUCKETS):
            out = jnp.where(ids == b, rb_ref[b, h] - far, out)
        o_ref[0, t] = jnp.where(ids < 0, NEG, out * LOG2E)


def _t5_bucket(dist):
    max_exact = N_BUCKETS // 2
    d = jnp.maximum(dist, 0)
    large = max_exact + (jnp.log(jnp.maximum(d, 1).astype(F32) / max_exact)
                         / math.log(MAX_DISTANCE / max_exact)
                         * (N_BUCKETS - max_exact)).astype(I32)
    large = jnp.minimum(large, N_BUCKETS - 1)
    return jnp.where(d < max_exact, d, large)


def _bias_tiles(rel_bias, ids, shift_far):
    n_tiles, r, c = ids.shape
    return pl.pallas_call(
        functools.partial(_bias_kernel, n_tiles=n_tiles, far_bucket=N_BUCKETS - 1,
                          shift_far=shift_far),
        grid=(N_HEADS,),
        in_specs=[pl.BlockSpec(memory_space=pltpu.SMEM),
                  pl.BlockSpec((n_tiles, r, c), lambda h: (0, 0, 0))],
        out_specs=pl.BlockSpec((1, n_tiles, r, c), lambda h: (h, 0, 0, 0)),
        out_shape=jax.ShapeDtypeStruct((N_HEADS, n_tiles, r, c), F32),
        compiler_params=_cparams("parallel"),
        name="bias_tiles",
    )(rel_bias, ids)


def _b_inproj_kernel(x_ref, g_ref, sh_ref, sc_ref, w_ref, b_ref, q_ref, k_ref, v_ref):
    h = _normmod(x_ref[0], g_ref[...], sh_ref[0], sc_ref[0]).astype(BF16)
    proj = jnp.dot(h, w_ref[...], preferred_element_type=F32) + b_ref[...]
    q_ref[0] = (proj[:, :HD] * (HEAD_DIM ** -0.5 * LOG2E)).astype(BF16)
    k_ref[0] = proj[:, HD:HD + KV_HEADS * HEAD_DIM].astype(BF16)
    v_ref[0] = proj[:, HD + KV_HEADS * HEAD_DIM:].astype(BF16)


def _b_inproj(x, g, sh, sc, w, layer, b, tm=512):
    bsz, L, d = x.shape
    kvw = KV_HEADS * HEAD_DIM
    row = lambda b_, i: (b_, i, 0)
    per_b = lambda b_, i: (b_, 0, 0)
    const2 = lambda b_, i: (0, 0)
    return pl.pallas_call(
        _b_inproj_kernel,
        grid=(bsz, L // tm),
        in_specs=[pl.BlockSpec((1, tm, d), row),
                  pl.BlockSpec((1, d), const2),
                  pl.BlockSpec((1, 1, d), per_b),
                  pl.BlockSpec((1, 1, d), per_b),
                  _layer_spec(w, layer),
                  pl.BlockSpec((1, B_IN), const2)],
        out_specs=[pl.BlockSpec((1, tm, HD), row),
                   pl.BlockSpec((1, tm, kvw), row),
                   pl.BlockSpec((1, tm, kvw), row)],
        out_shape=[jax.ShapeDtypeStruct((bsz, L, HD), BF16),
                   jax.ShapeDtypeStruct((bsz, L, kvw), BF16),
                   jax.ShapeDtypeStruct((bsz, L, kvw), BF16)],
        compiler_params=_cparams("parallel", "parallel"),
        name="b_inproj",
    )(x, g, sh, sc, w, b)


def _b_window_attention(sink_ref, q_block, kp, kc, vp, vc, bias_ref, first):
    w = WINDOW
    pairs = N_HEADS // KV_HEADS // 2
    kall = jnp.concatenate([kp, kc], axis=0).astype(F32)
    vall = jnp.concatenate([vp, vc], axis=0).astype(F32)
    low = lax.broadcasted_iota(I32, kall.shape, 1) < HEAD_DIM

    def padded(x, g):
        swapped = pltpu.roll(x, HEAD_DIM, axis=1)
        on_low, on_high = (x, swapped) if g == 0 else (swapped, x)
        return (jnp.where(low, on_low, 0.0).astype(BF16), jnp.where(low, 0.0, on_high).astype(BF16))

    kpad = [padded(kall, g) for g in range(KV_HEADS)]
    vpad = [padded(vall, g) for g in range(KV_HEADS)]

    def attend(blk, g):
        variant = jnp.where(first, 1, 0) if blk == 0 else 0
        keys = slice(blk * w, (blk + 2) * w)
        q = q_block(blk)
        blocks = [q[:, (pairs * g + p) * 2 * HEAD_DIM:(pairs * g + p + 1) * 2 * HEAD_DIM]
                  for p in range(pairs)]
        lg = lax.dot_general(jnp.concatenate(blocks, axis=0),
                             jnp.concatenate([kpad[g][0][keys], kpad[g][1][keys]], axis=0),
                             _NT, preferred_element_type=F32)
        lg = lg + bias_ref[variant, g]
        probs = [[], []]
        for p in range(pairs):
            for par in range(2):
                h = 2 * pairs * g + 2 * p + par
                t = lg[p * w:(p + 1) * w, par * 2 * w:(par + 1) * 2 * w]
                sink = sink_ref[0, h] * LOG2E
                m = jnp.maximum(jnp.max(t, axis=-1, keepdims=True), sink)
                e = jnp.exp2(t - m)
                denom = jnp.sum(e, axis=-1, keepdims=True) + jnp.exp2(sink - m)
                probs[par].append((e * (1.0 / denom)).astype(BF16))
        og = (jnp.dot(jnp.concatenate(probs[0], axis=0), vpad[g][0][keys],
                      preferred_element_type=F32)
              + jnp.dot(jnp.concatenate(probs[1], axis=0), vpad[g][1][keys],
                        preferred_element_type=F32))
        return jnp.concatenate([og[p * w:(p + 1) * w] for p in range(pairs)], axis=1).astype(BF16)

    return attend


FF_CHUNK = 256


def _out_ffn_tile(o, wo_ref, bo_ref, g1_ref, x_ref, g_ref, sh_ref, sc_ref, gate_ref,
                  w1_ref, w3_ref, w2_ref, fg_ref, y_ref, final_norm):
    mix = jnp.dot(o, wo_ref[...], preferred_element_type=F32) + bo_ref[...]
    x = x_ref[0] + g1_ref[0] * mix
    h = _normmod(x, g_ref[...], sh_ref[0], sc_ref[0]).astype(BF16)
    acc = jnp.zeros(x.shape, F32)
    for c in range(D_FF // FF_CHUNK):
        cs = slice(c * FF_CHUNK, (c + 1) * FF_CHUNK)
        a1 = jnp.dot(h, w1_ref[:, cs], preferred_element_type=F32)
        a3 = jnp.dot(h, w3_ref[:, cs], preferred_element_type=F32)
        act = (a1 * jax.nn.sigmoid(a1)) * a3
        acc = acc + jnp.dot(act.astype(BF16), w2_ref[cs, :], preferred_element_type=F32)
    y = x + gate_ref[0] * acc
    if final_norm:
        y = (y * lax.rsqrt(jnp.mean(y * y, axis=-1, keepdims=True) + RMS_EPS)) * fg_ref[...]
    y_ref[0] = y


def _out_ffn_kernel(o_ref, *refs, final_norm):
    _out_ffn_tile(o_ref[0], *refs, final_norm)


def _b_attn_ffn_kernel(sink_ref, q_ref, kp_ref, kc_ref, vp_ref, vc_ref, bias_ref, *refs,
                       final_norm, n_blocks, tiles_per_seq, n_work):
    ffn_refs, o_scr = refs[:-1], refs[-1]
    s = pl.program_id(0)
    slot = s & 1
    w = WINDOW

    def placeholder(r, carry):
        o_scr[1, pl.ds(pl.multiple_of(r * w, w), w), :] = jnp.zeros((w, HD), BF16)
        return carry

    lax.fori_loop(0, jnp.where(s == 0, n_blocks, 0), placeholder, 0)

    first = lax.rem(jnp.minimum(s, n_work - 1), tiles_per_seq) == 0
    attend = _b_window_attention(sink_ref, lambda blk: q_ref[0, blk * w:(blk + 1) * w, :],
                                 kp_ref[0], kc_ref[0], vp_ref[0], vc_ref[0], bias_ref, first)

    _out_ffn_tile(o_scr[1 - slot], *ffn_refs, final_norm)
    gw = HD // KV_HEADS
    for blk in range(n_blocks):
        for g in range(KV_HEADS):
            o_scr[slot, blk * w:(blk + 1) * w, g * gw:(g + 1) * gw] = attend(blk, g)


def _b_attn_ffn(q, k, v, sinks, bias_b, w_out, mix_layer, b_out, g1, x, g, sh, sc, gate, w1, w3,
                w2, layer, final_g, final_norm, tm=512):
    bsz, L, d = x.shape
    w = WINDOW
    kvw = KV_HEADS * HEAD_DIM
    tiles = L // tm
    n_work = bsz * tiles

    def tile(s):
        return s // tiles, s % tiles

    cur = lambda s: tile(jnp.minimum(s, n_work - 1))
    prev = lambda s: tile(jnp.maximum(s - 1, 0))
    cur_row = lambda s: cur(s) + (0,)
    before = lambda s: (cur(s)[0], jnp.maximum(cur(s)[1] * (tm // w) - 1, 0), 0)
    prev_row = lambda s: prev(s) + (0,)
    prev_b = lambda s: (prev(s)[0], 0, 0)
    const2 = lambda s: (0, 0)
    return pl.pallas_call(
        functools.partial(_b_attn_ffn_kernel, final_norm=final_norm, n_blocks=tm // w,
                          tiles_per_seq=tiles, n_work=n_work),
        grid=(n_work + 1,),
        in_specs=[pl.BlockSpec(memory_space=pltpu.SMEM),
                  pl.BlockSpec((1, tm, HD), cur_row),
                  pl.BlockSpec((1, w, kvw), before),
                  pl.BlockSpec((1, tm, kvw), cur_row),
                  pl.BlockSpec((1, w, kvw), before),
                  pl.BlockSpec((1, tm, kvw), cur_row),
                  pl.BlockSpec(bias_b.shape, lambda s: (0, 0, 0, 0)),
                  _layer_spec(w_out, mix_layer),
                  pl.BlockSpec((1, d), const2),
                  pl.BlockSpec((1, 1, d), prev_b),
                  pl.BlockSpec((1, tm, d), prev_row),
                  pl.BlockSpec((1, d), const2),
                  pl.BlockSpec((1, 1, d), prev_b),
                  pl.BlockSpec((1, 1, d), prev_b),
                  pl.BlockSpec((1, 1, d), prev_b),
                  _layer_spec(w1, layer),
                  _layer_spec(w3, layer),
                  _layer_spec(w2, layer),
                  pl.BlockSpec((1, d), const2)],
        out_specs=pl.BlockSpec((1, tm, d), prev_row),
        out_shape=jax.ShapeDtypeStruct((bsz, L, d), F32),
        scratch_shapes=[pltpu.VMEM((2, tm, HD), BF16)],
        compiler_params=_cparams("arbitrary"),
        name="b_attn_ffn",
    )(sinks.reshape(1, N_HEADS), q, k, k, v, v, bias_b, w_out, b_out, g1, x, g, sh, sc, gate,
      w1, w3, w2, final_g)


def _out_ffn(o, w_out, mix_layer, b_out, g1, x, g, sh, sc, gate, w1, w3, w2, layer, final_g,
             final_norm, tm=512):
    bsz, L, d = x.shape
    row = lambda b_, i: (b_, i, 0)
    per_b = lambda b_, i: (b_, 0, 0)
    const2 = lambda b_, i: (0, 0)
    return pl.pallas_call(
        functools.partial(_out_ffn_kernel, final_norm=final_norm),
        grid=(bsz, L // tm),
        in_specs=[pl.BlockSpec((1, tm, HD), row),
                  _layer_spec(w_out, mix_layer),
                  pl.BlockSpec((1, d), const2),
                  pl.BlockSpec((1, 1, d), per_b),
                  pl.BlockSpec((1, tm, d), row),
                  pl.BlockSpec((1, d), const2),
                  pl.BlockSpec((1, 1, d), per_b),
                  pl.BlockSpec((1, 1, d), per_b),
                  pl.BlockSpec((1, 1, d), per_b),
                  _layer_spec(w1, layer),
                  _layer_spec(w3, layer),
                  _layer_spec(w2, layer),
                  pl.BlockSpec((1, d), const2)],
        out_specs=pl.BlockSpec((1, tm, d), row),
        out_shape=jax.ShapeDtypeStruct((bsz, L, d), F32),
        compiler_params=_cparams("parallel", "parallel"),
        name="out_ffn",
    )(o, w_out, b_out, g1, x, g, sh, sc, gate, w1, w3, w2, final_g)


def kernel(x, c, rel_bias, w_ada, b_ada, norm_mix_g, norm_ffn_g, a_w_in, a_kv_norm_g, a_w_uk,
           a_w_uv, a_idx_k_g, a_idx_k_b, a_w_out, b_w_in, b_b_in, b_sinks, b_w_out, b_b_out,
           ffn_w1, ffn_w3, ffn_w2, norm_final_g):
    bsz, L, d = x.shape
    depth = w_ada.shape[0]
    topk = min(INDEX_TOPK, L // 4)

    mod = _adaln(c, w_ada, b_ada)

    r = jnp.arange(BIAS_BLK)
    ids_a = jnp.stack([_t5_bucket(r[:, None] - r[None, :]),
                       _t5_bucket(r[:, None] - r[None, :] + BIAS_BLK)])
    bias_a = _bias_tiles(rel_bias, ids_a, shift_far=True)
    dist_b = jnp.arange(WINDOW)[:, None] + WINDOW - jnp.arange(2 * WINDOW)[None, :]
    in_window = (dist_b >= 0) & (dist_b < WINDOW)
    ids_b = jnp.where(in_window, _t5_bucket(dist_b), -1)
    ids_b0 = jnp.where(jnp.arange(2 * WINDOW)[None, :] >= WINDOW, ids_b, -1)
    bias_b = _bias_tiles(rel_bias, jnp.stack([ids_b, ids_b0]), shift_far=False)
    pairs = N_HEADS // KV_HEADS // 2
    bias_b = bias_b.reshape(KV_HEADS, pairs, 2, 2, WINDOW, 2 * WINDOW).transpose(3, 0, 1, 4, 2, 5)
    bias_b = bias_b.reshape(2, KV_HEADS, pairs * WINDOW, 4 * WINDOW)

    a_w_main = jnp.pad(a_w_in[:, :, :A_MAIN], ((0, 0), (0, 0), (0, A_MAIN_PAD - A_MAIN))).astype(BF16)
    a_w_wi_t = jnp.pad(a_w_in[:, :, A_WI0:A_WI0 + IDX_HEADS].transpose(0, 2, 1),
                       ((0, 0), (0, 16 - IDX_HEADS), (0, 0))).astype(BF16)
    a_w_uk, a_w_uv, a_w_out, b_w_in, b_w_out, ffn_w1, ffn_w3, ffn_w2 = [
        w.astype(BF16) for w in (jnp.swapaxes(_pair_blocks(a_w_uk), -1, -2),
                                 _pair_blocks(a_w_uv), a_w_out, b_w_in,
                                 b_w_out, ffn_w1, ffn_w3, ffn_w2)]

    zero_bias = jnp.zeros((1, d), F32)
    for i in range(depth):
        sh1, sc1, g1, sh2, sc2, g2 = [m.reshape(bsz, 1, d) for m in jnp.split(mod[i], 6, axis=-1)]
        jm = i // 2
        ffn_args = (g1, x, norm_ffn_g[i][None], sh2, sc2, g2, ffn_w1, ffn_w3, ffn_w2, i,
                    norm_final_g[None], i == depth - 1)
        if i % 2 == 0:
            q, ckv, ckvt, qi, ki, wit = _a_inproj(
                x, norm_mix_g[i][None], sh1, sc1, a_w_main, a_w_wi_t, jm, a_kv_norm_g[jm][None],
                a_idx_k_g[jm][None], a_idx_k_b[jm][None])
            mask = _a_index(qi, wit, ki, topk)
            o = _a_attn(q, ckv, ckvt, mask, a_w_uk, a_w_uv, jm, bias_a)
            x = _out_ffn(o, a_w_out, jm, zero_bias, *ffn_args)
        else:
            q, k, v = _b_inproj(x, norm_mix_g[i][None], sh1, sc1, b_w_in, jm, b_b_in[jm][None])
            x = _b_attn_ffn(q, k, v, b_sinks[jm], bias_b, b_w_out, jm, b_b_out[jm][None], *ffn_args)
    return x
```

```python
import functools
import math

import numpy as np
import jax
import jax.numpy as jnp
from jax import lax
from jax.experimental import pallas as pl
from jax.experimental.pallas import tpu as pltpu

N_HEADS = 16
HEAD_DIM = 64
KV_RANK = 256
IDX_HEADS = 8
IDX_DIM = 64
INDEX_TOPK = 256
KV_HEADS = 2
WINDOW = 128
N_BUCKETS = 32
MAX_DISTANCE = 128
D_FF = 2816
RMS_EPS = 1e-6
NEG = -1e30

HD = N_HEADS * HEAD_DIM
A_Q0, A_KV0, A_QI0, A_KI0, A_WI0 = 0, HD, HD + KV_RANK, HD + KV_RANK + IDX_HEADS * IDX_DIM, \
    HD + KV_RANK + IDX_HEADS * IDX_DIM + IDX_DIM
LANES = 128
HALF_ROWS = 16
A_MAIN = A_WI0
A_MAIN_PAD = -(-A_MAIN // LANES) * LANES
B_IN = (N_HEADS + 2 * KV_HEADS) * HEAD_DIM

CHUNK = 256
PV_ROWS = 64
ACC_ROWS = 256
BIAS_BLK = 128
VMEM_LIMIT = 56 * 1024 * 1024

F32 = jnp.float32
BF16 = jnp.bfloat16
I32 = jnp.int32
I16 = jnp.int16

_NT = (((1,), (1,)), ((), ()))


def _cparams(*sem):
    return pltpu.CompilerParams(dimension_semantics=sem, vmem_limit_bytes=VMEM_LIMIT)


def _pair_blocks(w):
    even, odd = w[..., 0::2, :, :], w[..., 1::2, :, :]
    zero = jnp.zeros_like(even)
    return jnp.concatenate([jnp.concatenate([even, zero], axis=-1),
                            jnp.concatenate([zero, odd], axis=-1)], axis=-2)


def _layer_spec(stack, layer):
    zeros = (0,) * (stack.ndim - 1)
    return pl.BlockSpec((None,) + stack.shape[1:], lambda *_: (layer,) + zeros)


def _f32_key(v):
    b = int(np.array(v, np.float32).view(np.int32))
    return b ^ ((b >> 31) & 0x7FFFFFFF)


LOG2E = math.log2(math.e)
KEY_NEG = _f32_key(NEG)
INT_MIN = -(2 ** 31)


def _adaln_kernel(c_ref, w_ref, b_ref, o_ref):
    c = c_ref[...]
    cs = c * jax.nn.sigmoid(c)
    o_ref[0] = jnp.dot(cs, w_ref[0], preferred_element_type=F32,
                       precision=lax.Precision.HIGHEST) + b_ref[0]


def _adaln(c, w_ada, b_ada):
    depth, d, n = w_ada.shape
    bsz = c.shape[0]
    tn = 1536
    return pl.pallas_call(
        _adaln_kernel,
        grid=(depth, n // tn),
        in_specs=[pl.BlockSpec((bsz, d), lambda i, j: (0, 0)),
                  pl.BlockSpec((1, d, tn), lambda i, j: (i, 0, j)),
                  pl.BlockSpec((1, 1, tn), lambda i, j: (i, 0, j))],
        out_specs=pl.BlockSpec((1, bsz, tn), lambda i, j: (i, 0, j)),
        out_shape=jax.ShapeDtypeStruct((depth, bsz, n), F32),
        compiler_params=_cparams("parallel", "parallel"),
        name="adaln",
    )(c, w_ada, b_ada.reshape(depth, 1, n))


def _normmod(x, g, sh, sc):
    ms = jnp.mean(x * x, axis=-1, keepdims=True)
    y = (x * lax.rsqrt(ms + RMS_EPS)) * g
    return y * (1.0 + sc) + sh


def _a_inproj_kernel(x_ref, g_ref, sh_ref, sc_ref, w_ref, wwi_ref, kvg_ref, ikg_ref, ikb_ref,
                     q_ref, ckv_ref, ckvt_ref, qi_ref, ki_ref, wit_ref):
    h = _normmod(x_ref[0], g_ref[...], sh_ref[0], sc_ref[0]).astype(BF16)
    proj = jnp.dot(h, w_ref[...], preferred_element_type=F32)
    q_ref[0] = (proj[:, A_Q0:A_KV0] * (HEAD_DIM ** -0.5)).astype(BF16)
    ckv = proj[:, A_KV0:A_QI0]
    ckv = (ckv * lax.rsqrt(jnp.mean(ckv * ckv, axis=-1, keepdims=True) + RMS_EPS)) * kvg_ref[...]
    ckv_ref[0] = ckv.astype(BF16)
    for c in range(ckvt_ref.shape[1]):
        ckvt_ref[0, c] = ckv[c * CHUNK:(c + 1) * CHUNK].T.astype(BF16)
    qi_ref[0] = proj[:, A_QI0:A_KI0].astype(BF16)
    ki = proj[:, A_KI0:A_WI0]
    mu = jnp.mean(ki, axis=-1, keepdims=True)
    var = jnp.mean(jnp.square(ki - mu), axis=-1, keepdims=True)
    ki = ((ki - mu) * lax.rsqrt(var + RMS_EPS)) * ikg_ref[...] + ikb_ref[...]
    ki_ref[0] = ki.astype(BF16)
    wit = lax.dot_general(wwi_ref[...], h, _NT, preferred_element_type=F32)
    wit_ref[0] = wit[:IDX_HEADS] * (IDX_HEADS ** -0.5 * IDX_DIM ** -0.5)


def _a_inproj(x, g, sh, sc, w_main, w_wi_t, layer, kv_g, ik_g, ik_b, tm=512):
    bsz, L, d = x.shape
    row = lambda b, i: (b, i, 0)
    per_b = lambda b, i: (b, 0, 0)
    const2 = lambda b, i: (0, 0)
    return pl.pallas_call(
        _a_inproj_kernel,
        grid=(bsz, L // tm),
        in_specs=[pl.BlockSpec((1, tm, d), row),
                  pl.BlockSpec((1, d), const2),
                  pl.BlockSpec((1, 1, d), per_b),
                  pl.BlockSpec((1, 1, d), per_b),
                  _layer_spec(w_main, layer),
                  _layer_spec(w_wi_t, layer),
                  pl.BlockSpec((1, KV_RANK), const2),
                  pl.BlockSpec((1, IDX_DIM), const2),
                  pl.BlockSpec((1, IDX_DIM), const2)],
        out_specs=[pl.BlockSpec((1, tm, HD), row),
                   pl.BlockSpec((1, tm, KV_RANK), row),
                   pl.BlockSpec((1, tm // CHUNK, KV_RANK, CHUNK), lambda b, i: (b, i, 0, 0)),
                   pl.BlockSpec((1, tm, IDX_HEADS * IDX_DIM), row),
                   pl.BlockSpec((1, tm, IDX_DIM), row),
                   pl.BlockSpec((1, IDX_HEADS, tm), lambda b, i: (b, 0, i))],
        out_shape=[jax.ShapeDtypeStruct((bsz, L, HD), BF16),
                   jax.ShapeDtypeStruct((bsz, L, KV_RANK), BF16),
                   jax.ShapeDtypeStruct((bsz, L // CHUNK, KV_RANK, CHUNK), BF16),
                   jax.ShapeDtypeStruct((bsz, L, IDX_HEADS * IDX_DIM), BF16),
                   jax.ShapeDtypeStruct((bsz, L, IDX_DIM), BF16),
                   jax.ShapeDtypeStruct((bsz, IDX_HEADS, L), F32)],
        compiler_params=_cparams("parallel", "parallel"),
        name="a_inproj",
    )(x, g, sh, sc, w_main, w_wi_t, kv_g, ik_g, ik_b)


def _a_index_kernel(qi_ref, wit_ref, ki_ref, tri_ref, mask_ref, key_scr, half_scr, *, topk,
                    seq_len):
    j = pl.program_id(1)
    nchunk = seq_len // CHUNK
    qi = qi_ref[0]
    wit = wit_ref[0]
    t_glob = j * CHUNK + lax.broadcasted_iota(I32, (CHUNK, CHUNK), 1)
    s_loc = lax.broadcasted_iota(I32, (CHUNK, CHUNK), 0)

    def chunks(fn, init):
        def pair(i, carry):
            return fn(2 * i + 1, fn(2 * i, carry))
        carry = lax.fori_loop(0, lax.shift_right_logical(j + 1, 1), pair, init)
        return lax.cond((j & 1) == 0, lambda c: fn(j, c), lambda c: c, carry)

    def score_chunk(kc, carry):
        kik = ki_ref[0, pl.ds(pl.multiple_of(kc * CHUNK, CHUNK), CHUNK), :]
        acc = jnp.zeros((CHUNK, CHUNK), F32)
        for h in range(IDX_HEADS):
            r = lax.dot_general(kik, qi[:, h * IDX_DIM:(h + 1) * IDX_DIM], _NT,
                                preferred_element_type=F32)
            acc = acc + jnp.maximum(r, 0.0) * wit[h:h + 1, :]
        sc = jnp.where(kc * CHUNK + s_loc <= t_glob, acc, NEG)
        bits = lax.bitcast_convert_type(sc, I32)
        key = bits ^ ((bits >> 31) & 0x7FFFFFFF)
        key_scr[kc] = key
        half_scr[kc] = (key >> 16).astype(I16)
        return carry

    chunks(score_chunk, 0)

    n_beyond = seq_len - (j + 1) * CHUNK
    ge = lambda a, b: a >= b
    gt = lambda a, b: a > b

    def count(pred, thr):
        def body(kc, acc):
            m = jnp.where(pred(key_scr[kc], thr), jnp.int32(1), jnp.int32(0))
            return acc + m.reshape(CHUNK // 8, 8, CHUNK).sum(axis=0)
        acc = chunks(body, jnp.zeros((8, CHUNK), I32))
        cnt = acc.sum(axis=0, keepdims=True)
        return cnt + jnp.where(pred(KEY_NEG, thr), n_beyond, 0)

    def count_half(pred, thr16):
        t = jnp.broadcast_to(thr16, (HALF_ROWS, CHUNK)).astype(I16)
        def body(kc, acc):
            m = jnp.where(pred(half_scr[kc].reshape(CHUNK // HALF_ROWS, HALF_ROWS, CHUNK), t[None]),
                          jnp.int16(1), jnp.int16(0))
            for r in range(CHUNK // HALF_ROWS):
                acc = acc + m[r]
            return acc
        acc = chunks(body, jnp.zeros((HALF_ROWS, CHUNK), I16))
        return acc.astype(I32).sum(axis=0, keepdims=True)

    def beyond(cand):
        return jnp.where(KEY_NEG >= cand, n_beyond, 0)

    lowest = jnp.full((1, CHUNK), INT_MIN, I32), jnp.full((1, CHUNK), seq_len, I32)

    def step(state, cand, cnt):
        keep = cnt >= topk
        return jnp.where(keep, cand, state[0]), jnp.where(keep, cnt, state[1])

    def bisect():
        zero = jnp.zeros((1, CHUNK), I32)
        state = step(lowest, zero, count_half(ge, zero) + beyond(zero))

        def bisect_high(i, state):
            cand = state[0] + lax.shift_left(jnp.int32(1), 30 - i)
            return step(state, cand, count_half(ge, cand >> 16) + beyond(cand))

        state = lax.fori_loop(0, 15, bisect_high, state)
        thr = state[0]

        thr_hi = thr >> 16
        n_above = count_half(gt, thr_hi)

        def low_half(kc, carry):
            key = key_scr[kc]
            low = (key & 0xFFFF) - 0x8000
            half_scr[kc] = jnp.where((key >> 16) == thr_hi, low, -0x8000).astype(I16)
            return carry

        chunks(low_half, 0)

        def bisect_low(i, state):
            cand = state[0] + lax.shift_left(jnp.int32(1), 15 - i)
            cnt = n_above + count_half(ge, (cand & 0xFFFF) - 0x8000) + beyond(cand)
            return step(state, cand, cnt)

        return lax.fori_loop(0, 16, bisect_low, state)

    takes_all = (j + 1) * CHUNK <= topk
    thr, n_at_thr = lax.cond(takes_all, lambda: lowest, bisect)

    def emit_ranked():
        need = (topk - count(gt, thr)).astype(F32)
        tri = tri_ref[...]

        def emit(kc, seen):
            key = key_scr[kc]
            eq = jnp.where(key == thr, 1.0, 0.0)
            rank = jnp.dot(tri, eq.astype(BF16), preferred_element_type=F32) + seen
            take = jnp.where(key > thr, 1.0, jnp.where(rank < need, eq, 0.0))
            causal = kc * CHUNK + s_loc <= t_glob
            m = jnp.where(causal, jnp.where(take > 0.5, 0.0, NEG), NEG)
            mask_ref[0, kc] = m.T
            return seen + jnp.sum(eq, axis=0, keepdims=True)

        chunks(emit, jnp.zeros((1, CHUNK), F32))

    def emit_plain():
        def emit(kc, carry):
            causal = kc * CHUNK + s_loc <= t_glob
            m = jnp.where(causal, jnp.where(key_scr[kc] >= thr, 0.0, NEG), NEG)
            mask_ref[0, kc] = m.T
            return carry

        chunks(emit, 0)

    exact_fit = takes_all | (jnp.max(n_at_thr) <= topk)
    lax.cond(exact_fit, emit_plain, emit_ranked)

    def fill(kc, carry):
        mask_ref[0, kc] = jnp.full((CHUNK, CHUNK), NEG, F32)
        return carry

    lax.fori_loop(j + 1, nchunk, fill, 0)


def _a_index(qi, wit, ki, topk):
    bsz, L, _ = qi.shape
    nchunk = L // CHUNK
    tri = jnp.tril(jnp.ones((CHUNK, CHUNK), BF16), -1)
    return pl.pallas_call(
        functools.partial(_a_index_kernel, topk=topk, seq_len=L),
        grid=(bsz, nchunk),
        in_specs=[pl.BlockSpec((1, CHUNK, IDX_HEADS * IDX_DIM), lambda b, j: (b, j, 0)),
                  pl.BlockSpec((1, IDX_HEADS, CHUNK), lambda b, j: (b, 0, j)),
                  pl.BlockSpec((1, L, IDX_DIM), lambda b, j: (b, 0, 0)),
                  pl.BlockSpec((CHUNK, CHUNK), lambda b, j: (0, 0))],
        out_specs=pl.BlockSpec((1, nchunk, CHUNK, CHUNK), lambda b, j: (b, 0, j, 0)),
        out_shape=jax.ShapeDtypeStruct((bsz, nchunk, L, CHUNK), F32),
        scratch_shapes=[pltpu.VMEM((nchunk, CHUNK, CHUNK), I32),
                        pltpu.VMEM((nchunk, CHUNK, CHUNK), I16)],
        compiler_params=_cparams("parallel", "parallel"),
        name="a_index",
    )(qi, wit, ki, tri)


def _a_attn_kernel(q_ref, ckv_ref, ckvt_ref, mask_ref, wuk_ref, wuv_ref, bias_ref, o_ref,
                   qabs_scr, lg_scr, m_scr, l_scr, acc_scr, p_scr, *, hc, nq, n_work):
    s = pl.program_id(0)
    j = (jnp.minimum(s, n_work - 1) // (N_HEADS // hc)) % nq
    slot = s & 1
    tq = CHUNK

    def placeholder(r, carry):
        rows = pl.ds(pl.multiple_of(r * ACC_ROWS, ACC_ROWS), ACC_ROWS)
        for k in range(2):
            l_scr[k, rows, :] = jnp.ones((ACC_ROWS, LANES), F32)
            acc_scr[k, rows, :] = jnp.zeros((ACC_ROWS, KV_RANK), F32)
        m_scr[rows, :] = jnp.zeros((ACC_ROWS, LANES), F32)
        return carry

    lax.fori_loop(0, jnp.where(s == 0, hc * tq // ACC_ROWS, 0), placeholder, 0)

    q = q_ref[0]
    for i in range(hc // 2):
        qa = jnp.dot(q[:, i * 2 * HEAD_DIM:(i + 1) * 2 * HEAD_DIM], wuk_ref[i],
                     preferred_element_type=F32)
        for par in range(2):
            qabs_scr[(2 * i + par) * tq:(2 * i + par + 1) * tq, :] = (
                qa[:, par * KV_RANK:(par + 1) * KV_RANK] * LOG2E).astype(BF16)
    o = (acc_scr[1 - slot] / jnp.sum(l_scr[1 - slot], axis=-1, keepdims=True)).astype(BF16)
    outs = [jnp.dot(jnp.concatenate([o[2 * i * tq:(2 * i + 1) * tq],
                                     o[(2 * i + 1) * tq:(2 * i + 2) * tq]], axis=1),
                    wuv_ref[i], preferred_element_type=F32) for i in range(hc // 2)]
    o_ref[0] = jnp.concatenate(outs, axis=1).astype(BF16)
    qg = qabs_scr[...]

    n_far_pairs = lax.shift_right_arithmetic(j - 1, 1)
    odd = (j & 1) == 0

    def keys(kc, width):
        return ckv_ref[0, pl.ds(kc, width)].reshape(width * CHUNK, KV_RANK)

    def logits(kc, width):
        keys_t = jnp.concatenate([ckvt_ref[0, kc + w] for w in range(width)], axis=1)
        lg = jnp.dot(qg, keys_t, preferred_element_type=F32)
        mk = jnp.concatenate([mask_ref[0, kc + w] for w in range(width)], axis=1)
        return lg + jnp.concatenate([mk] * hc, axis=0)

    def near_bias(i, with_prev):
        d0, d1 = bias_ref[i, 0], bias_ref[i, 1]
        z = jnp.zeros_like(d0)
        top, bot = [d0, z], [d1, d0]
        if with_prev:
            top, bot = [z, d1] + top, [z, z] + bot
        return jnp.concatenate([jnp.concatenate(top, axis=1), jnp.concatenate(bot, axis=1)], axis=0)

    def put_logits(kc, width, lg, fresh):
        for w in range(width):
            lg_scr[kc + w] = lg[:, w * CHUNK:(w + 1) * CHUNK]
        tiles = [lg[:, c * LANES:(c + 1) * LANES] for c in range(width * CHUNK // LANES)]
        if fresh is True:
            m, tiles = tiles[0], tiles[1:]
        elif fresh is False:
            m = m_scr[...]
        else:
            m = jnp.where(fresh, -jnp.inf, m_scr[...])
        for t in tiles:
            m = jnp.maximum(m, t)
        m_scr[...] = m

    @pl.when(j >= 1)
    def _():
        bias = jnp.concatenate([near_bias(i, True) for i in range(hc)], axis=0)
        put_logits(j - 1, 2, logits(j - 1, 2) + bias, True)

    def far_pair(i, carry):
        kc = j - 3 - 2 * i
        put_logits(kc, 2, logits(kc, 2), False)
        return carry

    lax.fori_loop(0, n_far_pairs, far_pair, 0)

    @pl.when(odd)
    def _():
        bias = jnp.concatenate([near_bias(i, False) for i in range(hc)], axis=0)
        put_logits(0, 1, logits(0, 1) + jnp.where(j == 0, bias, 0.0), j == 0)

    def pv(kc, width, reduce_max, fresh):
        def unless_fresh(old):
            return 0.0 if fresh is True else old if fresh is False else jnp.where(fresh, 0.0, old)

        ck = keys(kc, width)
        for piece in range(hc * tq // ACC_ROWS):
            for r in range(piece * ACC_ROWS // PV_ROWS, (piece + 1) * ACC_ROWS // PV_ROWS):
                rows = slice(r * PV_ROWS, (r + 1) * PV_ROWS)
                m = m_scr[rows, :]
                if reduce_max:
                    m = jnp.broadcast_to(jnp.max(m, axis=-1, keepdims=True), m.shape)
                    m_scr[rows, :] = m
                l = jnp.zeros((PV_ROWS, LANES), F32) if fresh is True else unless_fresh(
                    l_scr[slot, rows, :])
                for w in range(width):
                    for c in range(CHUNK // LANES):
                        col = w * CHUNK + c * LANES
                        p = jnp.exp2(lg_scr[kc + w, rows, c * LANES:(c + 1) * LANES] - m)
                        l = l + p
                        p_scr[rows, col:col + LANES] = p.astype(BF16)
                l_scr[slot, rows, :] = l
            rows = slice(piece * ACC_ROWS, (piece + 1) * ACC_ROWS)
            part = jnp.dot(p_scr[rows, :width * CHUNK], ck, preferred_element_type=F32)
            acc_scr[slot, rows, :] = part if fresh is True else unless_fresh(
                acc_scr[slot, rows, :]) + part

    @pl.when(j >= 1)
    def _():
        pv(j - 1, 2, True, True)

    def far_pv(i, carry):
        pv(j - 3 - 2 * i, 2, False, False)
        return carry

    lax.fori_loop(0, n_far_pairs, far_pv, 0)

    @pl.when(odd)
    def _():
        pv(0, 1, True, j == 0)


def _a_attn(q, ckv, ckvt, mask, w_uk, w_uv, layer, bias_nd, hc=8):
    bsz, L, _ = q.shape
    nchunk = L // CHUNK
    hg = N_HEADS // hc
    m = hc * CHUNK
    ckv4 = ckv.reshape(bsz, nchunk, CHUNK, KV_RANK)
    n_work = bsz * nchunk * hg

    def item(s):
        return s // (nchunk * hg), (s // hg) % nchunk, s % hg

    cur = lambda s: item(jnp.minimum(s, n_work - 1))
    prev = lambda s: item(jnp.maximum(s - 1, 0))
    return pl.pallas_call(
        functools.partial(_a_attn_kernel, hc=hc, nq=nchunk, n_work=n_work),
        grid=(n_work + 1,),
        in_specs=[pl.BlockSpec((1, CHUNK, hc * HEAD_DIM), lambda s: cur(s)),
                  pl.BlockSpec((1, nchunk, CHUNK, KV_RANK), lambda s: (cur(s)[0], 0, 0, 0)),
                  pl.BlockSpec((1, nchunk, KV_RANK, CHUNK), lambda s: (cur(s)[0], 0, 0, 0)),
                  pl.BlockSpec((1, nchunk, CHUNK, CHUNK), lambda s: (cur(s)[0], 0, cur(s)[1], 0)),
                  pl.BlockSpec((None, hc // 2) + w_uk.shape[2:], lambda s: (layer, cur(s)[2], 0, 0)),
                  pl.BlockSpec((None, hc // 2) + w_uv.shape[2:], lambda s: (layer, prev(s)[2], 0, 0)),
                  pl.BlockSpec((hc, 2, BIAS_BLK, BIAS_BLK), lambda s: (cur(s)[2], 0, 0, 0))],
        out_specs=pl.BlockSpec((1, CHUNK, hc * HEAD_DIM), lambda s: prev(s)),
        out_shape=jax.ShapeDtypeStruct((bsz, L, HD), BF16),
        scratch_shapes=[pltpu.VMEM((m, KV_RANK), BF16),
                        pltpu.VMEM((nchunk, m, CHUNK), F32),
                        pltpu.VMEM((m, LANES), F32),
                        pltpu.VMEM((2, m, LANES), F32),
                        pltpu.VMEM((2, m, KV_RANK), F32),
                        pltpu.VMEM((m, 2 * CHUNK), BF16)],
        compiler_params=_cparams("arbitrary"),
        name="a_attn",
    )(q, ckv4, ckvt, mask, w_uk, w_uv, bias_nd)


def _bias_kernel(rb_ref, ids_ref, o_ref, *, n_tiles, far_bucket, shift_far):
    h = pl.program_id(0)
    far = rb_ref[far_bucket, h] if shift_far else 0.0
    for t in range(n_tiles):
        ids = ids_ref[t]
        out = jnp.zeros(ids.shape, F32)
        for b in range(N_BUCKETS):
            out = jnp.where(ids == b, rb_ref[b, h] - far, out)
        o_ref[0, t] = jnp.where(ids < 0, NEG, out * LOG2E)


def _t5_bucket(dist):
    max_exact = N_BUCKETS // 2
    d = jnp.maximum(dist, 0)
    large = max_exact + (jnp.log(jnp.maximum(d, 1).astype(F32) / max_exact)
                         / math.log(MAX_DISTANCE / max_exact)
                         * (N_BUCKETS - max_exact)).astype(I32)
    large = jnp.minimum(large, N_BUCKETS - 1)
    return jnp.where(d < max_exact, d, large)


def _bias_tiles(rel_bias, ids, shift_far):
    n_tiles, r, c = ids.shape
    return pl.pallas_call(
        functools.partial(_bias_kernel, n_tiles=n_tiles, far_bucket=N_BUCKETS - 1,
                          shift_far=shift_far),
        grid=(N_HEADS,),
        in_specs=[pl.BlockSpec(memory_space=pltpu.SMEM),
                  pl.BlockSpec((n_tiles, r, c), lambda h: (0, 0, 0))],
        out_specs=pl.BlockSpec((1, n_tiles, r, c), lambda h: (h, 0, 0, 0)),
        out_shape=jax.ShapeDtypeStruct((N_HEADS, n_tiles, r, c), F32),
        compiler_params=_cparams("parallel"),
        name="bias_tiles",
    )(rel_bias, ids)


def _b_inproj_kernel(x_ref, g_ref, sh_ref, sc_ref, w_ref, b_ref, q_ref, k_ref, v_ref):
    h = _normmod(x_ref[0], g_ref[...], sh_ref[0], sc_ref[0]).astype(BF16)
    proj = jnp.dot(h, w_ref[...], preferred_element_type=F32) + b_ref[...]
    q_ref[0] = (proj[:, :HD] * (HEAD_DIM ** -0.5 * LOG2E)).astype(BF16)
    k_ref[0] = proj[:, HD:HD + KV_HEADS * HEAD_DIM].astype(BF16)
    v_ref[0] = proj[:, HD + KV_HEADS * HEAD_DIM:].astype(BF16)


def _b_inproj(x, g, sh, sc, w, layer, b, tm=512):
    bsz, L, d = x.shape
    kvw = KV_HEADS * HEAD_DIM
    row = lambda b_, i: (b_, i, 0)
    per_b = lambda b_, i: (b_, 0, 0)
    const2 = lambda b_, i: (0, 0)
    return pl.pallas_call(
        _b_inproj_kernel,
        grid=(bsz, L // tm),
        in_specs=[pl.BlockSpec((1, tm, d), row),
                  pl.BlockSpec((1, d), const2),
                  pl.BlockSpec((1, 1, d), per_b),
                  pl.BlockSpec((1, 1, d), per_b),
                  _layer_spec(w, layer),
                  pl.BlockSpec((1, B_IN), const2)],
        out_specs=[pl.BlockSpec((1, tm, HD), row),
                   pl.BlockSpec((1, tm, kvw), row),
                   pl.BlockSpec((1, tm, kvw), row)],
        out_shape=[jax.ShapeDtypeStruct((bsz, L, HD), BF16),
                   jax.ShapeDtypeStruct((bsz, L, kvw), BF16),
                   jax.ShapeDtypeStruct((bsz, L, kvw), BF16)],
        compiler_params=_cparams("parallel", "parallel"),
        name="b_inproj",
    )(x, g, sh, sc, w, b)


def _b_window_attention(sink_ref, q_block, kp, kc, vp, vc, bias_ref, first):
    w = WINDOW
    pairs = N_HEADS // KV_HEADS // 2
    kall = jnp.concatenate([kp, kc], axis=0).astype(F32)
    vall = jnp.concatenate([vp, vc], axis=0).astype(F32)
    low = lax.broadcasted_iota(I32, kall.shape, 1) < HEAD_DIM

    def padded(x, g):
        swapped = pltpu.roll(x, HEAD_DIM, axis=1)
        on_low, on_high = (x, swapped) if g == 0 else (swapped, x)
        return (jnp.where(low, on_low, 0.0).astype(BF16), jnp.where(low, 0.0, on_high).astype(BF16))

    kpad = [padded(kall, g) for g in range(KV_HEADS)]
    vpad = [padded(vall, g) for g in range(KV_HEADS)]

    def attend(blk, g):
        variant = jnp.where(first, 1, 0) if blk == 0 else 0
        keys = slice(blk * w, (blk + 2) * w)
        q = q_block(blk)
        blocks = [q[:, (pairs * g + p) * 2 * HEAD_DIM:(pairs * g + p + 1) * 2 * HEAD_DIM]
                  for p in range(pairs)]
        lg = lax.dot_general(jnp.concatenate(blocks, axis=0),
                             jnp.concatenate([kpad[g][0][keys], kpad[g][1][keys]], axis=0),
                             _NT, preferred_element_type=F32)
        lg = lg + bias_ref[variant, g]
        probs = [[], []]
        for p in range(pairs):
            for par in range(2):
                h = 2 * pairs * g + 2 * p + par
                t = lg[p * w:(p + 1) * w, par * 2 * w:(par + 1) * 2 * w]
                sink = sink_ref[0, h] * LOG2E
                m = jnp.maximum(jnp.max(t, axis=-1, keepdims=True), sink)
                e = jnp.exp2(t - m)
                denom = jnp.sum(e, axis=-1, keepdims=True) + jnp.exp2(sink - m)
                probs[par].append((e * (1.0 / denom)).astype(BF16))
        og = (jnp.dot(jnp.concatenate(probs[0], axis=0), vpad[g][0][keys],
                      preferred_element_type=F32)
              + jnp.dot(jnp.concatenate(probs[1], axis=0), vpad[g][1][keys],
                        preferred_element_type=F32))
        return jnp.concatenate([og[p * w:(p + 1) * w] for p in range(pairs)], axis=1).astype(BF16)

    return attend


FF_CHUNK = 256


def _out_ffn_tile(o, wo_ref, bo_ref, g1_ref, x_ref, g_ref, sh_ref, sc_ref, gate_ref,
                  w1_ref, w3_ref, w2_ref, fg_ref, y_ref, final_norm):
    mix = jnp.dot(o, wo_ref[...], preferred_element_type=F32) + bo_ref[...]
    x = x_ref[0] + g1_ref[0] * mix
    h = _normmod(x, g_ref[...], sh_ref[0], sc_ref[0]).astype(BF16)
    acc = jnp.zeros(x.shape, F32)
    for c in range(D_FF // FF_CHUNK):
        cs = slice(c * FF_CHUNK, (c + 1) * FF_CHUNK)
        a1 = jnp.dot(h, w1_ref[:, cs], preferred_element_type=F32)
        a3 = jnp.dot(h, w3_ref[:, cs], preferred_element_type=F32)
        act = (a1 * jax.nn.sigmoid(a1)) * a3
        acc = acc + jnp.dot(act.astype(BF16), w2_ref[cs, :], preferred_element_type=F32)
    y = x + gate_ref[0] * acc
    if final_norm:
        y = (y * lax.rsqrt(jnp.mean(y * y, axis=-1, keepdims=True) + RMS_EPS)) * fg_ref[...]
    y_ref[0] = y


def _out_ffn_kernel(o_ref, *refs, final_norm):
    _out_ffn_tile(o_ref[0], *refs, final_norm)


def _b_attn_ffn_kernel(sink_ref, q_ref, kp_ref, kc_ref, vp_ref, vc_ref, bias_ref, *refs,
                       final_norm, n_blocks, tiles_per_seq, n_work):
    ffn_refs, o_scr = refs[:-1], refs[-1]
    s = pl.program_id(0)
    slot = s & 1
    w = WINDOW

    def placeholder(r, carry):
        o_scr[1, pl.ds(pl.multiple_of(r * w, w), w), :] = jnp.zeros((w, HD), BF16)
        return carry

    lax.fori_loop(0, jnp.where(s == 0, n_blocks, 0), placeholder, 0)

    first = lax.rem(jnp.minimum(s, n_work - 1), tiles_per_seq) == 0
    attend = _b_window_attention(sink_ref, lambda blk: q_ref[0, blk * w:(blk + 1) * w, :],
                                 kp_ref[0], kc_ref[0], vp_ref[0], vc_ref[0], bias_ref, first)

    _out_ffn_tile(o_scr[1 - slot], *ffn_refs, final_norm)
    gw = HD // KV_HEADS
    for blk in range(n_blocks):
        for g in range(KV_HEADS):
            o_scr[slot, blk * w:(blk + 1) * w, g * gw:(g + 1) * gw] = attend(blk, g)


def _b_attn_ffn(q, k, v, sinks, bias_b, w_out, mix_layer, b_out, g1, x, g, sh, sc, gate, w1, w3,
                w2, layer, final_g, final_norm, tm=512):
    bsz, L, d = x.shape
    w = WINDOW
    kvw = KV_HEADS * HEAD_DIM
    tiles = L // tm
    n_work = bsz * tiles

    def tile(s):
        return s // tiles, s % tiles

    cur = lambda s: tile(jnp.minimum(s, n_work - 1))
    prev = lambda s: tile(jnp.maximum(s - 1, 0))
    cur_row = lambda s: cur(s) + (0,)
    before = lambda s: (cur(s)[0], jnp.maximum(cur(s)[1] * (tm // w) - 1, 0), 0)
    prev_row = lambda s: prev(s) + (0,)
    prev_b = lambda s: (prev(s)[0], 0, 0)
    const2 = lambda s: (0, 0)
    return pl.pallas_call(
        functools.partial(_b_attn_ffn_kernel, final_norm=final_norm, n_blocks=tm // w,
                          tiles_per_seq=tiles, n_work=n_work),
        grid=(n_work + 1,),
        in_specs=[pl.BlockSpec(memory_space=pltpu.SMEM),
                  pl.BlockSpec((1, tm, HD), cur_row),
                  pl.BlockSpec((1, w, kvw), before),
                  pl.BlockSpec((1, tm, kvw), cur_row),
                  pl.BlockSpec((1, w, kvw), before),
                  pl.BlockSpec((1, tm, kvw), cur_row),
                  pl.BlockSpec(bias_b.shape, lambda s: (0, 0, 0, 0)),
                  _layer_spec(w_out, mix_layer),
                  pl.BlockSpec((1, d), const2),
                  pl.BlockSpec((1, 1, d), prev_b),
                  pl.BlockSpec((1, tm, d), prev_row),
                  pl.BlockSpec((1, d), const2),
                  pl.BlockSpec((1, 1, d), prev_b),
                  pl.BlockSpec((1, 1, d), prev_b),
                  pl.BlockSpec((1, 1, d), prev_b),
                  _layer_spec(w1, layer),
                  _layer_spec(w3, layer),
                  _layer_spec(w2, layer),
                  pl.BlockSpec((1, d), const2)],
        out_specs=pl.BlockSpec((1, tm, d), prev_row),
        out_shape=jax.ShapeDtypeStruct((bsz, L, d), F32),
        scratch_shapes=[pltpu.VMEM((2, tm, HD), BF16)],
        compiler_params=_cparams("arbitrary"),
        name="b_attn_ffn",
    )(sinks.reshape(1, N_HEADS), q, k, k, v, v, bias_b, w_out, b_out, g1, x, g, sh, sc, gate,
      w1, w3, w2, final_g)


def _out_ffn(o, w_out, mix_layer, b_out, g1, x, g, sh, sc, gate, w1, w3, w2, layer, final_g,
             final_norm, tm=512):
    bsz, L, d = x.shape
    row = lambda b_, i: (b_, i, 0)
    per_b = lambda b_, i: (b_, 0, 0)
    const2 = lambda b_, i: (0, 0)
    return pl.pallas_call(
        functools.partial(_out_ffn_kernel, final_norm=final_norm),
        grid=(bsz, L // tm),
        in_specs=[pl.BlockSpec((1, tm, HD), row),
                  _layer_spec(w_out, mix_layer),
                  pl.BlockSpec((1, d), const2),
                  pl.BlockSpec((1, 1, d), per_b),
                  pl.BlockSpec((1, tm, d), row),
                  pl.BlockSpec((1, d), const2),
                  pl.BlockSpec((1, 1, d), per_b),
                  pl.BlockSpec((1, 1, d), per_b),
                  pl.BlockSpec((1, 1, d), per_b),
                  _layer_spec(w1, layer),
                  _layer_spec(w3, layer),
                  _layer_spec(w2, layer),
                  pl.BlockSpec((1, d), const2)],
        out_specs=pl.BlockSpec((1, tm, d), row),
        out_shape=jax.ShapeDtypeStruct((bsz, L, d), F32),
        compiler_params=_cparams("parallel", "parallel"),
        name="out_ffn",
    )(o, w_out, b_out, g1, x, g, sh, sc, gate, w1, w3, w2, final_g)


def kernel(x, c, rel_bias, w_ada, b_ada, norm_mix_g, norm_ffn_g, a_w_in, a_kv_norm_g, a_w_uk,
           a_w_uv, a_idx_k_g, a_idx_k_b, a_w_out, b_w_in, b_b_in, b_sinks, b_w_out, b_b_out,
           ffn_w1, ffn_w3, ffn_w2, norm_final_g):
    bsz, L, d = x.shape
    depth = w_ada.shape[0]
    topk = min(INDEX_TOPK, L // 4)

    mod = _adaln(c, w_ada, b_ada)

    r = jnp.arange(BIAS_BLK)
    ids_a = jnp.stack([_t5_bucket(r[:, None] - r[None, :]),
                       _t5_bucket(r[:, None] - r[None, :] + BIAS_BLK)])
    bias_a = _bias_tiles(rel_bias, ids_a, shift_far=True)
    dist_b = jnp.arange(WINDOW)[:, None] + WINDOW - jnp.arange(2 * WINDOW)[None, :]
    in_window = (dist_b >= 0) & (dist_b < WINDOW)
    ids_b = jnp.where(in_window, _t5_bucket(dist_b), -1)
    ids_b0 = jnp.where(jnp.arange(2 * WINDOW)[None, :] >= WINDOW, ids_b, -1)
    bias_b = _bias_tiles(rel_bias, jnp.stack([ids_b, ids_b0]), shift_far=False)
    pairs = N_HEADS // KV_HEADS // 2
    bias_b = bias_b.reshape(KV_HEADS, pairs, 2, 2, WINDOW, 2 * WINDOW).transpose(3, 0, 1, 4, 2, 5)
    bias_b = bias_b.reshape(2, KV_HEADS, pairs * WINDOW, 4 * WINDOW)

    a_w_main = jnp.pad(a_w_in[:, :, :A_MAIN], ((0, 0), (0, 0), (0, A_MAIN_PAD - A_MAIN))).astype(BF16)
    a_w_wi_t = jnp.pad(a_w_in[:, :, A_WI0:A_WI0 + IDX_HEADS].transpose(0, 2, 1),
                       ((0, 0), (0, 16 - IDX_HEADS), (0, 0))).astype(BF16)
    a_w_uk, a_w_uv, a_w_out, b_w_in, b_w_out, ffn_w1, ffn_w3, ffn_w2 = [
        w.astype(BF16) for w in (jnp.swapaxes(_pair_blocks(a_w_uk), -1, -2),
                                 _pair_blocks(a_w_uv), a_w_out, b_w_in,
                                 b_w_out, ffn_w1, ffn_w3, ffn_w2)]

    zero_bias = jnp.zeros((1, d), F32)
    for i in range(depth):
        sh1, sc1, g1, sh2, sc2, g2 = [m.reshape(bsz, 1, d) for m in jnp.split(mod[i], 6, axis=-1)]
        jm = i // 2
        ffn_args = (g1, x, norm_ffn_g[i][None], sh2, sc2, g2, ffn_w1, ffn_w3, ffn_w2, i,
                    norm_final_g[None], i == depth - 1)
        if i % 2 == 0:
            q, ckv, ckvt, qi, ki, wit = _a_inproj(
                x, norm_mix_g[i][None], sh1, sc1, a_w_main, a_w_wi_t, jm, a_kv_norm_g[jm][None],
                a_idx_k_g[jm][None], a_idx_k_b[jm][None])
            mask = _a_index(qi, wit, ki, topk)
            o = _a_attn(q, ckv, ckvt, mask, a_w_uk, a_w_uv, jm, bias_a)
            x = _out_ffn(o, a_w_out, jm, zero_bias, *ffn_args)
        else:
            q, k, v = _b_inproj(x, norm_mix_g[i][None], sh1, sc1, b_w_in, jm, b_b_in[jm][None])
            x = _b_attn_ffn(q, k, v, b_sinks[jm], bias_b, b_w_out, jm, b_b_out[jm][None], *ffn_args)
    return x
```

```python
import functools
import math

import numpy as np
import jax
import jax.numpy as jnp
from jax import lax
from jax.experimental import pallas as pl
from jax.experimental.pallas import tpu as pltpu

N_HEADS = 16
HEAD_DIM = 64
KV_RANK = 256
IDX_HEADS = 8
IDX_DIM = 64
INDEX_TOPK = 256
KV_HEADS = 2
WINDOW = 128
N_BUCKETS = 32
MAX_DISTANCE = 128
D_FF = 2816
RMS_EPS = 1e-6
NEG = -1e30

HD = N_HEADS * HEAD_DIM
A_Q0, A_KV0, A_QI0, A_KI0, A_WI0 = 0, HD, HD + KV_RANK, HD + KV_RANK + IDX_HEADS * IDX_DIM, \
    HD + KV_RANK + IDX_HEADS * IDX_DIM + IDX_DIM
LANES = 128
HALF_ROWS = 16
A_MAIN = A_WI0
A_MAIN_PAD = -(-A_MAIN // LANES) * LANES
B_IN = (N_HEADS + 2 * KV_HEADS) * HEAD_DIM

CHUNK = 256
PV_ROWS = 64
ACC_ROWS = 256
BIAS_BLK = 128
VMEM_LIMIT = 56 * 1024 * 1024

F32 = jnp.float32
BF16 = jnp.bfloat16
I32 = jnp.int32
I16 = jnp.int16

_NT = (((1,), (1,)), ((), ()))


def _cparams(*sem):
    return pltpu.CompilerParams(dimension_semantics=sem, vmem_limit_bytes=VMEM_LIMIT)


def _pair_blocks(w):
    even, odd = w[..., 0::2, :, :], w[..., 1::2, :, :]
    zero = jnp.zeros_like(even)
    return jnp.concatenate([jnp.concatenate([even, zero], axis=-1),
                            jnp.concatenate([zero, odd], axis=-1)], axis=-2)


def _layer_spec(stack, layer):
    zeros = (0,) * (stack.ndim - 1)
    return pl.BlockSpec((None,) + stack.shape[1:], lambda *_: (layer,) + zeros)


def _f32_key(v):
    b = int(np.array(v, np.float32).view(np.int32))
    return b ^ ((b >> 31) & 0x7FFFFFFF)


LOG2E = math.log2(math.e)
KEY_NEG = _f32_key(NEG)
INT_MIN = -(2 ** 31)


def _adaln_kernel(c_ref, w_ref, b_ref, o_ref):
    c = c_ref[...]
    cs = c * jax.nn.sigmoid(c)
    o_ref[0] = jnp.dot(cs, w_ref[0], preferred_element_type=F32,
                       precision=lax.Precision.HIGHEST) + b_ref[0]


def _adaln(c, w_ada, b_ada):
    depth, d, n = w_ada.shape
    bsz = c.shape[0]
    tn = 1536
    return pl.pallas_call(
        _adaln_kernel,
        grid=(depth, n // tn),
        in_specs=[pl.BlockSpec((bsz, d), lambda i, j: (0, 0)),
                  pl.BlockSpec((1, d, tn), lambda i, j: (i, 0, j)),
                  pl.BlockSpec((1, 1, tn), lambda i, j: (i, 0, j))],
        out_specs=pl.BlockSpec((1, bsz, tn), lambda i, j: (i, 0, j)),
        out_shape=jax.ShapeDtypeStruct((depth, bsz, n), F32),
        compiler_params=_cparams("parallel", "parallel"),
        name="adaln",
    )(c, w_ada, b_ada.reshape(depth, 1, n))


def _normmod(x, g, sh, sc):
    ms = jnp.mean(x * x, axis=-1, keepdims=True)
    y = (x * lax.rsqrt(ms + RMS_EPS)) * g
    return y * (1.0 + sc) + sh


def _a_inproj_kernel(x_ref, g_ref, sh_ref, sc_ref, w_ref, wwi_ref, kvg_ref, ikg_ref, ikb_ref,
                     q_ref, ckv_ref, ckvt_ref, qi_ref, ki_ref, wit_ref):
    h = _normmod(x_ref[0], g_ref[...], sh_ref[0], sc_ref[0]).astype(BF16)
    proj = jnp.dot(h, w_ref[...], preferred_element_type=F32)
    q_ref[0] = (proj[:, A_Q0:A_KV0] * (HEAD_DIM ** -0.5)).astype(BF16)
    ckv = proj[:, A_KV0:A_QI0]
    ckv = (ckv * lax.rsqrt(jnp.mean(ckv * ckv, axis=-1, keepdims=True) + RMS_EPS)) * kvg_ref[...]
    ckv_ref[0] = ckv.astype(BF16)
    for c in range(ckvt_ref.shape[1]):
        ckvt_ref[0, c] = ckv[c * CHUNK:(c + 1) * CHUNK].T.astype(BF16)
    qi_ref[0] = proj[:, A_QI0:A_KI0].astype(BF16)
    ki = proj[:, A_KI0:A_WI0]
    mu = jnp.mean(ki, axis=-1, keepdims=True)
    var = jnp.mean(jnp.square(ki - mu), axis=-1, keepdims=True)
    ki = ((ki - mu) * lax.rsqrt(var + RMS_EPS)) * ikg_ref[...] + ikb_ref[...]
    ki_ref[0] = ki.astype(BF16)
    wit = lax.dot_general(wwi_ref[...], h, _NT, preferred_element_type=F32)
    wit_ref[0] = wit[:IDX_HEADS] * (IDX_HEADS ** -0.5 * IDX_DIM ** -0.5)


def _a_inproj(x, g, sh, sc, w_main, w_wi_t, layer, kv_g, ik_g, ik_b, tm=512):
    bsz, L, d = x.shape
    row = lambda b, i: (b, i, 0)
    per_b = lambda b, i: (b, 0, 0)
    const2 = lambda b, i: (0, 0)
    return pl.pallas_call(
        _a_inproj_kernel,
        grid=(bsz, L // tm),
        in_specs=[pl.BlockSpec((1, tm, d), row),
                  pl.BlockSpec((1, d), const2),
                  pl.BlockSpec((1, 1, d), per_b),
                  pl.BlockSpec((1, 1, d), per_b),
                  _layer_spec(w_main, layer),
                  _layer_spec(w_wi_t, layer),
                  pl.BlockSpec((1, KV_RANK), const2),
                  pl.BlockSpec((1, IDX_DIM), const2),
                  pl.BlockSpec((1, IDX_DIM), const2)],
        out_specs=[pl.BlockSpec((1, tm, HD), row),
                   pl.BlockSpec((1, tm, KV_RANK), row),
                   pl.BlockSpec((1, tm // CHUNK, KV_RANK, CHUNK), lambda b, i: (b, i, 0, 0)),
                   pl.BlockSpec((1, tm, IDX_HEADS * IDX_DIM), row),
                   pl.BlockSpec((1, tm, IDX_DIM), row),
                   pl.BlockSpec((1, IDX_HEADS, tm), lambda b, i: (b, 0, i))],
        out_shape=[jax.ShapeDtypeStruct((bsz, L, HD), BF16),
                   jax.ShapeDtypeStruct((bsz, L, KV_RANK), BF16),
                   jax.ShapeDtypeStruct((bsz, L // CHUNK, KV_RANK, CHUNK), BF16),
                   jax.ShapeDtypeStruct((bsz, L, IDX_HEADS * IDX_DIM), BF16),
                   jax.ShapeDtypeStruct((bsz, L, IDX_DIM), BF16),
                   jax.ShapeDtypeStruct((bsz, IDX_HEADS, L), F32)],
        compiler_params=_cparams("parallel", "parallel"),
        name="a_inproj",
    )(x, g, sh, sc, w_main, w_wi_t, kv_g, ik_g, ik_b)


def _a_index_kernel(qi_ref, wit_ref, ki_ref, tri_ref, mask_ref, key_scr, half_scr, *, topk,
                    seq_len):
    j = pl.program_id(1)
    nchunk = seq_len // CHUNK
    qi = qi_ref[0]
    wit = wit_ref[0]
    t_glob = j * CHUNK + lax.broadcasted_iota(I32, (CHUNK, CHUNK), 1)
    s_loc = lax.broadcasted_iota(I32, (CHUNK, CHUNK), 0)

    def chunks(fn, init):
        def pair(i, carry):
            return fn(2 * i + 1, fn(2 * i, carry))
        carry = lax.fori_loop(0, lax.shift_right_logical(j + 1, 1), pair, init)
        return lax.cond((j & 1) == 0, lambda c: fn(j, c), lambda c: c, carry)

    def score_chunk(kc, carry):
        kik = ki_ref[0, pl.ds(pl.multiple_of(kc * CHUNK, CHUNK), CHUNK), :]
        acc = jnp.zeros((CHUNK, CHUNK), F32)
        for h in range(IDX_HEADS):
            r = lax.dot_general(kik, qi[:, h * IDX_DIM:(h + 1) * IDX_DIM], _NT,
                                preferred_element_type=F32)
            acc = acc + jnp.maximum(r, 0.0) * wit[h:h + 1, :]
        sc = jnp.where(kc * CHUNK + s_loc <= t_glob, acc, NEG)
        bits = lax.bitcast_convert_type(sc, I32)
        key = bits ^ ((bits >> 31) & 0x7FFFFFFF)
        key_scr[kc] = key
        half_scr[kc] = (key >> 16).astype(I16)
        return carry

    chunks(score_chunk, 0)

    n_beyond = seq_len - (j + 1) * CHUNK
    ge = lambda a, b: a >= b
    gt = lambda a, b: a > b

    def count(pred, thr):
        def body(kc, acc):
            m = jnp.where(pred(key_scr[kc], thr), jnp.int32(1), jnp.int32(0))
            return acc + m.reshape(CHUNK // 8, 8, CHUNK).sum(axis=0)
        acc = chunks(body, jnp.zeros((8, CHUNK), I32))
        cnt = acc.sum(axis=0, keepdims=True)
        return cnt + jnp.where(pred(KEY_NEG, thr), n_beyond, 0)

    def count_half(pred, thr16):
        t = jnp.broadcast_to(thr16, (HALF_ROWS, CHUNK)).astype(I16)
        def body(kc, acc):
            m = jnp.where(pred(half_scr[kc].reshape(CHUNK // HALF_ROWS, HALF_ROWS, CHUNK), t[None]),
                          jnp.int16(1), jnp.int16(0))
            for r in range(CHUNK // HALF_ROWS):
                acc = acc + m[r]
            return acc
        acc = chunks(body, jnp.zeros((HALF_ROWS, CHUNK), I16))
        return acc.astype(I32).sum(axis=0, keepdims=True)

    def beyond(cand):
        return jnp.where(KEY_NEG >= cand, n_beyond, 0)

    lowest = jnp.full((1, CHUNK), INT_MIN, I32), jnp.full((1, CHUNK), seq_len, I32)

    def step(state, cand, cnt):
        keep = cnt >= topk
        return jnp.where(keep, cand, state[0]), jnp.where(keep, cnt, state[1])

    def bisect():
        zero = jnp.zeros((1, CHUNK), I32)
        state = step(lowest, zero, count_half(ge, zero) + beyond(zero))

        def bisect_high(i, state):
            cand = state[0] + lax.shift_left(jnp.int32(1), 30 - i)
            return step(state, cand, count_half(ge, cand >> 16) + beyond(cand))

        state = lax.fori_loop(0, 15, bisect_high, state)
        thr = state[0]

        thr_hi = thr >> 16
        n_above = count_half(gt, thr_hi)

        def low_half(kc, carry):
            key = key_scr[kc]
            low = (key & 0xFFFF) - 0x8000
            half_scr[kc] = jnp.where((key >> 16) == thr_hi, low, -0x8000).astype(I16)
            return carry

        chunks(low_half, 0)

        def bisect_low(i, state):
            cand = state[0] + lax.shift_left(jnp.int32(1), 15 - i)
            cnt = n_above + count_half(ge, (cand & 0xFFFF) - 0x8000) + beyond(cand)
            return step(state, cand, cnt)

        return lax.fori_loop(0, 16, bisect_low, state)

    takes_all = (j + 1) * CHUNK <= topk
    thr, n_at_thr = lax.cond(takes_all, lambda: lowest, bisect)

    def emit_ranked():
        need = (topk - count(gt, thr)).astype(F32)
        tri = tri_ref[...]

        def emit(kc, seen):
            key = key_scr[kc]
            eq = jnp.where(key == thr, 1.0, 0.0)
            rank = jnp.dot(tri, eq.astype(BF16), preferred_element_type=F32) + seen
            take = jnp.where(key > thr, 1.0, jnp.where(rank < need, eq, 0.0))
            causal = kc * CHUNK + s_loc <= t_glob
            m = jnp.where(causal, jnp.where(take > 0.5, 0.0, NEG), NEG)
            mask_ref[0, kc] = m.T
            return seen + jnp.sum(eq, axis=0, keepdims=True)

        chunks(emit, jnp.zeros((1, CHUNK), F32))

    def emit_plain():
        def emit(kc, carry):
            causal = kc * CHUNK + s_loc <= t_glob
            m = jnp.where(causal, jnp.where(key_scr[kc] >= thr, 0.0, NEG), NEG)
            mask_ref[0, kc] = m.T
            return carry

        chunks(emit, 0)

    exact_fit = takes_all | (jnp.max(n_at_thr) <= topk)
    lax.cond(exact_fit, emit_plain, emit_ranked)

    def fill(kc, carry):
        mask_ref[0, kc] = jnp.full((CHUNK, CHUNK), NEG, F32)
        return carry

    lax.fori_loop(j + 1, nchunk, fill, 0)


def _a_index(qi, wit, ki, topk):
    bsz, L, _ = qi.shape
    nchunk = L // CHUNK
    tri = jnp.tril(jnp.ones((CHUNK, CHUNK), BF16), -1)
    return pl.pallas_call(
        functools.partial(_a_index_kernel, topk=topk, seq_len=L),
        grid=(bsz, nchunk),
        in_specs=[pl.BlockSpec((1, CHUNK, IDX_HEADS * IDX_DIM), lambda b, j: (b, j, 0)),
                  pl.BlockSpec((1, IDX_HEADS, CHUNK), lambda b, j: (b, 0, j)),
                  pl.BlockSpec((1, L, IDX_DIM), lambda b, j: (b, 0, 0)),
                  pl.BlockSpec((CHUNK, CHUNK), lambda b, j: (0, 0))],
        out_specs=pl.BlockSpec((1, nchunk, CHUNK, CHUNK), lambda b, j: (b, 0, j, 0)),
        out_shape=jax.ShapeDtypeStruct((bsz, nchunk, L, CHUNK), F32),
        scratch_shapes=[pltpu.VMEM((nchunk, CHUNK, CHUNK), I32),
                        pltpu.VMEM((nchunk, CHUNK, CHUNK), I16)],
        compiler_params=_cparams("parallel", "parallel"),
        name="a_index",
    )(qi, wit, ki, tri)


def _a_attn_kernel(q_ref, ckv_ref, ckvt_ref, mask_ref, wuk_ref, wuv_ref, bias_ref, o_ref,
                   qabs_scr, lg_scr, m_scr, l_scr, acc_scr, p_scr, *, hc, nq, n_work):
    s = pl.program_id(0)
    j = (jnp.minimum(s, n_work - 1) // (N_HEADS // hc)) % nq
    slot = s & 1
    tq = CHUNK

    def placeholder(r, carry):
        rows = pl.ds(pl.multiple_of(r * ACC_ROWS, ACC_ROWS), ACC_ROWS)
        l_scr[1, rows, :] = jnp.ones((ACC_ROWS, LANES), F32)
        acc_scr[1, rows, :] = jnp.zeros((ACC_ROWS, KV_RANK), F32)
        return carry

    lax.fori_loop(0, jnp.where(s == 0, hc * tq // ACC_ROWS, 0), placeholder, 0)

    q = q_ref[0]
    for i in range(hc // 2):
        qa = jnp.dot(q[:, i * 2 * HEAD_DIM:(i + 1) * 2 * HEAD_DIM], wuk_ref[i],
                     preferred_element_type=F32)
        for par in range(2):
            qabs_scr[(2 * i + par) * tq:(2 * i + par + 1) * tq, :] = (
                qa[:, par * KV_RANK:(par + 1) * KV_RANK] * LOG2E).astype(BF16)
    o = (acc_scr[1 - slot] / jnp.sum(l_scr[1 - slot], axis=-1, keepdims=True)).astype(BF16)
    outs = [jnp.dot(jnp.concatenate([o[2 * i * tq:(2 * i + 1) * tq],
                                     o[(2 * i + 1) * tq:(2 * i + 2) * tq]], axis=1),
                    wuv_ref[i], preferred_element_type=F32) for i in range(hc // 2)]
    o_ref[0] = jnp.concatenate(outs, axis=1).astype(BF16)
    qg = qabs_scr[...]

    n_far_pairs = lax.shift_right_arithmetic(j - 1, 1)
    near_is_pair = (j & 1) == 1
    near_is_triple = ((j & 1) == 0) & (j >= 2)
    far_top = j - 4 + (j & 1)

    def keys(kc, width):
        return ckv_ref[0, pl.ds(kc, width)].reshape(width * CHUNK, KV_RANK)

    def logits(kc, width):
        keys_t = jnp.concatenate([ckvt_ref[0, kc + w] for w in range(width)], axis=1)
        lg = jnp.dot(qg, keys_t, preferred_element_type=F32)
        mk = jnp.concatenate([mask_ref[0, kc + w] for w in range(width)], axis=1)
        return lg + jnp.concatenate([mk] * hc, axis=0)

    def near_bias(i, width):
        d0, d1 = bias_ref[i, 0], bias_ref[i, 1]
        z = jnp.zeros_like(d0)
        top, bot = [d0, z], [d1, d0]
        if width >= 2:
            top, bot = [z, d1] + top, [z, z] + bot
        if width == 3:
            top, bot = [z, z] + top, [z, z] + bot
        return jnp.concatenate([jnp.concatenate(top, axis=1), jnp.concatenate(bot, axis=1)], axis=0)

    def near_logits(kc, width):
        bias = jnp.concatenate([near_bias(i, width) for i in range(hc)], axis=0)
        return logits(kc, width) + bias

    def put_logits(kc, width, lg):
        for w in range(width):
            lg_scr[kc + w] = lg[:, w * CHUNK:(w + 1) * CHUNK]
        m = m_scr[...]
        for c in range(width * CHUNK // LANES):
            m = jnp.maximum(m, lg[:, c * LANES:(c + 1) * LANES])
        m_scr[...] = m

    m_scr[...] = jnp.full(m_scr.shape, -jnp.inf, F32)

    @pl.when(near_is_pair)
    def _():
        put_logits(j - 1, 2, near_logits(j - 1, 2))

    @pl.when(near_is_triple)
    def _():
        put_logits(j - 2, 3, near_logits(j - 2, 3))

    @pl.when(j == 0)
    def _():
        put_logits(0, 1, near_logits(0, 1))

    def far_pair(i, carry):
        kc = far_top - 2 * i
        put_logits(kc, 2, logits(kc, 2))
        return carry

    lax.fori_loop(0, n_far_pairs, far_pair, 0)

    def pv(kc, width, first):
        ck = keys(kc, width)
        for piece in range(hc * tq // ACC_ROWS):
            for r in range(piece * ACC_ROWS // PV_ROWS, (piece + 1) * ACC_ROWS // PV_ROWS):
                rows = slice(r * PV_ROWS, (r + 1) * PV_ROWS)
                m = m_scr[rows, :]
                if first:
                    m = jnp.broadcast_to(jnp.max(m, axis=-1, keepdims=True), m.shape)
                    m_scr[rows, :] = m
                l = jnp.zeros((PV_ROWS, LANES), F32) if first else l_scr[slot, rows, :]
                for w in range(width):
                    for c in range(CHUNK // LANES):
                        col = w * CHUNK + c * LANES
                        p = jnp.exp2(lg_scr[kc + w, rows, c * LANES:(c + 1) * LANES] - m)
                        l = l + p
                        p_scr[rows, col:col + LANES] = p.astype(BF16)
                l_scr[slot, rows, :] = l
            rows = slice(piece * ACC_ROWS, (piece + 1) * ACC_ROWS)
            part = jnp.dot(p_scr[rows, :width * CHUNK], ck, preferred_element_type=F32)
            acc_scr[slot, rows, :] = part if first else acc_scr[slot, rows, :] + part

    @pl.when(near_is_pair)
    def _():
        pv(j - 1, 2, True)

    @pl.when(near_is_triple)
    def _():
        pv(j - 2, 3, True)

    @pl.when(j == 0)
    def _():
        pv(0, 1, True)

    def far_pv(i, carry):
        pv(far_top - 2 * i, 2, False)
        return carry

    lax.fori_loop(0, n_far_pairs, far_pv, 0)


def _a_attn(q, ckv, ckvt, mask, w_uk, w_uv, layer, bias_nd, hc=8):
    bsz, L, _ = q.shape
    nchunk = L // CHUNK
    hg = N_HEADS // hc
    m = hc * CHUNK
    ckv4 = ckv.reshape(bsz, nchunk, CHUNK, KV_RANK)
    n_work = bsz * nchunk * hg

    def item(s):
        return s // (nchunk * hg), (s // hg) % nchunk, s % hg

    cur = lambda s: item(jnp.minimum(s, n_work - 1))
    prev = lambda s: item(jnp.maximum(s - 1, 0))
    return pl.pallas_call(
        functools.partial(_a_attn_kernel, hc=hc, nq=nchunk, n_work=n_work),
        grid=(n_work + 1,),
        in_specs=[pl.BlockSpec((1, CHUNK, hc * HEAD_DIM), lambda s: cur(s)),
                  pl.BlockSpec((1, nchunk, CHUNK, KV_RANK), lambda s: (cur(s)[0], 0, 0, 0)),
                  pl.BlockSpec((1, nchunk, KV_RANK, CHUNK), lambda s: (cur(s)[0], 0, 0, 0)),
                  pl.BlockSpec((1, nchunk, CHUNK, CHUNK), lambda s: (cur(s)[0], 0, cur(s)[1], 0)),
                  pl.BlockSpec((None, hc // 2) + w_uk.shape[2:], lambda s: (layer, cur(s)[2], 0, 0)),
                  pl.BlockSpec((None, hc // 2) + w_uv.shape[2:], lambda s: (layer, prev(s)[2], 0, 0)),
                  pl.BlockSpec((hc, 2, BIAS_BLK, BIAS_BLK), lambda s: (cur(s)[2], 0, 0, 0))],
        out_specs=pl.BlockSpec((1, CHUNK, hc * HEAD_DIM), lambda s: prev(s)),
        out_shape=jax.ShapeDtypeStruct((bsz, L, HD), BF16),
        scratch_shapes=[pltpu.VMEM((m, KV_RANK), BF16),
                        pltpu.VMEM((nchunk, m, CHUNK), F32),
                        pltpu.VMEM((m, LANES), F32),
                        pltpu.VMEM((2, m, LANES), F32),
                        pltpu.VMEM((2, m, KV_RANK), F32),
                        pltpu.VMEM((m, 3 * CHUNK), BF16)],
        compiler_params=_cparams("arbitrary"),
        name="a_attn",
    )(q, ckv4, ckvt, mask, w_uk, w_uv, bias_nd)


def _bias_kernel(rb_ref, ids_ref, o_ref, *, n_tiles, far_bucket, shift_far):
    h = pl.program_id(0)
    far = rb_ref[far_bucket, h] if shift_far else 0.0
    for t in range(n_tiles):
        ids = ids_ref[t]
        out = jnp.zeros(ids.shape, F32)
        for b in range(N_BUCKETS):
            out = jnp.where(ids == b, rb_ref[b, h] - far, out)
        o_ref[0, t] = jnp.where(ids < 0, NEG, out * LOG2E)


def _t5_bucket(dist):
    max_exact = N_BUCKETS // 2
    d = jnp.maximum(dist, 0)
    large = max_exact + (jnp.log(jnp.maximum(d, 1).astype(F32) / max_exact)
                         / math.log(MAX_DISTANCE / max_exact)
                         * (N_BUCKETS - max_exact)).astype(I32)
    large = jnp.minimum(large, N_BUCKETS - 1)
    return jnp.where(d < max_exact, d, large)


def _bias_tiles(rel_bias, ids, shift_far):
    n_tiles, r, c = ids.shape
    return pl.pallas_call(
        functools.partial(_bias_kernel, n_tiles=n_tiles, far_bucket=N_BUCKETS - 1,
                          shift_far=shift_far),
        grid=(N_HEADS,),
        in_specs=[pl.BlockSpec(memory_space=pltpu.SMEM),
                  pl.BlockSpec((n_tiles, r, c), lambda h: (0, 0, 0))],
        out_specs=pl.BlockSpec((1, n_tiles, r, c), lambda h: (h, 0, 0, 0)),
        out_shape=jax.ShapeDtypeStruct((N_HEADS, n_tiles, r, c), F32),
        compiler_params=_cparams("parallel"),
        name="bias_tiles",
    )(rel_bias, ids)


def _b_inproj_kernel(x_ref, g_ref, sh_ref, sc_ref, w_ref, b_ref, q_ref, k_ref, v_ref):
    h = _normmod(x_ref[0], g_ref[...], sh_ref[0], sc_ref[0]).astype(BF16)
    proj = jnp.dot(h, w_ref[...], preferred_element_type=F32) + b_ref[...]
    q_ref[0] = (proj[:, :HD] * (HEAD_DIM ** -0.5 * LOG2E)).astype(BF16)
    k_ref[0] = proj[:, HD:HD + KV_HEADS * HEAD_DIM].astype(BF16)
    v_ref[0] = proj[:, HD + KV_HEADS * HEAD_DIM:].astype(BF16)


def _b_inproj(x, g, sh, sc, w, layer, b, tm=512):
    bsz, L, d = x.shape
    kvw = KV_HEADS * HEAD_DIM
    row = lambda b_, i: (b_, i, 0)
    per_b = lambda b_, i: (b_, 0, 0)
    const2 = lambda b_, i: (0, 0)
    return pl.pallas_call(
        _b_inproj_kernel,
        grid=(bsz, L // tm),
        in_specs=[pl.BlockSpec((1, tm, d), row),
                  pl.BlockSpec((1, d), const2),
                  pl.BlockSpec((1, 1, d), per_b),
                  pl.BlockSpec((1, 1, d), per_b),
                  _layer_spec(w, layer),
                  pl.BlockSpec((1, B_IN), const2)],
        out_specs=[pl.BlockSpec((1, tm, HD), row),
                   pl.BlockSpec((1, tm, kvw), row),
                   pl.BlockSpec((1, tm, kvw), row)],
        out_shape=[jax.ShapeDtypeStruct((bsz, L, HD), BF16),
                   jax.ShapeDtypeStruct((bsz, L, kvw), BF16),
                   jax.ShapeDtypeStruct((bsz, L, kvw), BF16)],
        compiler_params=_cparams("parallel", "parallel"),
        name="b_inproj",
    )(x, g, sh, sc, w, b)


def _b_window_attention(sink_ref, q_block, kp, kc, vp, vc, bias_ref, first):
    w = WINDOW
    pairs = N_HEADS // KV_HEADS // 2
    kall = jnp.concatenate([kp, kc], axis=0).astype(F32)
    vall = jnp.concatenate([vp, vc], axis=0).astype(F32)
    low = lax.broadcasted_iota(I32, kall.shape, 1) < HEAD_DIM

    def padded(x, g):
        swapped = pltpu.roll(x, HEAD_DIM, axis=1)
        on_low, on_high = (x, swapped) if g == 0 else (swapped, x)
        return (jnp.where(low, on_low, 0.0).astype(BF16), jnp.where(low, 0.0, on_high).astype(BF16))

    kpad = [padded(kall, g) for g in range(KV_HEADS)]
    vpad = [padded(vall, g) for g in range(KV_HEADS)]

    def attend(blk, g):
        variant = jnp.where(first, 1, 0) if blk == 0 else 0
        keys = slice(blk * w, (blk + 2) * w)
        q = q_block(blk)
        blocks = [q[:, (pairs * g + p) * 2 * HEAD_DIM:(pairs * g + p + 1) * 2 * HEAD_DIM]
                  for p in range(pairs)]
        lg = lax.dot_general(jnp.concatenate(blocks, axis=0),
                             jnp.concatenate([kpad[g][0][keys], kpad[g][1][keys]], axis=0),
                             _NT, preferred_element_type=F32)
        lg = lg + bias_ref[variant, g]
        probs = [[], []]
        for p in range(pairs):
            for par in range(2):
                h = 2 * pairs * g + 2 * p + par
                t = lg[p * w:(p + 1) * w, par * 2 * w:(par + 1) * 2 * w]
                sink = sink_ref[0, h] * LOG2E
                m = jnp.maximum(jnp.max(t, axis=-1, keepdims=True), sink)
                e = jnp.exp2(t - m)
                denom = jnp.sum(e, axis=-1, keepdims=True) + jnp.exp2(sink - m)
                probs[par].append((e * (1.0 / denom)).astype(BF16))
        og = (jnp.dot(jnp.concatenate(probs[0], axis=0), vpad[g][0][keys],
                      preferred_element_type=F32)
              + jnp.dot(jnp.concatenate(probs[1], axis=0), vpad[g][1][keys],
                        preferred_element_type=F32))
        return jnp.concatenate([og[p * w:(p + 1) * w] for p in range(pairs)], axis=1).astype(BF16)

    return attend


FF_CHUNK = 256


def _out_ffn_tile(o, wo_ref, bo_ref, g1_ref, x_ref, g_ref, sh_ref, sc_ref, gate_ref,
                  w1_ref, w3_ref, w2_ref, fg_ref, y_ref, final_norm):
    mix = jnp.dot(o, wo_ref[...], preferred_element_type=F32) + bo_ref[...]
    x = x_ref[0] + g1_ref[0] * mix
    h = _normmod(x, g_ref[...], sh_ref[0], sc_ref[0]).astype(BF16)
    acc = jnp.zeros(x.shape, F32)
    for c in range(D_FF // FF_CHUNK):
        cs = slice(c * FF_CHUNK, (c + 1) * FF_CHUNK)
        a1 = jnp.dot(h, w1_ref[:, cs], preferred_element_type=F32)
        a3 = jnp.dot(h, w3_ref[:, cs], preferred_element_type=F32)
        act = (a1 * jax.nn.sigmoid(a1)) * a3
        acc = acc + jnp.dot(act.astype(BF16), w2_ref[cs, :], preferred_element_type=F32)
    y = x + gate_ref[0] * acc
    if final_norm:
        y = (y * lax.rsqrt(jnp.mean(y * y, axis=-1, keepdims=True) + RMS_EPS)) * fg_ref[...]
    y_ref[0] = y


def _out_ffn_kernel(o_ref, *refs, final_norm):
    _out_ffn_tile(o_ref[0], *refs, final_norm)


def _b_attn_ffn_kernel(sink_ref, q_ref, kp_ref, kc_ref, vp_ref, vc_ref, bias_ref, *refs,
                       final_norm, n_blocks, tiles_per_seq, n_work):
    ffn_refs, o_scr = refs[:-1], refs[-1]
    s = pl.program_id(0)
    slot = s & 1
    w = WINDOW

    def placeholder(r, carry):
        o_scr[1, pl.ds(pl.multiple_of(r * w, w), w), :] = jnp.zeros((w, HD), BF16)
        return carry

    lax.fori_loop(0, jnp.where(s == 0, n_blocks, 0), placeholder, 0)

    first = lax.rem(jnp.minimum(s, n_work - 1), tiles_per_seq) == 0
    attend = _b_window_attention(sink_ref, lambda blk: q_ref[0, blk * w:(blk + 1) * w, :],
                                 kp_ref[0], kc_ref[0], vp_ref[0], vc_ref[0], bias_ref, first)

    _out_ffn_tile(o_scr[1 - slot], *ffn_refs, final_norm)
    gw = HD // KV_HEADS
    for blk in range(n_blocks):
        for g in range(KV_HEADS):
            o_scr[slot, blk * w:(blk + 1) * w, g * gw:(g + 1) * gw] = attend(blk, g)


def _b_attn_ffn(q, k, v, sinks, bias_b, w_out, mix_layer, b_out, g1, x, g, sh, sc, gate, w1, w3,
                w2, layer, final_g, final_norm, tm=512):
    bsz, L, d = x.shape
    w = WINDOW
    kvw = KV_HEADS * HEAD_DIM
    tiles = L // tm
    n_work = bsz * tiles

    def tile(s):
        return s // tiles, s % tiles

    cur = lambda s: tile(jnp.minimum(s, n_work - 1))
    prev = lambda s: tile(jnp.maximum(s - 1, 0))
    cur_row = lambda s: cur(s) + (0,)
    before = lambda s: (cur(s)[0], jnp.maximum(cur(s)[1] * (tm // w) - 1, 0), 0)
    prev_row = lambda s: prev(s) + (0,)
    prev_b = lambda s: (prev(s)[0], 0, 0)
    const2 = lambda s: (0, 0)
    return pl.pallas_call(
        functools.partial(_b_attn_ffn_kernel, final_norm=final_norm, n_blocks=tm // w,
                          tiles_per_seq=tiles, n_work=n_work),
        grid=(n_work + 1,),
        in_specs=[pl.BlockSpec(memory_space=pltpu.SMEM),
                  pl.BlockSpec((1, tm, HD), cur_row),
                  pl.BlockSpec((1, w, kvw), before),
                  pl.BlockSpec((1, tm, kvw), cur_row),
                  pl.BlockSpec((1, w, kvw), before),
                  pl.BlockSpec((1, tm, kvw), cur_row),
                  pl.BlockSpec(bias_b.shape, lambda s: (0, 0, 0, 0)),
                  _layer_spec(w_out, mix_layer),
                  pl.BlockSpec((1, d), const2),
                  pl.BlockSpec((1, 1, d), prev_b),
                  pl.BlockSpec((1, tm, d), prev_row),
                  pl.BlockSpec((1, d), const2),
                  pl.BlockSpec((1, 1, d), prev_b),
                  pl.BlockSpec((1, 1, d), prev_b),
                  pl.BlockSpec((1, 1, d), prev_b),
                  _layer_spec(w1, layer),
                  _layer_spec(w3, layer),
                  _layer_spec(w2, layer),
                  pl.BlockSpec((1, d), const2)],
        out_specs=pl.BlockSpec((1, tm, d), prev_row),
        out_shape=jax.ShapeDtypeStruct((bsz, L, d), F32),
        scratch_shapes=[pltpu.VMEM((2, tm, HD), BF16)],
        compiler_params=_cparams("arbitrary"),
        name="b_attn_ffn",
    )(sinks.reshape(1, N_HEADS), q, k, k, v, v, bias_b, w_out, b_out, g1, x, g, sh, sc, gate,
      w1, w3, w2, final_g)


def _out_ffn(o, w_out, mix_layer, b_out, g1, x, g, sh, sc, gate, w1, w3, w2, layer, final_g,
             final_norm, tm=512):
    bsz, L, d = x.shape
    row = lambda b_, i: (b_, i, 0)
    per_b = lambda b_, i: (b_, 0, 0)
    const2 = lambda b_, i: (0, 0)
    return pl.pallas_call(
        functools.partial(_out_ffn_kernel, final_norm=final_norm),
        grid=(bsz, L // tm),
        in_specs=[pl.BlockSpec((1, tm, HD), row),
                  _layer_spec(w_out, mix_layer),
                  pl.BlockSpec((1, d), const2),
                  pl.BlockSpec((1, 1, d), per_b),
                  pl.BlockSpec((1, tm, d), row),
                  pl.BlockSpec((1, d), const2),
                  pl.BlockSpec((1, 1, d), per_b),
                  pl.BlockSpec((1, 1, d), per_b),
                  pl.BlockSpec((1, 1, d), per_b),
                  _layer_spec(w1, layer),
                  _layer_spec(w3, layer),
                  _layer_spec(w2, layer),
                  pl.BlockSpec((1, d), const2)],
        out_specs=pl.BlockSpec((1, tm, d), row),
        out_shape=jax.ShapeDtypeStruct((bsz, L, d), F32),
        compiler_params=_cparams("parallel", "parallel"),
        name="out_ffn",
    )(o, w_out, b_out, g1, x, g, sh, sc, gate, w1, w3, w2, final_g)


def kernel(x, c, rel_bias, w_ada, b_ada, norm_mix_g, norm_ffn_g, a_w_in, a_kv_norm_g, a_w_uk,
           a_w_uv, a_idx_k_g, a_idx_k_b, a_w_out, b_w_in, b_b_in, b_sinks, b_w_out, b_b_out,
           ffn_w1, ffn_w3, ffn_w2, norm_final_g):
    bsz, L, d = x.shape
    depth = w_ada.shape[0]
    topk = min(INDEX_TOPK, L // 4)

    mod = _adaln(c, w_ada, b_ada)

    r = jnp.arange(BIAS_BLK)
    ids_a = jnp.stack([_t5_bucket(r[:, None] - r[None, :]),
                       _t5_bucket(r[:, None] - r[None, :] + BIAS_BLK)])
    bias_a = _bias_tiles(rel_bias, ids_a, shift_far=True)
    dist_b = jnp.arange(WINDOW)[:, None] + WINDOW - jnp.arange(2 * WINDOW)[None, :]
    in_window = (dist_b >= 0) & (dist_b < WINDOW)
    ids_b = jnp.where(in_window, _t5_bucket(dist_b), -1)
    ids_b0 = jnp.where(jnp.arange(2 * WINDOW)[None, :] >= WINDOW, ids_b, -1)
    bias_b = _bias_tiles(rel_bias, jnp.stack([ids_b, ids_b0]), shift_far=False)
    pairs = N_HEADS // KV_HEADS // 2
    bias_b = bias_b.reshape(KV_HEADS, pairs, 2, 2, WINDOW, 2 * WINDOW).transpose(3, 0, 1, 4, 2, 5)
    bias_b = bias_b.reshape(2, KV_HEADS, pairs * WINDOW, 4 * WINDOW)

    a_w_main = jnp.pad(a_w_in[:, :, :A_MAIN], ((0, 0), (0, 0), (0, A_MAIN_PAD - A_MAIN))).astype(BF16)
    a_w_wi_t = jnp.pad(a_w_in[:, :, A_WI0:A_WI0 + IDX_HEADS].transpose(0, 2, 1),
                       ((0, 0), (0, 16 - IDX_HEADS), (0, 0))).astype(BF16)
    a_w_uk, a_w_uv, a_w_out, b_w_in, b_w_out, ffn_w1, ffn_w3, ffn_w2 = [
        w.astype(BF16) for w in (jnp.swapaxes(_pair_blocks(a_w_uk), -1, -2),
                                 _pair_blocks(a_w_uv), a_w_out, b_w_in,
                                 b_w_out, ffn_w1, ffn_w3, ffn_w2)]

    zero_bias = jnp.zeros((1, d), F32)
    for i in range(depth):
        sh1, sc1, g1, sh2, sc2, g2 = [m.reshape(bsz, 1, d) for m in jnp.split(mod[i], 6, axis=-1)]
        jm = i // 2
        ffn_args = (g1, x, norm_ffn_g[i][None], sh2, sc2, g2, ffn_w1, ffn_w3, ffn_w2, i,
                    norm_final_g[None], i == depth - 1)
        if i % 2 == 0:
            q, ckv, ckvt, qi, ki, wit = _a_inproj(
                x, norm_mix_g[i][None], sh1, sc1, a_w_main, a_w_wi_t, jm, a_kv_norm_g[jm][None],
                a_idx_k_g[jm][None], a_idx_k_b[jm][None])
            mask = _a_index(qi, wit, ki, topk)
            o = _a_attn(q, ckv, ckvt, mask, a_w_uk, a_w_uv, jm, bias_a)
            x = _out_ffn(o, a_w_out, jm, zero_bias, *ffn_args)
        else:
            q, k, v = _b_inproj(x, norm_mix_g[i][None], sh1, sc1, b_w_in, jm, b_b_in[jm][None])
            x = _b_attn_ffn(q, k, v, b_sinks[jm], bias_b, b_w_out, jm, b_b_out[jm][None], *ffn_args)
    return x
```

```python
import functools
import math

import numpy as np
import jax
import jax.numpy as jnp
from jax import lax
from jax.experimental import pallas as pl
from jax.experimental.pallas import tpu as pltpu

N_HEADS = 16
HEAD_DIM = 64
KV_RANK = 256
IDX_HEADS = 8
IDX_DIM = 64
INDEX_TOPK = 256
KV_HEADS = 2
WINDOW = 128
N_BUCKETS = 32
MAX_DISTANCE = 128
D_FF = 2816
RMS_EPS = 1e-6
NEG = -1e30

HD = N_HEADS * HEAD_DIM
A_Q0, A_KV0, A_QI0, A_KI0, A_WI0 = 0, HD, HD + KV_RANK, HD + KV_RANK + IDX_HEADS * IDX_DIM, \
    HD + KV_RANK + IDX_HEADS * IDX_DIM + IDX_DIM
LANES = 128
HALF_ROWS = 16
A_MAIN = A_WI0
A_MAIN_PAD = -(-A_MAIN // LANES) * LANES
B_IN = (N_HEADS + 2 * KV_HEADS) * HEAD_DIM

CHUNK = 256
PV_ROWS = 64
ACC_ROWS = 256
BIAS_BLK = 128
VMEM_LIMIT = 56 * 1024 * 1024

F32 = jnp.float32
BF16 = jnp.bfloat16
I32 = jnp.int32
I16 = jnp.int16

_NT = (((1,), (1,)), ((), ()))


def _cparams(*sem):
    return pltpu.CompilerParams(dimension_semantics=sem, vmem_limit_bytes=VMEM_LIMIT)


def _pair_blocks(w):
    even, odd = w[..., 0::2, :, :], w[..., 1::2, :, :]
    zero = jnp.zeros_like(even)
    return jnp.concatenate([jnp.concatenate([even, zero], axis=-1),
                            jnp.concatenate([zero, odd], axis=-1)], axis=-2)


def _layer_spec(stack, layer):
    zeros = (0,) * (stack.ndim - 1)
    return pl.BlockSpec((None,) + stack.shape[1:], lambda *_: (layer,) + zeros)


def _f32_key(v):
    b = int(np.array(v, np.float32).view(np.int32))
    return b ^ ((b >> 31) & 0x7FFFFFFF)


LOG2E = math.log2(math.e)
KEY_NEG = _f32_key(NEG)
INT_MIN = -(2 ** 31)


def _adaln_kernel(c_ref, w_ref, b_ref, o_ref):
    c = c_ref[...]
    cs = c * jax.nn.sigmoid(c)
    o_ref[0] = jnp.dot(cs, w_ref[0], preferred_element_type=F32,
                       precision=lax.Precision.HIGHEST) + b_ref[0]


def _adaln(c, w_ada, b_ada):
    depth, d, n = w_ada.shape
    bsz = c.shape[0]
    tn = 1536
    return pl.pallas_call(
        _adaln_kernel,
        grid=(depth, n // tn),
        in_specs=[pl.BlockSpec((bsz, d), lambda i, j: (0, 0)),
                  pl.BlockSpec((1, d, tn), lambda i, j: (i, 0, j)),
                  pl.BlockSpec((1, 1, tn), lambda i, j: (i, 0, j))],
        out_specs=pl.BlockSpec((1, bsz, tn), lambda i, j: (i, 0, j)),
        out_shape=jax.ShapeDtypeStruct((depth, bsz, n), F32),
        compiler_params=_cparams("parallel", "parallel"),
        name="adaln",
    )(c, w_ada, b_ada.reshape(depth, 1, n))


def _normmod(x, g, sh, sc):
    ms = jnp.mean(x * x, axis=-1, keepdims=True)
    y = (x * lax.rsqrt(ms + RMS_EPS)) * g
    return y * (1.0 + sc) + sh


def _a_inproj_kernel(x_ref, g_ref, sh_ref, sc_ref, w_ref, wwi_ref, kvg_ref, ikg_ref, ikb_ref,
                     q_ref, ckv_ref, ckvt_ref, qi_ref, ki_ref, wit_ref):
    h = _normmod(x_ref[0], g_ref[...], sh_ref[0], sc_ref[0]).astype(BF16)
    proj = jnp.dot(h, w_ref[...], preferred_element_type=F32)
    q_ref[0] = (proj[:, A_Q0:A_KV0] * (HEAD_DIM ** -0.5)).astype(BF16)
    ckv = proj[:, A_KV0:A_QI0]
    ckv = (ckv * lax.rsqrt(jnp.mean(ckv * ckv, axis=-1, keepdims=True) + RMS_EPS)) * kvg_ref[...]
    ckv_ref[0] = ckv.astype(BF16)
    for c in range(ckvt_ref.shape[1]):
        ckvt_ref[0, c] = ckv[c * CHUNK:(c + 1) * CHUNK].T.astype(BF16)
    qi_ref[0] = proj[:, A_QI0:A_KI0].astype(BF16)
    ki = proj[:, A_KI0:A_WI0]
    mu = jnp.mean(ki, axis=-1, keepdims=True)
    var = jnp.mean(jnp.square(ki - mu), axis=-1, keepdims=True)
    ki = ((ki - mu) * lax.rsqrt(var + RMS_EPS)) * ikg_ref[...] + ikb_ref[...]
    ki_ref[0] = ki.astype(BF16)
    wit = lax.dot_general(wwi_ref[...], h, _NT, preferred_element_type=F32)
    wit_ref[0] = wit[:IDX_HEADS] * (IDX_HEADS ** -0.5 * IDX_DIM ** -0.5)


def _a_inproj(x, g, sh, sc, w_main, w_wi_t, layer, kv_g, ik_g, ik_b, tm=512):
    bsz, L, d = x.shape
    row = lambda b, i: (b, i, 0)
    per_b = lambda b, i: (b, 0, 0)
    const2 = lambda b, i: (0, 0)
    return pl.pallas_call(
        _a_inproj_kernel,
        grid=(bsz, L // tm),
        in_specs=[pl.BlockSpec((1, tm, d), row),
                  pl.BlockSpec((1, d), const2),
                  pl.BlockSpec((1, 1, d), per_b),
                  pl.BlockSpec((1, 1, d), per_b),
                  _layer_spec(w_main, layer),
                  _layer_spec(w_wi_t, layer),
                  pl.BlockSpec((1, KV_RANK), const2),
                  pl.BlockSpec((1, IDX_DIM), const2),
                  pl.BlockSpec((1, IDX_DIM), const2)],
        out_specs=[pl.BlockSpec((1, tm, HD), row),
                   pl.BlockSpec((1, tm, KV_RANK), row),
                   pl.BlockSpec((1, tm // CHUNK, KV_RANK, CHUNK), lambda b, i: (b, i, 0, 0)),
                   pl.BlockSpec((1, tm, IDX_HEADS * IDX_DIM), row),
                   pl.BlockSpec((1, tm, IDX_DIM), row),
                   pl.BlockSpec((1, IDX_HEADS, tm), lambda b, i: (b, 0, i))],
        out_shape=[jax.ShapeDtypeStruct((bsz, L, HD), BF16),
                   jax.ShapeDtypeStruct((bsz, L, KV_RANK), BF16),
                   jax.ShapeDtypeStruct((bsz, L // CHUNK, KV_RANK, CHUNK), BF16),
                   jax.ShapeDtypeStruct((bsz, L, IDX_HEADS * IDX_DIM), BF16),
                   jax.ShapeDtypeStruct((bsz, L, IDX_DIM), BF16),
                   jax.ShapeDtypeStruct((bsz, IDX_HEADS, L), F32)],
        compiler_params=_cparams("parallel", "parallel"),
        name="a_inproj",
    )(x, g, sh, sc, w_main, w_wi_t, kv_g, ik_g, ik_b)


def _a_index_kernel(qi_ref, wit_ref, ki_ref, tri_ref, mask_ref, key_scr, half_scr, *, topk,
                    seq_len):
    j = pl.program_id(1)
    nchunk = seq_len // CHUNK
    qi = qi_ref[0]
    wit = wit_ref[0]
    t_glob = j * CHUNK + lax.broadcasted_iota(I32, (CHUNK, CHUNK), 1)
    s_loc = lax.broadcasted_iota(I32, (CHUNK, CHUNK), 0)

    def chunks(fn, init):
        def pair(i, carry):
            return fn(2 * i + 1, fn(2 * i, carry))
        carry = lax.fori_loop(0, lax.shift_right_logical(j + 1, 1), pair, init)
        return lax.cond((j & 1) == 0, lambda c: fn(j, c), lambda c: c, carry)

    def score_chunk(kc, carry):
        kik = ki_ref[0, pl.ds(pl.multiple_of(kc * CHUNK, CHUNK), CHUNK), :]
        acc = jnp.zeros((CHUNK, CHUNK), F32)
        for h in range(IDX_HEADS):
            r = lax.dot_general(kik, qi[:, h * IDX_DIM:(h + 1) * IDX_DIM], _NT,
                                preferred_element_type=F32)
            acc = acc + jnp.maximum(r, 0.0) * wit[h:h + 1, :]
        sc = jnp.where(kc * CHUNK + s_loc <= t_glob, acc, NEG)
        bits = lax.bitcast_convert_type(sc, I32)
        key = bits ^ ((bits >> 31) & 0x7FFFFFFF)
        key_scr[kc] = key
        half_scr[kc] = (key >> 16).astype(I16)
        return carry

    chunks(score_chunk, 0)

    n_beyond = seq_len - (j + 1) * CHUNK
    ge = lambda a, b: a >= b
    gt = lambda a, b: a > b

    def count(pred, thr):
        def body(kc, acc):
            m = jnp.where(pred(key_scr[kc], thr), jnp.int32(1), jnp.int32(0))
            return acc + m.reshape(CHUNK // 8, 8, CHUNK).sum(axis=0)
        acc = chunks(body, jnp.zeros((8, CHUNK), I32))
        cnt = acc.sum(axis=0, keepdims=True)
        return cnt + jnp.where(pred(KEY_NEG, thr), n_beyond, 0)

    def count_half(pred, thr16):
        t = jnp.broadcast_to(thr16, (HALF_ROWS, CHUNK)).astype(I16)
        def body(kc, acc):
            m = jnp.where(pred(half_scr[kc].reshape(CHUNK // HALF_ROWS, HALF_ROWS, CHUNK), t[None]),
                          jnp.int16(1), jnp.int16(0))
            for r in range(CHUNK // HALF_ROWS):
                acc = acc + m[r]
            return acc
        acc = chunks(body, jnp.zeros((HALF_ROWS, CHUNK), I16))
        return acc.astype(I32).sum(axis=0, keepdims=True)

    def beyond(cand):
        return jnp.where(KEY_NEG >= cand, n_beyond, 0)

    lowest = jnp.full((1, CHUNK), INT_MIN, I32), jnp.full((1, CHUNK), seq_len, I32)

    def step(state, cand, cnt):
        keep = cnt >= topk
        return jnp.where(keep, cand, state[0]), jnp.where(keep, cnt, state[1])

    def bisect():
        zero = jnp.zeros((1, CHUNK), I32)
        state = step(lowest, zero, count_half(ge, zero) + beyond(zero))

        def bisect_high(i, state):
            cand = state[0] + lax.shift_left(jnp.int32(1), 30 - i)
            return step(state, cand, count_half(ge, cand >> 16) + beyond(cand))

        state = lax.fori_loop(0, 15, bisect_high, state)
        thr = state[0]

        thr_hi = thr >> 16
        n_above = count_half(gt, thr_hi)

        def low_half(kc, carry):
            key = key_scr[kc]
            low = (key & 0xFFFF) - 0x8000
            half_scr[kc] = jnp.where((key >> 16) == thr_hi, low, -0x8000).astype(I16)
            return carry

        chunks(low_half, 0)

        def bisect_low(i, state):
            cand = state[0] + lax.shift_left(jnp.int32(1), 15 - i)
            cnt = n_above + count_half(ge, (cand & 0xFFFF) - 0x8000) + beyond(cand)
            return step(state, cand, cnt)

        return lax.fori_loop(0, 16, bisect_low, state)

    takes_all = (j + 1) * CHUNK <= topk
    thr, n_at_thr = lax.cond(takes_all, lambda: lowest, bisect)

    def emit_ranked():
        need = (topk - count(gt, thr)).astype(F32)
        tri = tri_ref[...]

        def emit(kc, seen):
            key = key_scr[kc]
            eq = jnp.where(key == thr, 1.0, 0.0)
            rank = jnp.dot(tri, eq.astype(BF16), preferred_element_type=F32) + seen
            take = jnp.where(key > thr, 1.0, jnp.where(rank < need, eq, 0.0))
            causal = kc * CHUNK + s_loc <= t_glob
            m = jnp.where(causal, jnp.where(take > 0.5, 0.0, NEG), NEG)
            mask_ref[0, kc] = m.T
            return seen + jnp.sum(eq, axis=0, keepdims=True)

        chunks(emit, jnp.zeros((1, CHUNK), F32))

    def emit_plain():
        def emit(kc, carry):
            causal = kc * CHUNK + s_loc <= t_glob
            m = jnp.where(causal, jnp.where(key_scr[kc] >= thr, 0.0, NEG), NEG)
            mask_ref[0, kc] = m.T
            return carry

        chunks(emit, 0)

    exact_fit = takes_all | (jnp.max(n_at_thr) <= topk)
    lax.cond(exact_fit, emit_plain, emit_ranked)

    def fill(kc, carry):
        mask_ref[0, kc] = jnp.full((CHUNK, CHUNK), NEG, F32)
        return carry

    lax.fori_loop(j + 1, nchunk, fill, 0)


def _a_index(qi, wit, ki, topk):
    bsz, L, _ = qi.shape
    nchunk = L // CHUNK
    tri = jnp.tril(jnp.ones((CHUNK, CHUNK), BF16), -1)
    return pl.pallas_call(
        functools.partial(_a_index_kernel, topk=topk, seq_len=L),
        grid=(bsz, nchunk),
        in_specs=[pl.BlockSpec((1, CHUNK, IDX_HEADS * IDX_DIM), lambda b, j: (b, j, 0)),
                  pl.BlockSpec((1, IDX_HEADS, CHUNK), lambda b, j: (b, 0, j)),
                  pl.BlockSpec((1, L, IDX_DIM), lambda b, j: (b, 0, 0)),
                  pl.BlockSpec((CHUNK, CHUNK), lambda b, j: (0, 0))],
        out_specs=pl.BlockSpec((1, nchunk, CHUNK, CHUNK), lambda b, j: (b, 0, j, 0)),
        out_shape=jax.ShapeDtypeStruct((bsz, nchunk, L, CHUNK), F32),
        scratch_shapes=[pltpu.VMEM((nchunk, CHUNK, CHUNK), I32),
                        pltpu.VMEM((nchunk, CHUNK, CHUNK), I16)],
        compiler_params=_cparams("parallel", "parallel"),
        name="a_index",
    )(qi, wit, ki, tri)


def _a_attn_kernel(q_ref, ckv_ref, ckvt_ref, mask_ref, wuk_ref, wuv_ref, bias_ref, o_ref,
                   qabs_scr, lg_scr, m_scr, l_scr, acc_scr, p_scr, *, hc, nq, n_work):
    s = pl.program_id(0)
    j = (jnp.minimum(s, n_work - 1) // (N_HEADS // hc)) % nq
    slot = s & 1
    tq = CHUNK

    def placeholder(r, carry):
        rows = pl.ds(pl.multiple_of(r * ACC_ROWS, ACC_ROWS), ACC_ROWS)
        l_scr[1, rows, :] = jnp.ones((ACC_ROWS, LANES), F32)
        acc_scr[1, rows, :] = jnp.zeros((ACC_ROWS, KV_RANK), F32)
        return carry

    lax.fori_loop(0, jnp.where(s == 0, hc * tq // ACC_ROWS, 0), placeholder, 0)

    q = q_ref[0]
    for i in range(hc // 2):
        qa = jnp.dot(q[:, i * 2 * HEAD_DIM:(i + 1) * 2 * HEAD_DIM], wuk_ref[i],
                     preferred_element_type=F32)
        for par in range(2):
            qabs_scr[(2 * i + par) * tq:(2 * i + par + 1) * tq, :] = (
                qa[:, par * KV_RANK:(par + 1) * KV_RANK] * LOG2E).astype(BF16)
    o = (acc_scr[1 - slot] / jnp.sum(l_scr[1 - slot], axis=-1, keepdims=True)).astype(BF16)
    outs = [jnp.dot(jnp.concatenate([o[2 * i * tq:(2 * i + 1) * tq],
                                     o[(2 * i + 1) * tq:(2 * i + 2) * tq]], axis=1),
                    wuv_ref[i], preferred_element_type=F32) for i in range(hc // 2)]
    o_ref[0] = jnp.concatenate(outs, axis=1).astype(BF16)
    qg = qabs_scr[...]

    n_far_pairs = lax.shift_right_arithmetic(j - 1, 1)
    near_is_pair = (j & 1) == 1
    near_is_triple = ((j & 1) == 0) & (j >= 2)
    far_top = j - 4 + (j & 1)
    n_far_quads = lax.shift_right_arithmetic(n_far_pairs, 1)
    far_pair_left = (n_far_pairs >= 1) & ((n_far_pairs & 1) == 1)

    def keys(kc, width):
        return ckv_ref[0, pl.ds(kc, width)].reshape(width * CHUNK, KV_RANK)

    def logits(kc, width):
        keys_t = jnp.concatenate([ckvt_ref[0, kc + w] for w in range(width)], axis=1)
        lg = jnp.dot(qg, keys_t, preferred_element_type=F32)
        mk = jnp.concatenate([mask_ref[0, kc + w] for w in range(width)], axis=1)
        return lg + jnp.concatenate([mk] * hc, axis=0)

    def near_bias(i, width):
        d0, d1 = bias_ref[i, 0], bias_ref[i, 1]
        z = jnp.zeros_like(d0)
        top, bot = [d0, z], [d1, d0]
        if width >= 2:
            top, bot = [z, d1] + top, [z, z] + bot
        if width == 3:
            top, bot = [z, z] + top, [z, z] + bot
        return jnp.concatenate([jnp.concatenate(top, axis=1), jnp.concatenate(bot, axis=1)], axis=0)

    def near_logits(kc, width):
        bias = jnp.concatenate([near_bias(i, width) for i in range(hc)], axis=0)
        return logits(kc, width) + bias

    def put_logits(kc, width, lg):
        for w in range(width):
            lg_scr[kc + w] = lg[:, w * CHUNK:(w + 1) * CHUNK]
        m = m_scr[...]
        for c in range(width * CHUNK // LANES):
            m = jnp.maximum(m, lg[:, c * LANES:(c + 1) * LANES])
        m_scr[...] = m

    m_scr[...] = jnp.full(m_scr.shape, -jnp.inf, F32)

    @pl.when(near_is_pair)
    def _():
        put_logits(j - 1, 2, near_logits(j - 1, 2))

    @pl.when(near_is_triple)
    def _():
        put_logits(j - 2, 3, near_logits(j - 2, 3))

    @pl.when(j == 0)
    def _():
        put_logits(0, 1, near_logits(0, 1))

    def far_quad(i, carry):
        kc = far_top - 2 - 4 * i
        put_logits(kc, 4, logits(kc, 4))
        return carry

    lax.fori_loop(0, n_far_quads, far_quad, 0)

    @pl.when(far_pair_left)
    def _():
        put_logits(0, 2, logits(0, 2))

    def pv(kc, width, first):
        ck = keys(kc, width)
        for piece in range(hc * tq // ACC_ROWS):
            for r in range(piece * ACC_ROWS // PV_ROWS, (piece + 1) * ACC_ROWS // PV_ROWS):
                rows = slice(r * PV_ROWS, (r + 1) * PV_ROWS)
                m = m_scr[rows, :]
                if first:
                    m = jnp.broadcast_to(jnp.max(m, axis=-1, keepdims=True), m.shape)
                    m_scr[rows, :] = m
                l = jnp.zeros((PV_ROWS, LANES), F32) if first else l_scr[slot, rows, :]
                for w in range(width):
                    for c in range(CHUNK // LANES):
                        col = w * CHUNK + c * LANES
                        p = jnp.exp2(lg_scr[kc + w, rows, c * LANES:(c + 1) * LANES] - m)
                        l = l + p
                        p_scr[rows, col:col + LANES] = p.astype(BF16)
                l_scr[slot, rows, :] = l
            rows = slice(piece * ACC_ROWS, (piece + 1) * ACC_ROWS)
            part = jnp.dot(p_scr[rows, :width * CHUNK], ck, preferred_element_type=F32)
            acc_scr[slot, rows, :] = part if first else acc_scr[slot, rows, :] + part

    @pl.when(near_is_pair)
    def _():
        pv(j - 1, 2, True)

    @pl.when(near_is_triple)
    def _():
        pv(j - 2, 3, True)

    @pl.when(j == 0)
    def _():
        pv(0, 1, True)

    def far_pv(i, carry):
        pv(far_top - 2 - 4 * i, 4, False)
        return carry

    lax.fori_loop(0, n_far_quads, far_pv, 0)

    @pl.when(far_pair_left)
    def _():
        pv(0, 2, False)


def _a_attn(q, ckv, ckvt, mask, w_uk, w_uv, layer, bias_nd, hc=8):
    bsz, L, _ = q.shape
    nchunk = L // CHUNK
    hg = N_HEADS // hc
    m = hc * CHUNK
    ckv4 = ckv.reshape(bsz, nchunk, CHUNK, KV_RANK)
    n_work = bsz * nchunk * hg

    def item(s):
        return s // (nchunk * hg), (s // hg) % nchunk, s % hg

    cur = lambda s: item(jnp.minimum(s, n_work - 1))
    prev = lambda s: item(jnp.maximum(s - 1, 0))
    return pl.pallas_call(
        functools.partial(_a_attn_kernel, hc=hc, nq=nchunk, n_work=n_work),
        grid=(n_work + 1,),
        in_specs=[pl.BlockSpec((1, CHUNK, hc * HEAD_DIM), lambda s: cur(s)),
                  pl.BlockSpec((1, nchunk, CHUNK, KV_RANK), lambda s: (cur(s)[0], 0, 0, 0)),
                  pl.BlockSpec((1, nchunk, KV_RANK, CHUNK), lambda s: (cur(s)[0], 0, 0, 0)),
                  pl.BlockSpec((1, nchunk, CHUNK, CHUNK), lambda s: (cur(s)[0], 0, cur(s)[1], 0)),
                  pl.BlockSpec((None, hc // 2) + w_uk.shape[2:], lambda s: (layer, cur(s)[2], 0, 0)),
                  pl.BlockSpec((None, hc // 2) + w_uv.shape[2:], lambda s: (layer, prev(s)[2], 0, 0)),
                  pl.BlockSpec((hc, 2, BIAS_BLK, BIAS_BLK), lambda s: (cur(s)[2], 0, 0, 0))],
        out_specs=pl.BlockSpec((1, CHUNK, hc * HEAD_DIM), lambda s: prev(s)),
        out_shape=jax.ShapeDtypeStruct((bsz, L, HD), BF16),
        scratch_shapes=[pltpu.VMEM((m, KV_RANK), BF16),
                        pltpu.VMEM((nchunk, m, CHUNK), F32),
                        pltpu.VMEM((m, LANES), F32),
                        pltpu.VMEM((2, m, LANES), F32),
                        pltpu.VMEM((2, m, KV_RANK), F32),
                        pltpu.VMEM((m, 4 * CHUNK), BF16)],
        compiler_params=_cparams("arbitrary"),
        name="a_attn",
    )(q, ckv4, ckvt, mask, w_uk, w_uv, bias_nd)


def _bias_kernel(rb_ref, ids_ref, o_ref, *, n_tiles, far_bucket, shift_far):
    h = pl.program_id(0)
    far = rb_ref[far_bucket, h] if shift_far else 0.0
    for t in range(n_tiles):
        ids = ids_ref[t]
        out = jnp.zeros(ids.shape, F32)
        for b in range(N_BUCKETS):
            out = jnp.where(ids == b, rb_ref[b, h] - far, out)
        o_ref[0, t] = jnp.where(ids < 0, NEG, out * LOG2E)


def _t5_bucket(dist):
    max_exact = N_BUCKETS // 2
    d = jnp.maximum(dist, 0)
    large = max_exact + (jnp.log(jnp.maximum(d, 1).astype(F32) / max_exact)
                         / math.log(MAX_DISTANCE / max_exact)
                         * (N_BUCKETS - max_exact)).astype(I32)
    large = jnp.minimum(large, N_BUCKETS - 1)
    return jnp.where(d < max_exact, d, large)


def _bias_tiles(rel_bias, ids, shift_far):
    n_tiles, r, c = ids.shape
    return pl.pallas_call(
        functools.partial(_bias_kernel, n_tiles=n_tiles, far_bucket=N_BUCKETS - 1,
                          shift_far=shift_far),
        grid=(N_HEADS,),
        in_specs=[pl.BlockSpec(memory_space=pltpu.SMEM),
                  pl.BlockSpec((n_tiles, r, c), lambda h: (0, 0, 0))],
        out_specs=pl.BlockSpec((1, n_tiles, r, c), lambda h: (h, 0, 0, 0)),
        out_shape=jax.ShapeDtypeStruct((N_HEADS, n_tiles, r, c), F32),
        compiler_params=_cparams("parallel"),
        name="bias_tiles",
    )(rel_bias, ids)


def _b_inproj_kernel(x_ref, g_ref, sh_ref, sc_ref, w_ref, b_ref, q_ref, k_ref, v_ref):
    h = _normmod(x_ref[0], g_ref[...], sh_ref[0], sc_ref[0]).astype(BF16)
    proj = jnp.dot(h, w_ref[...], preferred_element_type=F32) + b_ref[...]
    q_ref[0] = (proj[:, :HD] * (HEAD_DIM ** -0.5 * LOG2E)).astype(BF16)
    k_ref[0] = proj[:, HD:HD + KV_HEADS * HEAD_DIM].astype(BF16)
    v_ref[0] = proj[:, HD + KV_HEADS * HEAD_DIM:].astype(BF16)


def _b_inproj(x, g, sh, sc, w, layer, b, tm=512):
    bsz, L, d = x.shape
    kvw = KV_HEADS * HEAD_DIM
    row = lambda b_, i: (b_, i, 0)
    per_b = lambda b_, i: (b_, 0, 0)
    const2 = lambda b_, i: (0, 0)
    return pl.pallas_call(
        _b_inproj_kernel,
        grid=(bsz, L // tm),
        in_specs=[pl.BlockSpec((1, tm, d), row),
                  pl.BlockSpec((1, d), const2),
                  pl.BlockSpec((1, 1, d), per_b),
                  pl.BlockSpec((1, 1, d), per_b),
                  _layer_spec(w, layer),
                  pl.BlockSpec((1, B_IN), const2)],
        out_specs=[pl.BlockSpec((1, tm, HD), row),
                   pl.BlockSpec((1, tm, kvw), row),
                   pl.BlockSpec((1, tm, kvw), row)],
        out_shape=[jax.ShapeDtypeStruct((bsz, L, HD), BF16),
                   jax.ShapeDtypeStruct((bsz, L, kvw), BF16),
                   jax.ShapeDtypeStruct((bsz, L, kvw), BF16)],
        compiler_params=_cparams("parallel", "parallel"),
        name="b_inproj",
    )(x, g, sh, sc, w, b)


def _b_window_attention(sink_ref, q_block, kp, kc, vp, vc, bias_ref, first):
    w = WINDOW
    pairs = N_HEADS // KV_HEADS // 2
    kall = jnp.concatenate([kp, kc], axis=0).astype(F32)
    vall = jnp.concatenate([vp, vc], axis=0).astype(F32)
    low = lax.broadcasted_iota(I32, kall.shape, 1) < HEAD_DIM

    def padded(x, g):
        swapped = pltpu.roll(x, HEAD_DIM, axis=1)
        on_low, on_high = (x, swapped) if g == 0 else (swapped, x)
        return (jnp.where(low, on_low, 0.0).astype(BF16), jnp.where(low, 0.0, on_high).astype(BF16))

    kpad = [padded(kall, g) for g in range(KV_HEADS)]
    vpad = [padded(vall, g) for g in range(KV_HEADS)]

    def attend(blk, g):
        variant = jnp.where(first, 1, 0) if blk == 0 else 0
        keys = slice(blk * w, (blk + 2) * w)
        q = q_block(blk)
        blocks = [q[:, (pairs * g + p) * 2 * HEAD_DIM:(pairs * g + p + 1) * 2 * HEAD_DIM]
                  for p in range(pairs)]
        lg = lax.dot_general(jnp.concatenate(blocks, axis=0),
                             jnp.concatenate([kpad[g][0][keys], kpad[g][1][keys]], axis=0),
                             _NT, preferred_element_type=F32)
        lg = lg + bias_ref[variant, g]
        probs = [[], []]
        for p in range(pairs):
            for par in range(2):
                h = 2 * pairs * g + 2 * p + par
                t = lg[p * w:(p + 1) * w, par * 2 * w:(par + 1) * 2 * w]
                sink = sink_ref[0, h] * LOG2E
                m = jnp.maximum(jnp.max(t, axis=-1, keepdims=True), sink)
                e = jnp.exp2(t - m)
                denom = jnp.sum(e, axis=-1, keepdims=True) + jnp.exp2(sink - m)
                probs[par].append((e * (1.0 / denom)).astype(BF16))
        og = (jnp.dot(jnp.concatenate(probs[0], axis=0), vpad[g][0][keys],
                      preferred_element_type=F32)
              + jnp.dot(jnp.concatenate(probs[1], axis=0), vpad[g][1][keys],
                        preferred_element_type=F32))
        return jnp.concatenate([og[p * w:(p + 1) * w] for p in range(pairs)], axis=1).astype(BF16)

    return attend


FF_CHUNK = 256


def _out_ffn_tile(o, wo_ref, bo_ref, g1_ref, x_ref, g_ref, sh_ref, sc_ref, gate_ref,
                  w1_ref, w3_ref, w2_ref, fg_ref, y_ref, final_norm):
    mix = jnp.dot(o, wo_ref[...], preferred_element_type=F32) + bo_ref[...]
    x = x_ref[0] + g1_ref[0] * mix
    h = _normmod(x, g_ref[...], sh_ref[0], sc_ref[0]).astype(BF16)
    acc = jnp.zeros(x.shape, F32)
    for c in range(D_FF // FF_CHUNK):
        cs = slice(c * FF_CHUNK, (c + 1) * FF_CHUNK)
        a1 = jnp.dot(h, w1_ref[:, cs], preferred_element_type=F32)
        a3 = jnp.dot(h, w3_ref[:, cs], preferred_element_type=F32)
        act = (a1 * jax.nn.sigmoid(a1)) * a3
        acc = acc + jnp.dot(act.astype(BF16), w2_ref[cs, :], preferred_element_type=F32)
    y = x + gate_ref[0] * acc
    if final_norm:
        y = (y * lax.rsqrt(jnp.mean(y * y, axis=-1, keepdims=True) + RMS_EPS)) * fg_ref[...]
    y_ref[0] = y


def _out_ffn_kernel(o_ref, *refs, final_norm):
    _out_ffn_tile(o_ref[0], *refs, final_norm)


def _b_attn_ffn_kernel(sink_ref, q_ref, kp_ref, kc_ref, vp_ref, vc_ref, bias_ref, *refs,
                       final_norm, n_blocks, tiles_per_seq, n_work):
    ffn_refs, o_scr = refs[:-1], refs[-1]
    s = pl.program_id(0)
    slot = s & 1
    w = WINDOW

    def placeholder(r, carry):
        o_scr[1, pl.ds(pl.multiple_of(r * w, w), w), :] = jnp.zeros((w, HD), BF16)
        return carry

    lax.fori_loop(0, jnp.where(s == 0, n_blocks, 0), placeholder, 0)

    first = lax.rem(jnp.minimum(s, n_work - 1), tiles_per_seq) == 0
    attend = _b_window_attention(sink_ref, lambda blk: q_ref[0, blk * w:(blk + 1) * w, :],
                                 kp_ref[0], kc_ref[0], vp_ref[0], vc_ref[0], bias_ref, first)

    _out_ffn_tile(o_scr[1 - slot], *ffn_refs, final_norm)
    gw = HD // KV_HEADS
    for blk in range(n_blocks):
        for g in range(KV_HEADS):
            o_scr[slot, blk * w:(blk + 1) * w, g * gw:(g + 1) * gw] = attend(blk, g)


def _b_attn_ffn(q, k, v, sinks, bias_b, w_out, mix_layer, b_out, g1, x, g, sh, sc, gate, w1, w3,
                w2, layer, final_g, final_norm, tm=512):
    bsz, L, d = x.shape
    w = WINDOW
    kvw = KV_HEADS * HEAD_DIM
    tiles = L // tm
    n_work = bsz * tiles

    def tile(s):
        return s // tiles, s % tiles

    cur = lambda s: tile(jnp.minimum(s, n_work - 1))
    prev = lambda s: tile(jnp.maximum(s - 1, 0))
    cur_row = lambda s: cur(s) + (0,)
    before = lambda s: (cur(s)[0], jnp.maximum(cur(s)[1] * (tm // w) - 1, 0), 0)
    prev_row = lambda s: prev(s) + (0,)
    prev_b = lambda s: (prev(s)[0], 0, 0)
    const2 = lambda s: (0, 0)
    return pl.pallas_call(
        functools.partial(_b_attn_ffn_kernel, final_norm=final_norm, n_blocks=tm // w,
                          tiles_per_seq=tiles, n_work=n_work),
        grid=(n_work + 1,),
        in_specs=[pl.BlockSpec(memory_space=pltpu.SMEM),
                  pl.BlockSpec((1, tm, HD), cur_row),
                  pl.BlockSpec((1, w, kvw), before),
                  pl.BlockSpec((1, tm, kvw), cur_row),
                  pl.BlockSpec((1, w, kvw), before),
                  pl.BlockSpec((1, tm, kvw), cur_row),
                  pl.BlockSpec(bias_b.shape, lambda s: (0, 0, 0, 0)),
                  _layer_spec(w_out, mix_layer),
                  pl.BlockSpec((1, d), const2),
                  pl.BlockSpec((1, 1, d), prev_b),
                  pl.BlockSpec((1, tm, d), prev_row),
                  pl.BlockSpec((1, d), const2),
                  pl.BlockSpec((1, 1, d), prev_b),
                  pl.BlockSpec((1, 1, d), prev_b),
                  pl.BlockSpec((1, 1, d), prev_b),
                  _layer_spec(w1, layer),
                  _layer_spec(w3, layer),
                  _layer_spec(w2, layer),
                  pl.BlockSpec((1, d), const2)],
        out_specs=pl.BlockSpec((1, tm, d), prev_row),
        out_shape=jax.ShapeDtypeStruct((bsz, L, d), F32),
        scratch_shapes=[pltpu.VMEM((2, tm, HD), BF16)],
        compiler_params=_cparams("arbitrary"),
        name="b_attn_ffn",
    )(sinks.reshape(1, N_HEADS), q, k, k, v, v, bias_b, w_out, b_out, g1, x, g, sh, sc, gate,
      w1, w3, w2, final_g)


def _out_ffn(o, w_out, mix_layer, b_out, g1, x, g, sh, sc, gate, w1, w3, w2, layer, final_g,
             final_norm, tm=512):
    bsz, L, d = x.shape
    row = lambda b_, i: (b_, i, 0)
    per_b = lambda b_, i: (b_, 0, 0)
    const2 = lambda b_, i: (0, 0)
    return pl.pallas_call(
        functools.partial(_out_ffn_kernel, final_norm=final_norm),
        grid=(bsz, L // tm),
        in_specs=[pl.BlockSpec((1, tm, HD), row),
                  _layer_spec(w_out, mix_layer),
                  pl.BlockSpec((1, d), const2),
                  pl.BlockSpec((1, 1, d), per_b),
                  pl.BlockSpec((1, tm, d), row),
                  pl.BlockSpec((1, d), const2),
                  pl.BlockSpec((1, 1, d), per_b),
                  pl.BlockSpec((1, 1, d), per_b),
                  pl.BlockSpec((1, 1, d), per_b),
                  _layer_spec(w1, layer),
                  _layer_spec(w3, layer),
                  _layer_spec(w2, layer),
                  pl.BlockSpec((1, d), const2)],
        out_specs=pl.BlockSpec((1, tm, d), row),
        out_shape=jax.ShapeDtypeStruct((bsz, L, d), F32),
        compiler_params=_cparams("parallel", "parallel"),
        name="out_ffn",
    )(o, w_out, b_out, g1, x, g, sh, sc, gate, w1, w3, w2, final_g)


def kernel(x, c, rel_bias, w_ada, b_ada, norm_mix_g, norm_ffn_g, a_w_in, a_kv_norm_g, a_w_uk,
           a_w_uv, a_idx_k_g, a_idx_k_b, a_w_out, b_w_in, b_b_in, b_sinks, b_w_out, b_b_out,
           ffn_w1, ffn_w3, ffn_w2, norm_final_g):
    bsz, L, d = x.shape
    depth = w_ada.shape[0]
    topk = min(INDEX_TOPK, L // 4)

    mod = _adaln(c, w_ada, b_ada)

    r = jnp.arange(BIAS_BLK)
    ids_a = jnp.stack([_t5_bucket(r[:, None] - r[None, :]),
                       _t5_bucket(r[:, None] - r[None, :] + BIAS_BLK)])
    bias_a = _bias_tiles(rel_bias, ids_a, shift_far=True)
    dist_b = jnp.arange(WINDOW)[:, None] + WINDOW - jnp.arange(2 * WINDOW)[None, :]
    in_window = (dist_b >= 0) & (dist_b < WINDOW)
    ids_b = jnp.where(in_window, _t5_bucket(dist_b), -1)
    ids_b0 = jnp.where(jnp.arange(2 * WINDOW)[None, :] >= WINDOW, ids_b, -1)
    bias_b = _bias_tiles(rel_bias, jnp.stack([ids_b, ids_b0]), shift_far=False)
    pairs = N_HEADS // KV_HEADS // 2
    bias_b = bias_b.reshape(KV_HEADS, pairs, 2, 2, WINDOW, 2 * WINDOW).transpose(3, 0, 1, 4, 2, 5)
    bias_b = bias_b.reshape(2, KV_HEADS, pairs * WINDOW, 4 * WINDOW)

    a_w_main = jnp.pad(a_w_in[:, :, :A_MAIN], ((0, 0), (0, 0), (0, A_MAIN_PAD - A_MAIN))).astype(BF16)
    a_w_wi_t = jnp.pad(a_w_in[:, :, A_WI0:A_WI0 + IDX_HEADS].transpose(0, 2, 1),
                       ((0, 0), (0, 16 - IDX_HEADS), (0, 0))).astype(BF16)
    a_w_uk, a_w_uv, a_w_out, b_w_in, b_w_out, ffn_w1, ffn_w3, ffn_w2 = [
        w.astype(BF16) for w in (jnp.swapaxes(_pair_blocks(a_w_uk), -1, -2),
                                 _pair_blocks(a_w_uv), a_w_out, b_w_in,
                                 b_w_out, ffn_w1, ffn_w3, ffn_w2)]

    zero_bias = jnp.zeros((1, d), F32)
    for i in range(depth):
        sh1, sc1, g1, sh2, sc2, g2 = [m.reshape(bsz, 1, d) for m in jnp.split(mod[i], 6, axis=-1)]
        jm = i // 2
        ffn_args = (g1, x, norm_ffn_g[i][None], sh2, sc2, g2, ffn_w1, ffn_w3, ffn_w2, i,
                    norm_final_g[None], i == depth - 1)
        if i % 2 == 0:
            q, ckv, ckvt, qi, ki, wit = _a_inproj(
                x, norm_mix_g[i][None], sh1, sc1, a_w_main, a_w_wi_t, jm, a_kv_norm_g[jm][None],
                a_idx_k_g[jm][None], a_idx_k_b[jm][None])
            mask = _a_index(qi, wit, ki, topk)
            o = _a_attn(q, ckv, ckvt, mask, a_w_uk, a_w_uv, jm, bias_a)
            x = _out_ffn(o, a_w_out, jm, zero_bias, *ffn_args)
        else:
            q, k, v = _b_inproj(x, norm_mix_g[i][None], sh1, sc1, b_w_in, jm, b_b_in[jm][None])
            x = _b_attn_ffn(q, k, v, b_sinks[jm], bias_b, b_w_out, jm, b_b_out[jm][None], *ffn_args)
    return x
```
